```python
import jax, jax.numpy as jnp
from jax import lax
import numpy as np

D_MODEL = 1024
BATCH = 16
SEQ = 2048
DEPTH = 1

CHUNK = 64
EPS = 1e-6
N_HEADS = 8
HEAD_DIM = 64
ATTN_WIDTH = N_HEADS * HEAD_DIM
Q_LORA = 256
KV_LORA = 128
IDX_HEADS = 8
IDX_DIM = 32
TOPK_MAX = 256
Q_BLOCK = 128
ATTN_SCALE = KV_LORA ** -0.5
IDX_SCALE = (IDX_HEADS * IDX_DIM) ** -0.5
SGU_CHUNK = 128
SGU_GROUPS = 8
SGU_WIDTH = 512
SGU_GROUP_DIM = SGU_WIDTH // SGU_GROUPS
N_EXPERTS = 32
TOP_K = 4
D_FF = D_MODEL
SWIGLU_LIMIT = 7.0
SWIGLU_ALPHA = 1.702
EXPERT_BLOCK = 512
IN_SPLITS = (Q_LORA, KV_LORA, IDX_DIM, IDX_HEADS, SGU_WIDTH, SGU_WIDTH, D_MODEL, D_MODEL)
D_IN = Q_LORA + KV_LORA + IDX_DIM + IDX_HEADS + 2 * SGU_WIDTH + 2 * D_MODEL

kernel_name = 'hybrid_dsa_gmlp_moe_chunk_causal'


def rms_norm(x, g):
    xf = x.astype(jnp.float32)
    y = xf * lax.rsqrt(jnp.mean(xf * xf, axis=-1, keepdims=True) + EPS)
    return (y * g.astype(jnp.float32)).astype(x.dtype)


def layer_norm(x, g, b):
    xf = x.astype(jnp.float32)
    mu = jnp.mean(xf, axis=-1, keepdims=True)
    var = jnp.mean(jnp.square(xf - mu), axis=-1, keepdims=True)
    y = (xf - mu) * lax.rsqrt(var + EPS)
    return (y * g.astype(jnp.float32) + b.astype(jnp.float32)).astype(x.dtype)


def split_cols(z, sizes):
    outs, off = [], 0
    for s in sizes:
        outs.append(z[..., off:off + s])
        off += s
    return outs


def dsa_mla(c_q, c_kv, k_idx, w_idx, w_uq, w_uv, w_q_idx):
    B, S, _ = c_q.shape
    topk = min(TOPK_MAX, S // 4)
    q = (c_q @ w_uq).reshape(B, S, N_HEADS, KV_LORA)
    q_idx = (c_q @ w_q_idx).reshape(B, S, IDX_HEADS, IDX_DIM)
    key_chunk = jnp.arange(S) // CHUNK
    gather = jax.vmap(lambda c, j: c[j])

    def block(i):
        start = i * Q_BLOCK
        qb = lax.dynamic_slice_in_dim(q, start, Q_BLOCK, axis=1)
        qib = lax.dynamic_slice_in_dim(q_idx, start, Q_BLOCK, axis=1)
        wb = lax.dynamic_slice_in_dim(w_idx, start, Q_BLOCK, axis=1)
        q_chunk = (start + jnp.arange(Q_BLOCK)) // CHUNK
        allowed = key_chunk[None, :] <= q_chunk[:, None]
        dots = jnp.einsum('bthd,bsd->bths', qib, k_idx).astype(jnp.float32)
        score = jnp.einsum('bth,bths->bts', wb.astype(jnp.float32), jax.nn.relu(dots))
        score = jnp.where(allowed[None], score, -jnp.inf)
        _, idx = lax.top_k(score, topk)
        valid = key_chunk[idx] <= q_chunk[None, :, None]
        kv_sel = gather(c_kv, idx)
        s = jnp.einsum('bthc,btkc->bthk', qb, kv_sel).astype(jnp.float32) * ATTN_SCALE
        s = jnp.where(valid[:, :, None, :], s, -jnp.inf)
        p = jax.nn.softmax(s, axis=-1).astype(c_kv.dtype)
        o = jnp.einsum('bthk,btkc->bthc', p, kv_sel)
        o = jnp.einsum('bthc,hcd->bthd', o, w_uv)
        return o.reshape(B, Q_BLOCK, ATTN_WIDTH)

    out = lax.map(block, jnp.arange(S // Q_BLOCK))
    return out.transpose(1, 0, 2, 3).reshape(B, S, ATTN_WIDTH)


def spatial_gating(u, v, g_sgu, b_sgu, w_spatial, b_spatial):
    B, S, _ = u.shape
    u = jax.nn.gelu(u)
    v = layer_norm(jax.nn.gelu(v), g_sgu, b_sgu)
    vb = v.reshape(B, S // SGU_CHUNK, SGU_CHUNK, SGU_GROUPS, SGU_GROUP_DIM)
    pos_chunk = jnp.arange(SGU_CHUNK) // CHUNK
    mask = pos_chunk[:, None] >= pos_chunk[None, :]
    ws = jnp.where(mask[None], w_spatial, jnp.zeros_like(w_spatial))
    s = jnp.einsum('gij,bnjgc->bnigc', ws, vb) + b_spatial.T[None, None, :, :, None]
    return u * s.reshape(B, S, SGU_WIDTH)


def moe(h, w_router, b_router, w_gu, b_gu, w_down, b_down):
    B, S, D = h.shape
    N = B * S
    xf = h.reshape(N, D)
    logits = (xf @ w_router).astype(jnp.float32) + b_router.astype(jnp.float32)
    top_vals, top_idx = lax.top_k(logits, TOP_K)
    gates = jax.nn.softmax(top_vals, axis=-1).astype(h.dtype)
    NK = N * TOP_K
    flat_e = top_idx.reshape(NK)
    flat_tok = jnp.arange(NK, dtype=jnp.int32) // TOP_K
    flat_w = gates.reshape(NK)
    order = jnp.argsort(flat_e)
    sorted_e = flat_e[order]
    counts = jnp.bincount(flat_e, length=N_EXPERTS)
    padded = (counts + EXPERT_BLOCK - 1) // EXPERT_BLOCK * EXPERT_BLOCK
    padded_end = jnp.cumsum(padded)
    padded_start = padded_end - padded
    group_start = jnp.cumsum(counts) - counts
    dest = padded_start[sorted_e] + jnp.arange(NK, dtype=jnp.int32) - group_start[sorted_e]
    nb = -(-(NK + N_EXPERTS * EXPERT_BLOCK) // EXPERT_BLOCK)
    P = nb * EXPERT_BLOCK
    tok_buf = jnp.zeros((P,), jnp.int32).at[dest].set(flat_tok[order])
    w_buf = jnp.zeros((P,), h.dtype).at[dest].set(flat_w[order])
    block_e = jnp.minimum(
        jnp.searchsorted(padded_end, jnp.arange(nb) * EXPERT_BLOCK, side='right'),
        N_EXPERTS - 1)

    def expert_block(args):
        tok, w, e = args
        xb = xf[tok]
        gu = xb @ w_gu[e] + b_gu[e]
        gate, up = gu[:, :D_FF], gu[:, D_FF:]
        gate = jnp.minimum(gate, SWIGLU_LIMIT)
        up = jnp.clip(up, -SWIGLU_LIMIT, SWIGLU_LIMIT)
        act = (up + 1.0) * (gate * jax.nn.sigmoid(SWIGLU_ALPHA * gate))
        return (act @ w_down[e] + b_down[e]) * w[:, None]

    out = lax.map(expert_block, (tok_buf.reshape(nb, EXPERT_BLOCK),
                                 w_buf.reshape(nb, EXPERT_BLOCK), block_e))
    y = jnp.zeros((N, D), h.dtype).at[tok_buf].add(out.reshape(P, D))
    return y.reshape(B, S, D)


def setup_inputs(seed: int = 0) -> dict:
    key = jax.random.key(seed)
    ks = jax.random.split(key, 26)
    f32 = jnp.float32

    def nrm(k, shape, scale):
        return jax.random.normal(k, shape, f32) * scale

    def gain(k, shape):
        return 1.0 + 0.05 * jax.random.normal(k, shape, f32)

    L, D, E, F = DEPTH, D_MODEL, N_EXPERTS, D_FF
    return {
        'x': nrm(ks[0], (BATCH, SEQ, D), 1.0),
        'g_mix': gain(ks[1], (L, D)),
        'w_in': nrm(ks[2], (L, D, D_IN), D ** -0.5),
        'g_cq': gain(ks[3], (L, Q_LORA)),
        'g_ckv': gain(ks[4], (L, KV_LORA)),
        'w_uq': nrm(ks[5], (L, Q_LORA, N_HEADS * KV_LORA), Q_LORA ** -0.5),
        'w_uv': nrm(ks[6], (L, N_HEADS, KV_LORA, HEAD_DIM), KV_LORA ** -0.5),
        'w_q_idx': nrm(ks[7], (L, Q_LORA, IDX_HEADS * IDX_DIM), Q_LORA ** -0.5),
        'g_kidx': gain(ks[8], (L, IDX_DIM)),
        'b_kidx': nrm(ks[9], (L, IDX_DIM), 0.02),
        'g_sgu': gain(ks[10], (L, SGU_WIDTH)),
        'b_sgu': nrm(ks[11], (L, SGU_WIDTH), 0.02),
        'w_spatial': nrm(ks[12], (L, SGU_GROUPS, SGU_CHUNK, SGU_CHUNK), SGU_CHUNK ** -0.5),
        'b_spatial': 1.0 + nrm(ks[13], (L, SGU_GROUPS, SGU_CHUNK), 0.1),
        'w_br_a': nrm(ks[14], (L, ATTN_WIDTH, D), ATTN_WIDTH ** -0.5),
        'w_br_b': nrm(ks[15], (L, SGU_WIDTH, D), SGU_WIDTH ** -0.5),
        'w_o': nrm(ks[16], (L, D, D), D ** -0.5),
        'g_ffn': gain(ks[17], (L, D)),
        'w_router': nrm(ks[18], (L, D, E), D ** -0.5),
        'b_router': nrm(ks[19], (L, E), 0.01),
        'w_gu': nrm(ks[20], (L, E, D, 2 * F), D ** -0.5),
        'b_gu': nrm(ks[21], (L, E, 2 * F), 0.01),
        'w_down': nrm(ks[22], (L, E, F, D), F ** -0.5),
        'b_down': nrm(ks[23], (L, E, D), 0.01),
        'g_final': gain(ks[24], (D,)),
    }


def reference(x, g_mix, w_in, g_cq, g_ckv, w_uq, w_uv, w_q_idx, g_kidx, b_kidx,
              g_sgu, b_sgu, w_spatial, b_spatial, w_br_a, w_br_b, w_o, g_ffn,
              w_router, b_router, w_gu, b_gu, w_down, b_down, g_final):
    for l in range(DEPTH):
        h = rms_norm(x, g_mix[l])
        z = h @ w_in[l]
        c_q, c_kv, k_idx, w_idx, u, v, ga, gb = split_cols(z, IN_SPLITS)
        c_q = rms_norm(c_q, g_cq[l])
        c_kv = rms_norm(c_kv, g_ckv[l])
        k_idx = layer_norm(k_idx, g_kidx[l], b_kidx[l])
        w_idx = w_idx * IDX_SCALE
        ya = dsa_mla(c_q, c_kv, k_idx, w_idx, w_uq[l], w_uv[l], w_q_idx[l])
        yb = spatial_gating(u, v, g_sgu[l], b_sgu[l], w_spatial[l], b_spatial[l])
        merged = (jax.nn.sigmoid(ga) * (ya @ w_br_a[l])
                  + jax.nn.sigmoid(gb) * (yb @ w_br_b[l]))
        x = x + merged @ w_o[l]
        x = x + moe(rms_norm(x, g_ffn[l]), w_router[l], b_router[l],
                    w_gu[l], b_gu[l], w_down[l], b_down[l])
    return rms_norm(x, g_final)
```

```python
import functools

import jax
import jax.numpy as jnp
from jax import lax
from jax.experimental import pallas as pl
from jax.experimental.pallas import tpu as pltpu

EPS = 1e-6
CHUNK = 64
N_HEADS = 8
HEAD_DIM = 64
Q_LORA = 256
KV_LORA = 128
IDX_HEADS = 8
IDX_DIM = 32
TOPK_MAX = 256
Q_BLOCK = 128
ATTN_SCALE = KV_LORA ** -0.5
IDX_SCALE = (IDX_HEADS * IDX_DIM) ** -0.5
SGU_CHUNK = 128
SGU_GROUPS = 8
SGU_WIDTH = 512
SGU_GROUP_DIM = SGU_WIDTH // SGU_GROUPS
N_EXPERTS = 32
TOP_K = 4
SWIGLU_LIMIT = 7.0
SWIGLU_ALPHA = 1.702
EXPERT_BLOCK = 512

LANES = 128
VMEM_LIMIT = 56 * 1024 * 1024
INT_MIN = -(2 ** 31)

BF16 = jnp.bfloat16
F32 = jnp.float32


def _dot(a, b):
    return jnp.dot(a, b, preferred_element_type=F32)


def _dot_nt(a, b):
    return lax.dot_general(a, b, (((1,), (1,)), ((), ())), preferred_element_type=F32)


def _rms(x, g):
    return x * lax.rsqrt(jnp.mean(x * x, axis=-1, keepdims=True) + EPS) * g


def _layer_norm(x, g, b):
    mu = jnp.mean(x, axis=-1, keepdims=True)
    xc = x - mu
    var = jnp.mean(xc * xc, axis=-1, keepdims=True)
    return xc * lax.rsqrt(var + EPS) * g + b


C_Q = 0
C_KV = C_Q + Q_LORA
C_KI = C_KV + KV_LORA
C_WI = C_KI + IDX_HEADS * IDX_DIM
C_U = C_WI + LANES
C_V = C_U + SGU_WIDTH
D_IN_P_BASE = C_V + SGU_WIDTH


def _inproj_kernel(x_ref, gmix_ref, w_ref, gcq_ref, gckv_ref, gki_ref, bki_ref,
                   gsgu_ref, bsgu_ref, wsp_ref, bsp_ref, wbrb_ref,
                   cq_ref, ckv_ref, ki_ref, wi_ref, gas_ref, pb_ref):
    tm, d = x_ref.shape
    c_ga = D_IN_P_BASE
    c_gb = c_ga + d
    h = _rms(x_ref[...], gmix_ref[...]).astype(BF16)

    def proj(lo, width):
        return _dot(h, w_ref[:, lo:lo + width])

    cq_ref[...] = _rms(proj(C_Q, Q_LORA), gcq_ref[...]).astype(BF16)
    ckv_ref[...] = _rms(proj(C_KV, KV_LORA), gckv_ref[...]).astype(BF16)
    ki_ref[...] = _layer_norm(proj(C_KI, IDX_HEADS * IDX_DIM), gki_ref[...],
                              bki_ref[...]).astype(BF16)
    wi_ref[...] = proj(C_WI, LANES) * IDX_SCALE
    gas_ref[...] = jax.nn.sigmoid(proj(c_ga, d)).astype(BF16)

    u = jax.nn.gelu(proj(C_U, SGU_WIDTH))
    v = _layer_norm(jax.nn.gelu(proj(C_V, SGU_WIDTH)), gsgu_ref[...], bsgu_ref[...])

    row = lax.broadcasted_iota(jnp.int32, (SGU_CHUNK, 2 * SGU_CHUNK), 0)
    col = lax.broadcasted_iota(jnp.int32, (SGU_CHUNK, 2 * SGU_CHUNK), 1) % SGU_CHUNK
    causal = (row // CHUNK) >= (col // CHUNK)
    lane = lax.broadcasted_iota(jnp.int32, (SGU_CHUNK, LANES), 1)
    left = lane < SGU_GROUP_DIM
    n_tiles = SGU_WIDTH // LANES
    ws = [jnp.where(causal, wsp_ref[j], 0.0).astype(BF16) for j in range(n_tiles)]
    yb_chunks = []
    for c in range(tm // SGU_CHUNK):
        tiles = []
        for j in range(n_tiles):
            blk = v[c * SGU_CHUNK:(c + 1) * SGU_CHUNK, j * LANES:(j + 1) * LANES]
            stacked = jnp.concatenate(
                [jnp.where(left, blk, 0.0), jnp.where(left, 0.0, blk)], axis=0).astype(BF16)
            tiles.append(_dot(ws[j], stacked))
        s = jnp.concatenate(tiles, axis=1) + bsp_ref[...]
        yb_chunks.append(u[c * SGU_CHUNK:(c + 1) * SGU_CHUNK, :] * s)
    yb = jnp.concatenate(yb_chunks, axis=0).astype(BF16)
    pb_ref[...] = (jax.nn.sigmoid(proj(c_gb, d)) * _dot(yb, wbrb_ref[...])).astype(BF16)


def _inproj(x2, g_mix, w_in_p, g_cq, g_ckv, g_ki, b_ki, g_sgu, b_sgu, ws_pair, bsp, w_br_b,
            tm):
    n, d = x2.shape
    d_in_p = w_in_p.shape[1]

    def full(shape):
        return pl.BlockSpec(shape, lambda i: (0,) * len(shape))

    def rows(width):
        return pl.BlockSpec((tm, width), lambda i: (i, 0))

    return pl.pallas_call(
        _inproj_kernel,
        grid=(n // tm,),
        in_specs=[rows(d), full((1, d)), full((d, d_in_p)), full((1, Q_LORA)),
                  full((1, KV_LORA)), full((1, IDX_HEADS * IDX_DIM)),
                  full((1, IDX_HEADS * IDX_DIM)), full((1, SGU_WIDTH)), full((1, SGU_WIDTH)),
                  full(ws_pair.shape), full(bsp.shape), full(w_br_b.shape)],
        out_specs=[rows(Q_LORA), rows(KV_LORA), rows(IDX_HEADS * IDX_DIM), rows(LANES),
                   rows(d), rows(d)],
        out_shape=[jax.ShapeDtypeStruct((n, Q_LORA), BF16),
                   jax.ShapeDtypeStruct((n, KV_LORA), BF16),
                   jax.ShapeDtypeStruct((n, IDX_HEADS * IDX_DIM), BF16),
                   jax.ShapeDtypeStruct((n, LANES), F32),
                   jax.ShapeDtypeStruct((n, d), BF16),
                   jax.ShapeDtypeStruct((n, d), BF16)],
        compiler_params=pltpu.CompilerParams(dimension_semantics=("parallel",),
                                             vmem_limit_bytes=VMEM_LIMIT),
        name="inproj",
    )(x2, g_mix, w_in_p, g_cq, g_ckv, g_ki, b_ki, g_sgu, b_sgu, ws_pair, bsp, w_br_b)


def _attn_body(s_eff, topk, cq_ref, ckv_ref, ki_ref, wi_ref, wuq_ref, wqi_ref, wuv_ref,
               ya_ref, q_scr, qi_scr, wcol_scr, key_scr, bias_scr, acc_scr):
    i = pl.program_id(1)
    cq = cq_ref[0]
    q = _dot(cq, wuq_ref[...])
    qi = _dot(cq, wqi_ref[...]).astype(BF16)
    wi = wi_ref[0]
    head_of_lane = lax.broadcasted_iota(jnp.int32, qi.shape, 1) // IDX_DIM
    for h in range(N_HEADS):
        q_scr[h] = q[:, h * KV_LORA:(h + 1) * KV_LORA].astype(BF16)
    for h in range(IDX_HEADS):
        qi_scr[h] = jnp.where(head_of_lane == h, qi, jnp.zeros_like(qi))
        wcol_scr[h] = wi[:, h:h + 1]

    bias_scr[...] = jnp.zeros_like(bias_scr)

    def score_step(h, carry):
        dots = _dot_nt(qi_scr[h], ki_ref[0, 0:s_eff, :])
        bias_scr[...] += wcol_scr[h] * jnp.maximum(dots, 0.0)
        return carry

    lax.fori_loop(0, IDX_HEADS, score_step, 0)

    score = bias_scr[...]
    score = jnp.where(score == 0.0, 0.0, score)
    bits = lax.bitcast_convert_type(score, jnp.int32)
    key = jnp.where(bits < 0, bits ^ jnp.int32(0x7FFFFFFF), bits)
    rowq = lax.broadcasted_iota(jnp.int32, (Q_BLOCK, s_eff), 0)
    colk = lax.broadcasted_iota(jnp.int32, (Q_BLOCK, s_eff), 1)
    allowed = (colk // CHUNK) <= ((i * Q_BLOCK + rowq) // CHUNK)
    key_scr[...] = jnp.where(allowed, key, jnp.int32(INT_MIN))

    def count_ge(thr):
        return jnp.sum(jnp.where(key_scr[...] >= thr, 1.0, 0.0), axis=1, keepdims=True)

    kf = float(topk)
    t0 = jnp.where(count_ge(jnp.zeros((Q_BLOCK, 1), jnp.int32)) >= kf,
                   jnp.int32(0), jnp.int32(INT_MIN))

    def search_step(it, t):
        cand = t + jnp.left_shift(jnp.int32(1), jnp.int32(30) - it)
        return jnp.where(count_ge(cand) >= kf, cand, t)

    thr = lax.fori_loop(0, 31, search_step, t0)

    sel = key_scr[...] >= thr
    n_ge = jnp.sum(jnp.where(sel, 1.0, 0.0), axis=1, keepdims=True)
    bias_scr[...] = jnp.where(sel & allowed, 0.0, -jnp.inf)
    tie_rows = (n_ge > kf) & (thr > jnp.int32(INT_MIN))
    any_tie = jnp.max(jnp.where(tie_rows, 1.0, 0.0))

    @pl.when(any_tie > 0.0)
    def _():
        kv = key_scr[...]
        gt = kv > thr
        eq = kv == thr
        need = kf - jnp.sum(jnp.where(gt, 1.0, 0.0), axis=1, keepdims=True)
        nbits = (s_eff - 1).bit_length()

        def tie_step(it, j):
            cand = j + jnp.left_shift(jnp.int32(1), jnp.int32(nbits - 1) - it)
            cnt = jnp.sum(jnp.where((key_scr[...] == thr) & (colk < cand), 1.0, 0.0),
                          axis=1, keepdims=True)
            return jnp.where(cnt < need, cand, j)

        jmax = lax.fori_loop(0, nbits, tie_step, jnp.zeros((Q_BLOCK, 1), jnp.int32))
        keep = gt | (eq & (colk <= jmax))
        bias_scr[...] = jnp.where(keep & allowed, 0.0, -jnp.inf)

    acc_scr[...] = jnp.zeros_like(acc_scr)

    def head_step(h, carry):
        kv_lat = ckv_ref[0, 0:s_eff, :]
        s = _dot_nt(q_scr[h], kv_lat) * ATTN_SCALE + bias_scr[...]
        m = jnp.max(s, axis=1, keepdims=True)
        p = jnp.exp(s - m)
        l = jnp.sum(p, axis=1, keepdims=True)
        o = _dot(p.astype(BF16), kv_lat) / l
        acc_scr[...] += _dot(o.astype(BF16), wuv_ref[h])
        return carry

    lax.fori_loop(0, N_HEADS, head_step, 0)
    ya_ref[0] = acc_scr[...].astype(BF16)


def _attn_kernel(n_buckets, topk, cq_ref, ckv_ref, ki_ref, wi_ref, wuq_ref, wqi_ref, wuv_ref,
                 ya_ref, q_scr, qi_scr, wcol_scr, key_scr, bias_scr, acc_scr):
    s = ckv_ref.shape[1]
    nq = s // Q_BLOCK
    per = nq // n_buckets
    i = pl.program_id(1)
    for k in range(n_buckets):
        s_eff = (k + 1) * per * Q_BLOCK

        @pl.when((i >= k * per) & (i < (k + 1) * per))
        def _(s_eff=s_eff):
            _attn_body(s_eff, topk, cq_ref, ckv_ref, ki_ref, wi_ref, wuq_ref, wqi_ref,
                       wuv_ref, ya_ref, q_scr, qi_scr, wcol_scr,
                       key_scr.at[:, 0:s_eff], bias_scr.at[:, 0:s_eff], acc_scr)


def _attn(cq, ckv, ki, wi, w_uq, w_qi, w_uv_pad, n_buckets):
    b, s, _ = cq.shape
    nq = s // Q_BLOCK
    topk = min(TOPK_MAX, s // 4)
    aw = N_HEADS * HEAD_DIM

    def full(shape):
        return pl.BlockSpec(shape, lambda bi, i: (0,) * len(shape))

    return pl.pallas_call(
        functools.partial(_attn_kernel, n_buckets, topk),
        grid=(b, nq),
        in_specs=[pl.BlockSpec((1, Q_BLOCK, Q_LORA), lambda bi, i: (bi, i, 0)),
                  pl.BlockSpec((1, s, KV_LORA), lambda bi, i: (bi, 0, 0)),
                  pl.BlockSpec((1, s, IDX_HEADS * IDX_DIM), lambda bi, i: (bi, 0, 0)),
                  pl.BlockSpec((1, Q_BLOCK, LANES), lambda bi, i: (bi, i, 0)),
                  full(w_uq.shape), full(w_qi.shape), full(w_uv_pad.shape)],
        out_specs=pl.BlockSpec((1, Q_BLOCK, aw), lambda bi, i: (bi, i, 0)),
        out_shape=jax.ShapeDtypeStruct((b, s, aw), BF16),
        scratch_shapes=[pltpu.VMEM((N_HEADS, Q_BLOCK, KV_LORA), BF16),
                        pltpu.VMEM((IDX_HEADS, Q_BLOCK, IDX_HEADS * IDX_DIM), BF16),
                        pltpu.VMEM((IDX_HEADS, Q_BLOCK, 1), F32),
                        pltpu.VMEM((Q_BLOCK, s), jnp.int32),
                        pltpu.VMEM((Q_BLOCK, s), F32),
                        pltpu.VMEM((Q_BLOCK, aw), F32)],
        compiler_params=pltpu.CompilerParams(dimension_semantics=("parallel", "parallel"),
                                             vmem_limit_bytes=VMEM_LIMIT),
        name="attn",
    )(cq, ckv, ki, wi, w_uq, w_qi, w_uv_pad)


def _merge_kernel(x_ref, ya_ref, gas_ref, pb_ref, wbra_ref, wo_ref, gffn_ref,
                  wr_hi_ref, wr_lo_ref, br_ref,
                  x1_ref, h2_ref, idx_ref, gate_ref, rank_ref, cnt_ref, base_scr):
    step = pl.program_id(0)
    tm = x_ref.shape[0]

    @pl.when(step == 0)
    def _():
        base_scr[...] = jnp.zeros_like(base_scr)

    a = _dot(ya_ref[...], wbra_ref[...])
    merged = gas_ref[...].astype(F32) * a + pb_ref[...].astype(F32)
    x1 = x_ref[...] + _dot(merged.astype(BF16), wo_ref[...])
    x1_ref[...] = x1
    h2 = _rms(x1, gffn_ref[...])
    h2_ref[...] = h2

    h_hi = h2.astype(BF16)
    h_lo = (h2 - h_hi.astype(F32)).astype(BF16)
    logits = (_dot(h_hi, wr_hi_ref[...]) + _dot(h_hi, wr_lo_ref[...])
              + _dot(h_lo, wr_hi_ref[...]) + br_ref[...])

    lane_e = lax.broadcasted_iota(jnp.int32, (tm, N_EXPERTS), 1).astype(F32)
    lane_o = lax.broadcasted_iota(jnp.int32, (tm, LANES), 1)
    work = logits
    vals, idxs = [], []
    onehot = jnp.zeros((tm, N_EXPERTS), F32)
    for _ in range(TOP_K):
        m = jnp.max(work, axis=1, keepdims=True)
        idx = jnp.min(jnp.where(work == m, lane_e, float(N_EXPERTS)), axis=1, keepdims=True)
        hit = lane_e == idx
        onehot = onehot + jnp.where(hit, 1.0, 0.0)
        work = jnp.where(hit, -jnp.inf, work)
        vals.append(m)
        idxs.append(idx)
    exps = [jnp.exp(v - vals[0]) for v in vals]
    denom = exps[0] + exps[1] + exps[2] + exps[3]

    r = lax.broadcasted_iota(jnp.int32, (tm, tm), 0)
    c = lax.broadcasted_iota(jnp.int32, (tm, tm), 1)
    tri = jnp.where(c < r, 1.0, 0.0).astype(BF16)
    rank_full = _dot(tri, onehot.astype(BF16)) + base_scr[...]

    idx_out = jnp.zeros((tm, LANES), jnp.int32)
    gate_out = jnp.zeros((tm, LANES), F32)
    rank_out = jnp.zeros((tm, LANES), jnp.int32)
    for k in range(TOP_K):
        rk = jnp.sum(jnp.where(lane_e == idxs[k], rank_full, 0.0), axis=1, keepdims=True)
        idx_out = jnp.where(lane_o == k, idxs[k].astype(jnp.int32), idx_out)
        gate_out = jnp.where(lane_o == k, exps[k] / denom, gate_out)
        rank_out = jnp.where(lane_o == k, rk.astype(jnp.int32), rank_out)
    idx_ref[...] = idx_out
    gate_ref[...] = gate_out
    rank_ref[...] = rank_out

    base_scr[...] = base_scr[...] + jnp.sum(onehot, axis=0, keepdims=True)
    cnt_ref[...] = base_scr[...].astype(jnp.int32)


def _merge(x2, ya, gas, pb, w_br_a, w_o, g_ffn, wr_hi, wr_lo, b_router, tm):
    n, d = x2.shape
    aw = ya.shape[1]

    def full(shape):
        return pl.BlockSpec(shape, lambda i: (0,) * len(shape))

    def rows(width):
        return pl.BlockSpec((tm, width), lambda i: (i, 0))

    return pl.pallas_call(
        _merge_kernel,
        grid=(n // tm,),
        in_specs=[rows(d), rows(aw), rows(d), rows(d), full(w_br_a.shape), full(w_o.shape),
                  full((1, d)), full(wr_hi.shape), full(wr_lo.shape), full((1, N_EXPERTS))],
        out_specs=[rows(d), rows(d), rows(LANES), rows(LANES), rows(LANES),
                   full((1, N_EXPERTS))],
        out_shape=[jax.ShapeDtypeStruct((n, d), F32),
                   jax.ShapeDtypeStruct((n, d), F32),
                   jax.ShapeDtypeStruct((n, LANES), jnp.int32),
                   jax.ShapeDtypeStruct((n, LANES), F32),
                   jax.ShapeDtypeStruct((n, LANES), jnp.int32),
                   jax.ShapeDtypeStruct((1, N_EXPERTS), jnp.int32)],
        scratch_shapes=[pltpu.VMEM((1, N_EXPERTS), F32)],
        compiler_params=pltpu.CompilerParams(dimension_semantics=("arbitrary",),
                                             vmem_limit_bytes=VMEM_LIMIT),
        name="merge_route",
    )(x2, ya, gas, pb, w_br_a, w_o, g_ffn, wr_hi, wr_lo, b_router)


def _row_copy(src_hbm, src_row, dst_vmem, dst_row, sem):
    return pltpu.make_async_copy(src_hbm.at[pl.ds(src_row, 1)], dst_vmem.at[pl.ds(dst_row, 1)],
                                 sem)


def _gather_rows(idx_hbm_row, src_hbm, dst_vmem, idx_smem, sem_idx, sem_rows, n_rows):
    cp = pltpu.make_async_copy(idx_hbm_row, idx_smem, sem_idx)
    cp.start()
    cp.wait()

    def issue(r, carry):
        _row_copy(src_hbm, idx_smem[r], dst_vmem, r, sem_rows).start()
        return carry

    lax.fori_loop(0, n_rows, issue, 0, unroll=8)

    def drain(r, carry):
        _row_copy(src_hbm, 0, dst_vmem, r, sem_rows).wait()
        return carry

    lax.fori_loop(0, n_rows, drain, 0, unroll=8)


def _dispatch_kernel(tok_ref, h2_ref, xs_ref, idx_smem, sem_idx, sem_rows):
    blk = pl.program_id(0)
    _gather_rows(tok_ref.at[blk], h2_ref, xs_ref, idx_smem, sem_idx, sem_rows, EXPERT_BLOCK)


def _dispatch(tok_buf2, h2):
    nb = tok_buf2.shape[0]
    d = h2.shape[1]
    return pl.pallas_call(
        _dispatch_kernel,
        grid=(nb,),
        in_specs=[pl.BlockSpec(memory_space=pl.ANY), pl.BlockSpec(memory_space=pl.ANY)],
        out_specs=pl.BlockSpec((EXPERT_BLOCK, d), lambda i: (i, 0)),
        out_shape=jax.ShapeDtypeStruct((nb * EXPERT_BLOCK, d), F32),
        scratch_shapes=[pltpu.SMEM((EXPERT_BLOCK,), jnp.int32),
                        pltpu.SemaphoreType.DMA, pltpu.SemaphoreType.DMA],
        compiler_params=pltpu.CompilerParams(dimension_semantics=("arbitrary",),
                                             vmem_limit_bytes=VMEM_LIMIT),
        name="dispatch",
    )(tok_buf2, h2)


def _expert_kernel(be_ref, xs_ref, wgu_ref, bgu_ref, wd_ref, bd_ref, y_ref):
    del be_ref
    f = wd_ref.shape[1]
    gu = _dot(xs_ref[...].astype(BF16), wgu_ref[0]) + bgu_ref[0]
    gate = jnp.minimum(gu[:, :f], SWIGLU_LIMIT)
    up = jnp.clip(gu[:, f:], -SWIGLU_LIMIT, SWIGLU_LIMIT)
    act = (up + 1.0) * (gate * jax.nn.sigmoid(SWIGLU_ALPHA * gate))
    y_ref[...] = _dot(act.astype(BF16), wd_ref[0]) + bd_ref[0]


def _experts(block_e, xs, w_gu, b_gu, w_down, b_down):
    p, d = xs.shape
    nb = p // EXPERT_BLOCK
    f2 = w_gu.shape[2]
    f = w_down.shape[1]
    grid_spec = pltpu.PrefetchScalarGridSpec(
        num_scalar_prefetch=1,
        grid=(nb,),
        in_specs=[pl.BlockSpec((EXPERT_BLOCK, d), lambda i, be: (i, 0)),
                  pl.BlockSpec((1, d, f2), lambda i, be: (be[i], 0, 0)),
                  pl.BlockSpec((1, 1, f2), lambda i, be: (be[i], 0, 0)),
                  pl.BlockSpec((1, f, d), lambda i, be: (be[i], 0, 0)),
                  pl.BlockSpec((1, 1, d), lambda i, be: (be[i], 0, 0))],
        out_specs=pl.BlockSpec((EXPERT_BLOCK, d), lambda i, be: (i, 0)),
    )
    return pl.pallas_call(
        _expert_kernel,
        grid_spec=grid_spec,
        out_shape=jax.ShapeDtypeStruct((p, d), F32),
        compiler_params=pltpu.CompilerParams(dimension_semantics=("arbitrary",),
                                             vmem_limit_bytes=VMEM_LIMIT),
        name="experts",
    )(block_e, xs, w_gu, b_gu, w_down, b_down)


def _combine_kernel(dest_ref, y_ref, x1_ref, gate_ref, gfin_ref, out_ref,
                    rows_scr, idx_smem, sem_idx, sem_rows):
    blk = pl.program_id(0)
    tm = x1_ref.shape[0]
    _gather_rows(dest_ref.at[blk], y_ref, rows_scr, idx_smem, sem_idx, sem_rows, TOP_K * tm)
    gates = gate_ref[...]
    acc = x1_ref[...]
    for k in range(TOP_K):
        acc = acc + gates[:, k:k + 1] * rows_scr[k * tm:(k + 1) * tm, :]
    out_ref[...] = _rms(acc, gfin_ref[...])


def _combine(dest_km, ybuf, x1, gates, g_final, tm):
    n, d = x1.shape
    return pl.pallas_call(
        _combine_kernel,
        grid=(n // tm,),
        in_specs=[pl.BlockSpec(memory_space=pl.ANY), pl.BlockSpec(memory_space=pl.ANY),
                  pl.BlockSpec((tm, d), lambda i: (i, 0)),
                  pl.BlockSpec((tm, LANES), lambda i: (i, 0)),
                  pl.BlockSpec((1, d), lambda i: (0, 0))],
        out_specs=pl.BlockSpec((tm, d), lambda i: (i, 0)),
        out_shape=jax.ShapeDtypeStruct((n, d), F32),
        scratch_shapes=[pltpu.VMEM((TOP_K * tm, d), F32),
                        pltpu.SMEM((TOP_K * tm,), jnp.int32),
                        pltpu.SemaphoreType.DMA, pltpu.SemaphoreType.DMA],
        compiler_params=pltpu.CompilerParams(dimension_semantics=("arbitrary",),
                                             vmem_limit_bytes=VMEM_LIMIT),
        name="combine",
    )(dest_km, ybuf, x1, gates, g_final)


def _pack_in_proj(w_in, d):
    offs = [0]
    for width in (Q_LORA, KV_LORA, IDX_DIM, IDX_HEADS, SGU_WIDTH, SGU_WIDTH, d, d):
        offs.append(offs[-1] + width)
    wq, wkv, wki, wwi, wu, wv, wga, wgb = [w_in[:, offs[j]:offs[j + 1]] for j in range(8)]
    wki_rep = jnp.tile(wki, (1, IDX_HEADS))
    wwi_pad = jnp.pad(wwi, ((0, 0), (0, LANES - IDX_HEADS)))
    return jnp.concatenate([wq, wkv, wki_rep, wwi_pad, wu, wv, wga, wgb], axis=1).astype(BF16)


def _layer(x, g_mix, w_in, g_cq, g_ckv, w_uq, w_uv, w_q_idx, g_kidx, b_kidx, g_sgu, b_sgu,
           w_spatial, b_spatial, w_br_a, w_br_b, w_o, g_ffn, w_router, b_router, w_gu, b_gu,
           w_down, b_down, g_final):
    b, s, d = x.shape
    n = b * s
    x2 = x.reshape(n, d)
    row = lambda v: v.reshape(1, -1).astype(F32)

    w_in_p = _pack_in_proj(w_in, d)
    ws_pair = w_spatial.reshape(SGU_GROUPS // 2, 2, SGU_CHUNK, SGU_CHUNK).transpose(
        0, 2, 1, 3).reshape(SGU_GROUPS // 2, SGU_CHUNK, 2 * SGU_CHUNK)
    bsp = jnp.repeat(b_spatial.T, SGU_GROUP_DIM, axis=1)
    cq, ckv, ki, wi, gas, pb = _inproj(
        x2, row(g_mix), w_in_p, row(g_cq), row(g_ckv), row(jnp.tile(g_kidx, IDX_HEADS)),
        row(jnp.tile(b_kidx, IDX_HEADS)), row(g_sgu), row(b_sgu), ws_pair, bsp,
        w_br_b.astype(BF16), tm=512)

    w_uv_pad = jnp.zeros((N_HEADS, KV_LORA, N_HEADS * HEAD_DIM), F32)
    for h in range(N_HEADS):
        w_uv_pad = w_uv_pad.at[h, :, h * HEAD_DIM:(h + 1) * HEAD_DIM].set(w_uv[h])
    n_buckets = 4 if (s // Q_BLOCK) % 4 == 0 else 1
    ya = _attn(cq.reshape(b, s, -1), ckv.reshape(b, s, -1), ki.reshape(b, s, -1),
               wi.reshape(b, s, -1), w_uq.astype(BF16), w_q_idx.astype(BF16),
               w_uv_pad.astype(BF16), n_buckets)

    wr_hi = w_router.astype(BF16)
    wr_lo = (w_router - wr_hi.astype(F32)).astype(BF16)
    x1, h2, idx_p, gate_p, rank_p, counts = _merge(
        x2, ya.reshape(n, -1), gas, pb, w_br_a.astype(BF16), w_o.astype(BF16), row(g_ffn),
        wr_hi, wr_lo, row(b_router), tm=512)

    counts = counts[0]
    top_idx = idx_p[:, :TOP_K]
    padded = (counts + EXPERT_BLOCK - 1) // EXPERT_BLOCK * EXPERT_BLOCK
    padded_end = jnp.cumsum(padded)
    padded_start = padded_end - padded
    dest = padded_start[top_idx] + rank_p[:, :TOP_K]
    nk = n * TOP_K
    nb = -(-(nk + N_EXPERTS * EXPERT_BLOCK) // EXPERT_BLOCK)
    tok = jnp.broadcast_to(jnp.arange(n, dtype=jnp.int32)[:, None], (n, TOP_K))
    tok_buf = jnp.zeros((nb * EXPERT_BLOCK,), jnp.int32).at[dest.reshape(-1)].set(
        tok.reshape(-1), unique_indices=True)
    block_e = jnp.minimum(
        jnp.searchsorted(padded_end, jnp.arange(nb, dtype=jnp.int32) * EXPERT_BLOCK,
                         side='right'), N_EXPERTS - 1).astype(jnp.int32)

    xs = _dispatch(tok_buf.reshape(nb, EXPERT_BLOCK), h2)
    ybuf = _experts(block_e, xs, w_gu.astype(BF16), b_gu.reshape(N_EXPERTS, 1, -1),
                    w_down.astype(BF16), b_down.reshape(N_EXPERTS, 1, -1))

    tm_c = 128
    dest_km = dest.reshape(n // tm_c, tm_c, TOP_K).transpose(0, 2, 1).reshape(
        n // tm_c, TOP_K * tm_c)
    out = _combine(dest_km, ybuf, x1, gate_p, row(g_final), tm_c)
    return out.reshape(b, s, d)


def kernel(x, g_mix, w_in, g_cq, g_ckv, w_uq, w_uv, w_q_idx, g_kidx, b_kidx, g_sgu, b_sgu,
           w_spatial, b_spatial, w_br_a, w_br_b, w_o, g_ffn, w_router, b_router, w_gu, b_gu,
           w_down, b_down, g_final):
    assert g_mix.shape[0] == 1, "single-layer block"
    return _layer(x, g_mix[0], w_in[0], g_cq[0], g_ckv[0], w_uq[0], w_uv[0], w_q_idx[0],
                  g_kidx[0], b_kidx[0], g_sgu[0], b_sgu[0], w_spatial[0], b_spatial[0],
                  w_br_a[0], w_br_b[0], w_o[0], g_ffn[0], w_router[0], b_router[0], w_gu[0],
                  b_gu[0], w_down[0], b_down[0], g_final)
```

```python
import functools

import jax
import jax.numpy as jnp
from jax import lax
from jax.experimental import pallas as pl
from jax.experimental.pallas import tpu as pltpu

EPS = 1e-6
CHUNK = 64
N_HEADS = 8
HEAD_DIM = 64
Q_LORA = 256
KV_LORA = 128
IDX_HEADS = 8
IDX_DIM = 32
TOPK_MAX = 256
Q_BLOCK = 128
ATTN_SCALE = KV_LORA ** -0.5
IDX_SCALE = (IDX_HEADS * IDX_DIM) ** -0.5
SGU_CHUNK = 128
SGU_GROUPS = 8
SGU_WIDTH = 512
SGU_GROUP_DIM = SGU_WIDTH // SGU_GROUPS
N_EXPERTS = 32
TOP_K = 4
SWIGLU_LIMIT = 7.0
SWIGLU_ALPHA = 1.702
EXPERT_BLOCK = 512

LANES = 128
VMEM_LIMIT = 56 * 1024 * 1024
INT_MIN = -(2 ** 31)
CODE_NEG_INF = INT_MIN + 0x7FFFFF

BF16 = jnp.bfloat16
F32 = jnp.float32


def _dot(a, b):
    return jnp.dot(a, b, preferred_element_type=F32)


def _dot_nt(a, b):
    return lax.dot_general(a, b, (((1,), (1,)), ((), ())), preferred_element_type=F32)


def _rms(x, g):
    return x * lax.rsqrt(jnp.mean(x * x, axis=-1, keepdims=True) + EPS) * g


def _layer_norm(x, g, b):
    mu = jnp.mean(x, axis=-1, keepdims=True)
    xc = x - mu
    var = jnp.mean(xc * xc, axis=-1, keepdims=True)
    return xc * lax.rsqrt(var + EPS) * g + b


C_Q = 0
C_KV = C_Q + Q_LORA
C_KI = C_KV + KV_LORA
C_WI = C_KI + IDX_HEADS * IDX_DIM
C_U = C_WI + LANES
C_V = C_U + SGU_WIDTH
D_IN_P_BASE = C_V + SGU_WIDTH


def _inproj_kernel(x_ref, gmix_ref, w_ref, gcq_ref, gckv_ref, gki_ref, bki_ref,
                   gsgu_ref, bsgu_ref, wsp_ref, bsp_ref, wbrb_ref,
                   cq_ref, ckv_ref, ki_ref, wi_ref, gas_ref, pb_ref):
    tm, d = x_ref.shape
    c_ga = D_IN_P_BASE
    c_gb = c_ga + d
    h = _rms(x_ref[...], gmix_ref[...]).astype(BF16)

    def proj(lo, width):
        return _dot(h, w_ref[:, lo:lo + width])

    cq_ref[...] = _rms(proj(C_Q, Q_LORA), gcq_ref[...]).astype(BF16)
    ckv_ref[...] = _rms(proj(C_KV, KV_LORA), gckv_ref[...]).astype(BF16)
    ki_ref[...] = _layer_norm(proj(C_KI, IDX_HEADS * IDX_DIM), gki_ref[...],
                              bki_ref[...]).astype(BF16)
    wi_ref[...] = proj(C_WI, LANES) * IDX_SCALE
    gas_ref[...] = jax.nn.sigmoid(proj(c_ga, d)).astype(BF16)

    u = jax.nn.gelu(proj(C_U, SGU_WIDTH))
    v = _layer_norm(jax.nn.gelu(proj(C_V, SGU_WIDTH)), gsgu_ref[...], bsgu_ref[...])

    row = lax.broadcasted_iota(jnp.int32, (SGU_CHUNK, 2 * SGU_CHUNK), 0)
    col = lax.broadcasted_iota(jnp.int32, (SGU_CHUNK, 2 * SGU_CHUNK), 1) % SGU_CHUNK
    causal = (row // CHUNK) >= (col // CHUNK)
    lane = lax.broadcasted_iota(jnp.int32, (SGU_CHUNK, LANES), 1)
    left = lane < SGU_GROUP_DIM
    n_tiles = SGU_WIDTH // LANES
    ws = [jnp.where(causal, wsp_ref[j], 0.0).astype(BF16) for j in range(n_tiles)]
    yb_chunks = []
    for c in range(tm // SGU_CHUNK):
        tiles = []
        for j in range(n_tiles):
            blk = v[c * SGU_CHUNK:(c + 1) * SGU_CHUNK, j * LANES:(j + 1) * LANES]
            stacked = jnp.concatenate(
                [jnp.where(left, blk, 0.0), jnp.where(left, 0.0, blk)], axis=0).astype(BF16)
            tiles.append(_dot(ws[j], stacked))
        s = jnp.concatenate(tiles, axis=1) + bsp_ref[...]
        yb_chunks.append(u[c * SGU_CHUNK:(c + 1) * SGU_CHUNK, :] * s)
    yb = jnp.concatenate(yb_chunks, axis=0).astype(BF16)
    pb_ref[...] = (jax.nn.sigmoid(proj(c_gb, d)) * _dot(yb, wbrb_ref[...])).astype(BF16)


def _inproj(x2, g_mix, w_in_p, g_cq, g_ckv, g_ki, b_ki, g_sgu, b_sgu, ws_pair, bsp, w_br_b,
            tm):
    n, d = x2.shape
    d_in_p = w_in_p.shape[1]

    def full(shape):
        return pl.BlockSpec(shape, lambda i: (0,) * len(shape))

    def rows(width):
        return pl.BlockSpec((tm, width), lambda i: (i, 0))

    return pl.pallas_call(
        _inproj_kernel,
        grid=(n // tm,),
        in_specs=[rows(d), full((1, d)), full((d, d_in_p)), full((1, Q_LORA)),
                  full((1, KV_LORA)), full((1, IDX_HEADS * IDX_DIM)),
                  full((1, IDX_HEADS * IDX_DIM)), full((1, SGU_WIDTH)), full((1, SGU_WIDTH)),
                  full(ws_pair.shape), full(bsp.shape), full(w_br_b.shape)],
        out_specs=[rows(Q_LORA), rows(KV_LORA), rows(IDX_HEADS * IDX_DIM), rows(LANES),
                   rows(d), rows(d)],
        out_shape=[jax.ShapeDtypeStruct((n, Q_LORA), BF16),
                   jax.ShapeDtypeStruct((n, KV_LORA), BF16),
                   jax.ShapeDtypeStruct((n, IDX_HEADS * IDX_DIM), BF16),
                   jax.ShapeDtypeStruct((n, LANES), F32),
                   jax.ShapeDtypeStruct((n, d), BF16),
                   jax.ShapeDtypeStruct((n, d), BF16)],
        compiler_params=pltpu.CompilerParams(dimension_semantics=("parallel",),
                                             vmem_limit_bytes=VMEM_LIMIT),
        name="inproj",
    )(x2, g_mix, w_in_p, g_cq, g_ckv, g_ki, b_ki, g_sgu, b_sgu, ws_pair, bsp, w_br_b)


def _attn_body(s_eff, topk, cq_ref, ckv_ref, ki_ref, wi_ref, wuq_ref, wqi_ref, wuv_ref,
               ya_ref, q_scr, qi_scr, wcol_scr, sc_scr, bias_scr, acc_scr):
    i = pl.program_id(1)
    cq = cq_ref[0]
    q = _dot(cq, wuq_ref[...])
    qi = _dot(cq, wqi_ref[...]).astype(BF16)
    wi = wi_ref[0]
    head_of_lane = lax.broadcasted_iota(jnp.int32, qi.shape, 1) // IDX_DIM
    for h in range(N_HEADS):
        q_scr[h] = q[:, h * KV_LORA:(h + 1) * KV_LORA].astype(BF16)
    for h in range(IDX_HEADS):
        qi_scr[h] = jnp.where(head_of_lane == h, qi, jnp.zeros_like(qi))
        wcol_scr[h] = wi[:, h:h + 1]

    bias_scr[...] = jnp.zeros_like(bias_scr)

    def score_step(h, carry):
        dots = _dot_nt(qi_scr[h], ki_ref[0, 0:s_eff, :])
        bias_scr[...] += wcol_scr[h] * jnp.maximum(dots, 0.0)
        return carry

    lax.fori_loop(0, IDX_HEADS, score_step, 0)

    rowq = lax.broadcasted_iota(jnp.int32, (Q_BLOCK, s_eff), 0)
    colk = lax.broadcasted_iota(jnp.int32, (Q_BLOCK, s_eff), 1)
    allowed = (colk // CHUNK) <= ((i * Q_BLOCK + rowq) // CHUNK)
    sc_scr[...] = jnp.where(allowed, bias_scr[...], -jnp.inf)

    def code_to_float(code):
        bits = jnp.where(code < 0, code ^ jnp.int32(0x7FFFFFFF), code)
        return lax.bitcast_convert_type(bits, F32)

    def count_ge(thr):
        return jnp.sum(jnp.where(sc_scr[...] >= thr, 1.0, 0.0), axis=1, keepdims=True)

    kf = float(topk)
    code = jnp.where(count_ge(jnp.zeros((Q_BLOCK, 1), F32)) >= kf,
                     jnp.int32(0), jnp.int32(INT_MIN))
    for bit in range(30, -1, -1):
        cand = code + jnp.int32(1 << bit)
        feasible = (count_ge(code_to_float(cand)) >= kf) | (cand <= jnp.int32(CODE_NEG_INF))
        code = jnp.where(feasible, cand, code)

    thr = code_to_float(code)
    thr_up = code_to_float(code + 1)
    sel = sc_scr[...] >= thr
    n_ge = jnp.sum(jnp.where(sel, 1.0, 0.0), axis=1, keepdims=True)
    bias_scr[...] = jnp.where(sel & allowed, 0.0, -jnp.inf)
    tie_rows = (n_ge > kf) & (code > jnp.int32(CODE_NEG_INF))
    any_tie = jnp.max(jnp.where(tie_rows, 1.0, 0.0))

    @pl.when(any_tie > 0.0)
    def _():
        sc = sc_scr[...]
        gt = sc >= thr_up
        eq = (sc >= thr) & jnp.logical_not(gt)
        need = kf - jnp.sum(jnp.where(gt, 1.0, 0.0), axis=1, keepdims=True)
        nbits = (s_eff - 1).bit_length()

        def tie_step(it, j):
            cand = j + jnp.left_shift(jnp.int32(1), jnp.int32(nbits - 1) - it)
            v = sc_scr[...]
            tie = (v >= thr) & jnp.logical_not(v >= thr_up) & (colk < cand)
            cnt = jnp.sum(jnp.where(tie, 1.0, 0.0), axis=1, keepdims=True)
            return jnp.where(cnt < need, cand, j)

        jmax = lax.fori_loop(0, nbits, tie_step, jnp.zeros((Q_BLOCK, 1), jnp.int32))
        keep = gt | (eq & (colk <= jmax))
        bias_scr[...] = jnp.where(keep & allowed, 0.0, -jnp.inf)

    acc_scr[...] = jnp.zeros_like(acc_scr)

    def head_step(h, carry):
        kv_lat = ckv_ref[0, 0:s_eff, :]
        s = _dot_nt(q_scr[h], kv_lat) * ATTN_SCALE + bias_scr[...]
        m = jnp.max(s, axis=1, keepdims=True)
        p = jnp.exp(s - m)
        l = jnp.sum(p, axis=1, keepdims=True)
        o = _dot(p.astype(BF16), kv_lat) / l
        acc_scr[...] += _dot(o.astype(BF16), wuv_ref[h])
        return carry

    lax.fori_loop(0, N_HEADS, head_step, 0)
    ya_ref[0] = acc_scr[...].astype(BF16)


def _attn_kernel(n_buckets, topk, cq_ref, ckv_ref, ki_ref, wi_ref, wuq_ref, wqi_ref, wuv_ref,
                 ya_ref, q_scr, qi_scr, wcol_scr, sc_scr, bias_scr, acc_scr):
    s = ckv_ref.shape[1]
    nq = s // Q_BLOCK
    per = nq // n_buckets
    i = pl.program_id(1)
    for k in range(n_buckets):
        s_eff = (k + 1) * per * Q_BLOCK

        @pl.when((i >= k * per) & (i < (k + 1) * per))
        def _(s_eff=s_eff):
            _attn_body(s_eff, topk, cq_ref, ckv_ref, ki_ref, wi_ref, wuq_ref, wqi_ref,
                       wuv_ref, ya_ref, q_scr, qi_scr, wcol_scr,
                       sc_scr.at[:, 0:s_eff], bias_scr.at[:, 0:s_eff], acc_scr)


def _attn(cq, ckv, ki, wi, w_uq, w_qi, w_uv_pad, n_buckets):
    b, s, _ = cq.shape
    nq = s // Q_BLOCK
    topk = min(TOPK_MAX, s // 4)
    aw = N_HEADS * HEAD_DIM

    def full(shape):
        return pl.BlockSpec(shape, lambda bi, i: (0,) * len(shape))

    return pl.pallas_call(
        functools.partial(_attn_kernel, n_buckets, topk),
        grid=(b, nq),
        in_specs=[pl.BlockSpec((1, Q_BLOCK, Q_LORA), lambda bi, i: (bi, i, 0)),
                  pl.BlockSpec((1, s, KV_LORA), lambda bi, i: (bi, 0, 0)),
                  pl.BlockSpec((1, s, IDX_HEADS * IDX_DIM), lambda bi, i: (bi, 0, 0)),
                  pl.BlockSpec((1, Q_BLOCK, LANES), lambda bi, i: (bi, i, 0)),
                  full(w_uq.shape), full(w_qi.shape), full(w_uv_pad.shape)],
        out_specs=pl.BlockSpec((1, Q_BLOCK, aw), lambda bi, i: (bi, i, 0)),
        out_shape=jax.ShapeDtypeStruct((b, s, aw), BF16),
        scratch_shapes=[pltpu.VMEM((N_HEADS, Q_BLOCK, KV_LORA), BF16),
                        pltpu.VMEM((IDX_HEADS, Q_BLOCK, IDX_HEADS * IDX_DIM), BF16),
                        pltpu.VMEM((IDX_HEADS, Q_BLOCK, 1), F32),
                        pltpu.VMEM((Q_BLOCK, s), F32),
                        pltpu.VMEM((Q_BLOCK, s), F32),
                        pltpu.VMEM((Q_BLOCK, aw), F32)],
        compiler_params=pltpu.CompilerParams(dimension_semantics=("parallel", "parallel"),
                                             vmem_limit_bytes=VMEM_LIMIT),
        name="attn",
    )(cq, ckv, ki, wi, w_uq, w_qi, w_uv_pad)


def _merge_kernel(x_ref, ya_ref, gas_ref, pb_ref, wbra_ref, wo_ref, gffn_ref,
                  wr_hi_ref, wr_lo_ref, br_ref,
                  x1_ref, h2_ref, idx_ref, gate_ref, rank_ref, cnt_ref, base_scr):
    step = pl.program_id(0)
    tm = x_ref.shape[0]

    @pl.when(step == 0)
    def _():
        base_scr[...] = jnp.zeros_like(base_scr)

    a = _dot(ya_ref[...], wbra_ref[...])
    merged = gas_ref[...].astype(F32) * a + pb_ref[...].astype(F32)
    x1 = x_ref[...] + _dot(merged.astype(BF16), wo_ref[...])
    x1_ref[...] = x1
    h2 = _rms(x1, gffn_ref[...])
    h2_ref[...] = h2

    h_hi = h2.astype(BF16)
    h_lo = (h2 - h_hi.astype(F32)).astype(BF16)
    logits = (_dot(h_hi, wr_hi_ref[...]) + _dot(h_hi, wr_lo_ref[...])
              + _dot(h_lo, wr_hi_ref[...]) + br_ref[...])

    lane_e = lax.broadcasted_iota(jnp.int32, (tm, N_EXPERTS), 1).astype(F32)
    lane_o = lax.broadcasted_iota(jnp.int32, (tm, LANES), 1)
    work = logits
    vals, idxs = [], []
    onehot = jnp.zeros((tm, N_EXPERTS), F32)
    for _ in range(TOP_K):
        m = jnp.max(work, axis=1, keepdims=True)
        idx = jnp.min(jnp.where(work == m, lane_e, float(N_EXPERTS)), axis=1, keepdims=True)
        hit = lane_e == idx
        onehot = onehot + jnp.where(hit, 1.0, 0.0)
        work = jnp.where(hit, -jnp.inf, work)
        vals.append(m)
        idxs.append(idx)
    exps = [jnp.exp(v - vals[0]) for v in vals]
    denom = exps[0] + exps[1] + exps[2] + exps[3]

    r = lax.broadcasted_iota(jnp.int32, (tm, tm), 0)
    c = lax.broadcasted_iota(jnp.int32, (tm, tm), 1)
    tri = jnp.where(c < r, 1.0, 0.0).astype(BF16)
    rank_full = _dot(tri, onehot.astype(BF16)) + base_scr[...]

    idx_out = jnp.zeros((tm, LANES), jnp.int32)
    gate_out = jnp.zeros((tm, LANES), F32)
    rank_out = jnp.zeros((tm, LANES), jnp.int32)
    for k in range(TOP_K):
        rk = jnp.sum(jnp.where(lane_e == idxs[k], rank_full, 0.0), axis=1, keepdims=True)
        idx_out = jnp.where(lane_o == k, idxs[k].astype(jnp.int32), idx_out)
        gate_out = jnp.where(lane_o == k, exps[k] / denom, gate_out)
        rank_out = jnp.where(lane_o == k, rk.astype(jnp.int32), rank_out)
    idx_ref[...] = idx_out
    gate_ref[...] = gate_out
    rank_ref[...] = rank_out

    base_scr[...] = base_scr[...] + jnp.sum(onehot, axis=0, keepdims=True)
    cnt_ref[...] = base_scr[...].astype(jnp.int32)


def _merge(x2, ya, gas, pb, w_br_a, w_o, g_ffn, wr_hi, wr_lo, b_router, tm):
    n, d = x2.shape
    aw = ya.shape[1]

    def full(shape):
        return pl.BlockSpec(shape, lambda i: (0,) * len(shape))

    def rows(width):
        return pl.BlockSpec((tm, width), lambda i: (i, 0))

    return pl.pallas_call(
        _merge_kernel,
        grid=(n // tm,),
        in_specs=[rows(d), rows(aw), rows(d), rows(d), full(w_br_a.shape), full(w_o.shape),
                  full((1, d)), full(wr_hi.shape), full(wr_lo.shape), full((1, N_EXPERTS))],
        out_specs=[rows(d), rows(d), rows(LANES), rows(LANES), rows(LANES),
                   full((1, N_EXPERTS))],
        out_shape=[jax.ShapeDtypeStruct((n, d), F32),
                   jax.ShapeDtypeStruct((n, d), F32),
                   jax.ShapeDtypeStruct((n, LANES), jnp.int32),
                   jax.ShapeDtypeStruct((n, LANES), F32),
                   jax.ShapeDtypeStruct((n, LANES), jnp.int32),
                   jax.ShapeDtypeStruct((1, N_EXPERTS), jnp.int32)],
        scratch_shapes=[pltpu.VMEM((1, N_EXPERTS), F32)],
        compiler_params=pltpu.CompilerParams(dimension_semantics=("arbitrary",),
                                             vmem_limit_bytes=VMEM_LIMIT),
        name="merge_route",
    )(x2, ya, gas, pb, w_br_a, w_o, g_ffn, wr_hi, wr_lo, b_router)


def _row_copy(src_hbm, src_row, dst_vmem, dst_row, sem):
    return pltpu.make_async_copy(src_hbm.at[pl.ds(src_row, 1)], dst_vmem.at[pl.ds(dst_row, 1)],
                                 sem)


def _gather_rows(idx_hbm_row, src_hbm, dst_vmem, idx_smem, sem_idx, sem_rows, n_rows):
    cp = pltpu.make_async_copy(idx_hbm_row, idx_smem, sem_idx)
    cp.start()
    cp.wait()

    def issue(r, carry):
        _row_copy(src_hbm, idx_smem[r], dst_vmem, r, sem_rows).start()
        return carry

    lax.fori_loop(0, n_rows, issue, 0, unroll=8)

    def drain(r, carry):
        _row_copy(src_hbm, 0, dst_vmem, r, sem_rows).wait()
        return carry

    lax.fori_loop(0, n_rows, drain, 0, unroll=8)


def _dispatch_kernel(tok_ref, h2_ref, xs_ref, idx_smem, sem_idx, sem_rows):
    blk = pl.program_id(0)
    _gather_rows(tok_ref.at[blk], h2_ref, xs_ref, idx_smem, sem_idx, sem_rows, EXPERT_BLOCK)


def _dispatch(tok_buf2, h2):
    nb = tok_buf2.shape[0]
    d = h2.shape[1]
    return pl.pallas_call(
        _dispatch_kernel,
        grid=(nb,),
        in_specs=[pl.BlockSpec(memory_space=pl.ANY), pl.BlockSpec(memory_space=pl.ANY)],
        out_specs=pl.BlockSpec((EXPERT_BLOCK, d), lambda i: (i, 0)),
        out_shape=jax.ShapeDtypeStruct((nb * EXPERT_BLOCK, d), F32),
        scratch_shapes=[pltpu.SMEM((EXPERT_BLOCK,), jnp.int32),
                        pltpu.SemaphoreType.DMA, pltpu.SemaphoreType.DMA],
        compiler_params=pltpu.CompilerParams(dimension_semantics=("arbitrary",),
                                             vmem_limit_bytes=VMEM_LIMIT),
        name="dispatch",
    )(tok_buf2, h2)


def _expert_kernel(be_ref, xs_ref, wgu_ref, bgu_ref, wd_ref, bd_ref, y_ref):
    del be_ref
    f = wd_ref.shape[1]
    gu = _dot(xs_ref[...].astype(BF16), wgu_ref[0]) + bgu_ref[0]
    gate = jnp.minimum(gu[:, :f], SWIGLU_LIMIT)
    up = jnp.clip(gu[:, f:], -SWIGLU_LIMIT, SWIGLU_LIMIT)
    act = (up + 1.0) * (gate * jax.nn.sigmoid(SWIGLU_ALPHA * gate))
    y_ref[...] = _dot(act.astype(BF16), wd_ref[0]) + bd_ref[0]


def _experts(block_e, xs, w_gu, b_gu, w_down, b_down):
    p, d = xs.shape
    nb = p // EXPERT_BLOCK
    f2 = w_gu.shape[2]
    f = w_down.shape[1]
    grid_spec = pltpu.PrefetchScalarGridSpec(
        num_scalar_prefetch=1,
        grid=(nb,),
        in_specs=[pl.BlockSpec((EXPERT_BLOCK, d), lambda i, be: (i, 0)),
                  pl.BlockSpec((1, d, f2), lambda i, be: (be[i], 0, 0)),
                  pl.BlockSpec((1, 1, f2), lambda i, be: (be[i], 0, 0)),
                  pl.BlockSpec((1, f, d), lambda i, be: (be[i], 0, 0)),
                  pl.BlockSpec((1, 1, d), lambda i, be: (be[i], 0, 0))],
        out_specs=pl.BlockSpec((EXPERT_BLOCK, d), lambda i, be: (i, 0)),
    )
    return pl.pallas_call(
        _expert_kernel,
        grid_spec=grid_spec,
        out_shape=jax.ShapeDtypeStruct((p, d), F32),
        compiler_params=pltpu.CompilerParams(dimension_semantics=("arbitrary",),
                                             vmem_limit_bytes=VMEM_LIMIT),
        name="experts",
    )(block_e, xs, w_gu, b_gu, w_down, b_down)


def _combine_kernel(dest_ref, y_ref, x1_ref, gate_ref, gfin_ref, out_ref,
                    rows_scr, idx_smem, sem_idx, sem_rows):
    blk = pl.program_id(0)
    tm = x1_ref.shape[0]
    _gather_rows(dest_ref.at[blk], y_ref, rows_scr, idx_smem, sem_idx, sem_rows, TOP_K * tm)
    gates = gate_ref[...]
    acc = x1_ref[...]
    for k in range(TOP_K):
        acc = acc + gates[:, k:k + 1] * rows_scr[k * tm:(k + 1) * tm, :]
    out_ref[...] = _rms(acc, gfin_ref[...])


def _combine(dest_km, ybuf, x1, gates, g_final, tm):
    n, d = x1.shape
    return pl.pallas_call(
        _combine_kernel,
        grid=(n // tm,),
        in_specs=[pl.BlockSpec(memory_space=pl.ANY), pl.BlockSpec(memory_space=pl.ANY),
                  pl.BlockSpec((tm, d), lambda i: (i, 0)),
                  pl.BlockSpec((tm, LANES), lambda i: (i, 0)),
                  pl.BlockSpec((1, d), lambda i: (0, 0))],
        out_specs=pl.BlockSpec((tm, d), lambda i: (i, 0)),
        out_shape=jax.ShapeDtypeStruct((n, d), F32),
        scratch_shapes=[pltpu.VMEM((TOP_K * tm, d), F32),
                        pltpu.SMEM((TOP_K * tm,), jnp.int32),
                        pltpu.SemaphoreType.DMA, pltpu.SemaphoreType.DMA],
        compiler_params=pltpu.CompilerParams(dimension_semantics=("arbitrary",),
                                             vmem_limit_bytes=VMEM_LIMIT),
        name="combine",
    )(dest_km, ybuf, x1, gates, g_final)


def _pack_in_proj(w_in, d):
    offs = [0]
    for width in (Q_LORA, KV_LORA, IDX_DIM, IDX_HEADS, SGU_WIDTH, SGU_WIDTH, d, d):
        offs.append(offs[-1] + width)
    wq, wkv, wki, wwi, wu, wv, wga, wgb = [w_in[:, offs[j]:offs[j + 1]] for j in range(8)]
    wki_rep = jnp.tile(wki, (1, IDX_HEADS))
    wwi_pad = jnp.pad(wwi, ((0, 0), (0, LANES - IDX_HEADS)))
    return jnp.concatenate([wq, wkv, wki_rep, wwi_pad, wu, wv, wga, wgb], axis=1).astype(BF16)


def _layer(x, g_mix, w_in, g_cq, g_ckv, w_uq, w_uv, w_q_idx, g_kidx, b_kidx, g_sgu, b_sgu,
           w_spatial, b_spatial, w_br_a, w_br_b, w_o, g_ffn, w_router, b_router, w_gu, b_gu,
           w_down, b_down, g_final):
    b, s, d = x.shape
    n = b * s
    x2 = x.reshape(n, d)
    row = lambda v: v.reshape(1, -1).astype(F32)

    w_in_p = _pack_in_proj(w_in, d)
    ws_pair = w_spatial.reshape(SGU_GROUPS // 2, 2, SGU_CHUNK, SGU_CHUNK).transpose(
        0, 2, 1, 3).reshape(SGU_GROUPS // 2, SGU_CHUNK, 2 * SGU_CHUNK)
    bsp = jnp.repeat(b_spatial.T, SGU_GROUP_DIM, axis=1)
    cq, ckv, ki, wi, gas, pb = _inproj(
        x2, row(g_mix), w_in_p, row(g_cq), row(g_ckv), row(jnp.tile(g_kidx, IDX_HEADS)),
        row(jnp.tile(b_kidx, IDX_HEADS)), row(g_sgu), row(b_sgu), ws_pair, bsp,
        w_br_b.astype(BF16), tm=512)

    head_eye = jnp.eye(N_HEADS, dtype=F32)
    w_uv_pad = (w_uv[:, :, None, :] * head_eye[:, None, :, None]).reshape(
        N_HEADS, KV_LORA, N_HEADS * HEAD_DIM)
    n_buckets = 4 if (s // Q_BLOCK) % 4 == 0 else 1
    ya = _attn(cq.reshape(b, s, -1), ckv.reshape(b, s, -1), ki.reshape(b, s, -1),
               wi.reshape(b, s, -1), w_uq.astype(BF16), w_q_idx.astype(BF16),
               w_uv_pad.astype(BF16), n_buckets)

    wr_hi = w_router.astype(BF16)
    wr_lo = (w_router - wr_hi.astype(F32)).astype(BF16)
    x1, h2, idx_p, gate_p, rank_p, counts = _merge(
        x2, ya.reshape(n, -1), gas, pb, w_br_a.astype(BF16), w_o.astype(BF16), row(g_ffn),
        wr_hi, wr_lo, row(b_router), tm=512)

    counts = counts[0]
    top_idx = idx_p[:, :TOP_K]
    padded = (counts + EXPERT_BLOCK - 1) // EXPERT_BLOCK * EXPERT_BLOCK
    padded_end = jnp.cumsum(padded)
    padded_start = padded_end - padded
    dest = padded_start[top_idx] + rank_p[:, :TOP_K]
    nk = n * TOP_K
    nb = -(-(nk + N_EXPERTS * EXPERT_BLOCK) // EXPERT_BLOCK)
    tok = jnp.broadcast_to(jnp.arange(n, dtype=jnp.int32)[:, None], (n, TOP_K))
    tok_buf = jnp.zeros((nb * EXPERT_BLOCK,), jnp.int32).at[dest.reshape(-1)].set(
        tok.reshape(-1), unique_indices=True)
    block_row0 = jnp.arange(nb, dtype=jnp.int32) * EXPERT_BLOCK
    block_e = jnp.minimum(
        jnp.sum((padded_end[None, :] <= block_row0[:, None]).astype(jnp.int32), axis=1),
        N_EXPERTS - 1).astype(jnp.int32)

    xs = _dispatch(tok_buf.reshape(nb, EXPERT_BLOCK), h2)
    ybuf = _experts(block_e, xs, w_gu.astype(BF16), b_gu.reshape(N_EXPERTS, 1, -1),
                    w_down.astype(BF16), b_down.reshape(N_EXPERTS, 1, -1))

    tm_c = 128
    dest_km = dest.reshape(n // tm_c, tm_c, TOP_K).transpose(0, 2, 1).reshape(
        n // tm_c, TOP_K * tm_c)
    out = _combine(dest_km, ybuf, x1, gate_p, row(g_final), tm_c)
    return out.reshape(b, s, d)


def kernel(x, g_mix, w_in, g_cq, g_ckv, w_uq, w_uv, w_q_idx, g_kidx, b_kidx, g_sgu, b_sgu,
           w_spatial, b_spatial, w_br_a, w_br_b, w_o, g_ffn, w_router, b_router, w_gu, b_gu,
           w_down, b_down, g_final):
    assert g_mix.shape[0] == 1, "single-layer block"
    return _layer(x, g_mix[0], w_in[0], g_cq[0], g_ckv[0], w_uq[0], w_uv[0], w_q_idx[0],
                  g_kidx[0], b_kidx[0], g_sgu[0], b_sgu[0], w_spatial[0], b_spatial[0],
                  w_br_a[0], w_br_b[0], w_o[0], g_ffn[0], w_router[0], b_router[0], w_gu[0],
                  b_gu[0], w_down[0], b_down[0], g_final)
```

```python
import functools

import jax
import jax.numpy as jnp
from jax import lax
from jax.experimental import pallas as pl
from jax.experimental.pallas import tpu as pltpu
from jax.experimental.pallas import tpu_sc as plsc

EPS = 1e-6
CHUNK = 64
N_HEADS = 8
HEAD_DIM = 64
Q_LORA = 256
KV_LORA = 128
IDX_HEADS = 8
IDX_DIM = 32
TOPK_MAX = 256
Q_BLOCK = 128
ATTN_SCALE = KV_LORA ** -0.5
IDX_SCALE = (IDX_HEADS * IDX_DIM) ** -0.5
SGU_CHUNK = 128
SGU_GROUPS = 8
SGU_WIDTH = 512
SGU_GROUP_DIM = SGU_WIDTH // SGU_GROUPS
N_EXPERTS = 32
TOP_K = 4
SWIGLU_LIMIT = 7.0
SWIGLU_ALPHA = 1.702
EXPERT_BLOCK = 512

SC_ROWS = 32
LANES = 128
VMEM_LIMIT = 56 * 1024 * 1024
INT_MIN = -(2 ** 31)
CODE_NEG_INF = INT_MIN + 0x7FFFFF

BF16 = jnp.bfloat16
F32 = jnp.float32


def _dot(a, b):
    return jnp.dot(a, b, preferred_element_type=F32)


def _dot_nt(a, b):
    return lax.dot_general(a, b, (((1,), (1,)), ((), ())), preferred_element_type=F32)


def _rms(x, g):
    return x * lax.rsqrt(jnp.mean(x * x, axis=-1, keepdims=True) + EPS) * g


def _layer_norm(x, g, b):
    mu = jnp.mean(x, axis=-1, keepdims=True)
    xc = x - mu
    var = jnp.mean(xc * xc, axis=-1, keepdims=True)
    return xc * lax.rsqrt(var + EPS) * g + b


C_Q = 0
C_KV = C_Q + Q_LORA
C_KI = C_KV + KV_LORA
C_WI = C_KI + IDX_HEADS * IDX_DIM
C_U = C_WI + LANES
C_V = C_U + SGU_WIDTH
D_IN_P_BASE = C_V + SGU_WIDTH


def _inproj_kernel(x_ref, gmix_ref, w_ref, gcq_ref, gckv_ref, gki_ref, bki_ref,
                   gsgu_ref, bsgu_ref, wsp_ref, bsp_ref, wbrb_ref,
                   cq_ref, ckv_ref, ki_ref, wi_ref, gas_ref, pb_ref):
    tm, d = x_ref.shape
    c_ga = D_IN_P_BASE
    c_gb = c_ga + d
    h = _rms(x_ref[...], gmix_ref[...]).astype(BF16)

    def proj(lo, width):
        return _dot(h, w_ref[:, lo:lo + width])

    cq_ref[...] = _rms(proj(C_Q, Q_LORA), gcq_ref[...]).astype(BF16)
    ckv_ref[...] = _rms(proj(C_KV, KV_LORA), gckv_ref[...]).astype(BF16)
    ki_ref[...] = _layer_norm(proj(C_KI, IDX_HEADS * IDX_DIM), gki_ref[...],
                              bki_ref[...]).astype(BF16)
    wi_ref[...] = proj(C_WI, LANES) * IDX_SCALE
    gas_ref[...] = jax.nn.sigmoid(proj(c_ga, d)).astype(BF16)

    u = jax.nn.gelu(proj(C_U, SGU_WIDTH))
    v = _layer_norm(jax.nn.gelu(proj(C_V, SGU_WIDTH)), gsgu_ref[...], bsgu_ref[...])

    row = lax.broadcasted_iota(jnp.int32, (SGU_CHUNK, 2 * SGU_CHUNK), 0)
    col = lax.broadcasted_iota(jnp.int32, (SGU_CHUNK, 2 * SGU_CHUNK), 1) % SGU_CHUNK
    causal = (row // CHUNK) >= (col // CHUNK)
    lane = lax.broadcasted_iota(jnp.int32, (SGU_CHUNK, LANES), 1)
    left = lane < SGU_GROUP_DIM
    n_tiles = SGU_WIDTH // LANES
    ws = [jnp.where(causal, wsp_ref[j], 0.0).astype(BF16) for j in range(n_tiles)]
    yb_chunks = []
    for c in range(tm // SGU_CHUNK):
        tiles = []
        for j in range(n_tiles):
            blk = v[c * SGU_CHUNK:(c + 1) * SGU_CHUNK, j * LANES:(j + 1) * LANES]
            stacked = jnp.concatenate(
                [jnp.where(left, blk, 0.0), jnp.where(left, 0.0, blk)], axis=0).astype(BF16)
            tiles.append(_dot(ws[j], stacked))
        s = jnp.concatenate(tiles, axis=1) + bsp_ref[...]
        yb_chunks.append(u[c * SGU_CHUNK:(c + 1) * SGU_CHUNK, :] * s)
    yb = jnp.concatenate(yb_chunks, axis=0).astype(BF16)
    pb_ref[...] = (jax.nn.sigmoid(proj(c_gb, d)) * _dot(yb, wbrb_ref[...])).astype(BF16)


def _inproj(x2, g_mix, w_in_p, g_cq, g_ckv, g_ki, b_ki, g_sgu, b_sgu, ws_pair, bsp, w_br_b,
            tm):
    n, d = x2.shape
    d_in_p = w_in_p.shape[1]

    def full(shape):
        return pl.BlockSpec(shape, lambda i: (0,) * len(shape))

    def rows(width):
        return pl.BlockSpec((tm, width), lambda i: (i, 0))

    return pl.pallas_call(
        _inproj_kernel,
        grid=(n // tm,),
        in_specs=[rows(d), full((1, d)), full((d, d_in_p)), full((1, Q_LORA)),
                  full((1, KV_LORA)), full((1, IDX_HEADS * IDX_DIM)),
                  full((1, IDX_HEADS * IDX_DIM)), full((1, SGU_WIDTH)), full((1, SGU_WIDTH)),
                  full(ws_pair.shape), full(bsp.shape), full(w_br_b.shape)],
        out_specs=[rows(Q_LORA), rows(KV_LORA), rows(IDX_HEADS * IDX_DIM), rows(LANES),
                   rows(d), rows(d)],
        out_shape=[jax.ShapeDtypeStruct((n, Q_LORA), BF16),
                   jax.ShapeDtypeStruct((n, KV_LORA), BF16),
                   jax.ShapeDtypeStruct((n, IDX_HEADS * IDX_DIM), BF16),
                   jax.ShapeDtypeStruct((n, LANES), F32),
                   jax.ShapeDtypeStruct((n, d), BF16),
                   jax.ShapeDtypeStruct((n, d), BF16)],
        compiler_params=pltpu.CompilerParams(dimension_semantics=("parallel",),
                                             vmem_limit_bytes=VMEM_LIMIT),
        name="inproj",
    )(x2, g_mix, w_in_p, g_cq, g_ckv, g_ki, b_ki, g_sgu, b_sgu, ws_pair, bsp, w_br_b)


def _attn_body(s_eff, topk, cq_ref, ckv_ref, ki_ref, wi_ref, wuq_ref, wqi_ref, wuv_ref,
               ya_ref, q_scr, qi_scr, wcol_scr, sc_scr, bias_scr, acc_scr):
    i = pl.program_id(1)
    cq = cq_ref[0]
    q = _dot(cq, wuq_ref[...])
    qi = _dot(cq, wqi_ref[...]).astype(BF16)
    wi = wi_ref[0]
    head_of_lane = lax.broadcasted_iota(jnp.int32, qi.shape, 1) // IDX_DIM
    for h in range(N_HEADS):
        q_scr[h] = q[:, h * KV_LORA:(h + 1) * KV_LORA].astype(BF16)
    for h in range(IDX_HEADS):
        qi_scr[h] = jnp.where(head_of_lane == h, qi, jnp.zeros_like(qi))
        wcol_scr[h] = wi[:, h:h + 1]

    bias_scr[...] = jnp.zeros_like(bias_scr)

    def score_step(h, carry):
        dots = _dot_nt(qi_scr[h], ki_ref[0, 0:s_eff, :])
        bias_scr[...] += wcol_scr[h] * jnp.maximum(dots, 0.0)
        return carry

    lax.fori_loop(0, IDX_HEADS, score_step, 0)

    rowq = lax.broadcasted_iota(jnp.int32, (Q_BLOCK, s_eff), 0)
    colk = lax.broadcasted_iota(jnp.int32, (Q_BLOCK, s_eff), 1)
    allowed = (colk // CHUNK) <= ((i * Q_BLOCK + rowq) // CHUNK)
    sc_scr[...] = jnp.where(allowed, bias_scr[...], -jnp.inf)

    def code_to_float(code):
        bits = jnp.where(code < 0, code ^ jnp.int32(0x7FFFFFFF), code)
        return lax.bitcast_convert_type(bits, F32)

    def count_ge(thr):
        return jnp.sum(jnp.where(sc_scr[...] >= thr, 1.0, 0.0), axis=1, keepdims=True)

    kf = float(topk)
    code = jnp.where(count_ge(jnp.zeros((Q_BLOCK, 1), F32)) >= kf,
                     jnp.int32(0), jnp.int32(INT_MIN))
    for bit in range(30, -1, -1):
        cand = code + jnp.int32(1 << bit)
        feasible = (count_ge(code_to_float(cand)) >= kf) | (cand <= jnp.int32(CODE_NEG_INF))
        code = jnp.where(feasible, cand, code)

    thr = code_to_float(code)
    thr_up = code_to_float(code + 1)
    sel = sc_scr[...] >= thr
    n_ge = jnp.sum(jnp.where(sel, 1.0, 0.0), axis=1, keepdims=True)
    bias_scr[...] = jnp.where(sel & allowed, 0.0, -jnp.inf)
    tie_rows = (n_ge > kf) & (code > jnp.int32(CODE_NEG_INF))
    any_tie = jnp.max(jnp.where(tie_rows, 1.0, 0.0))

    @pl.when(any_tie > 0.0)
    def _():
        sc = sc_scr[...]
        gt = sc >= thr_up
        eq = (sc >= thr) & jnp.logical_not(gt)
        need = kf - jnp.sum(jnp.where(gt, 1.0, 0.0), axis=1, keepdims=True)
        nbits = (s_eff - 1).bit_length()

        def tie_step(it, j):
            cand = j + jnp.left_shift(jnp.int32(1), jnp.int32(nbits - 1) - it)
            v = sc_scr[...]
            tie = (v >= thr) & jnp.logical_not(v >= thr_up) & (colk < cand)
            cnt = jnp.sum(jnp.where(tie, 1.0, 0.0), axis=1, keepdims=True)
            return jnp.where(cnt < need, cand, j)

        jmax = lax.fori_loop(0, nbits, tie_step, jnp.zeros((Q_BLOCK, 1), jnp.int32))
        keep = gt | (eq & (colk <= jmax))
        bias_scr[...] = jnp.where(keep & allowed, 0.0, -jnp.inf)

    acc_scr[...] = jnp.zeros_like(acc_scr)

    def head_step(h, carry):
        kv_lat = ckv_ref[0, 0:s_eff, :]
        s = _dot_nt(q_scr[h], kv_lat) * ATTN_SCALE + bias_scr[...]
        m = jnp.max(s, axis=1, keepdims=True)
        p = jnp.exp(s - m)
        l = jnp.sum(p, axis=1, keepdims=True)
        o = _dot(p.astype(BF16), kv_lat) / l
        acc_scr[...] += _dot(o.astype(BF16), wuv_ref[h])
        return carry

    lax.fori_loop(0, N_HEADS, head_step, 0)
    ya_ref[0] = acc_scr[...].astype(BF16)


def _attn_kernel(n_buckets, topk, cq_ref, ckv_ref, ki_ref, wi_ref, wuq_ref, wqi_ref, wuv_ref,
                 ya_ref, q_scr, qi_scr, wcol_scr, sc_scr, bias_scr, acc_scr):
    s = ckv_ref.shape[1]
    nq = s // Q_BLOCK
    per = nq // n_buckets
    i = pl.program_id(1)
    for k in range(n_buckets):
        s_eff = (k + 1) * per * Q_BLOCK

        @pl.when((i >= k * per) & (i < (k + 1) * per))
        def _(s_eff=s_eff):
            _attn_body(s_eff, topk, cq_ref, ckv_ref, ki_ref, wi_ref, wuq_ref, wqi_ref,
                       wuv_ref, ya_ref, q_scr, qi_scr, wcol_scr,
                       sc_scr.at[:, 0:s_eff], bias_scr.at[:, 0:s_eff], acc_scr)


def _attn(cq, ckv, ki, wi, w_uq, w_qi, w_uv_pad, n_buckets):
    b, s, _ = cq.shape
    nq = s // Q_BLOCK
    topk = min(TOPK_MAX, s // 4)
    aw = N_HEADS * HEAD_DIM

    def full(shape):
        return pl.BlockSpec(shape, lambda bi, i: (0,) * len(shape))

    return pl.pallas_call(
        functools.partial(_attn_kernel, n_buckets, topk),
        grid=(b, nq),
        in_specs=[pl.BlockSpec((1, Q_BLOCK, Q_LORA), lambda bi, i: (bi, i, 0)),
                  pl.BlockSpec((1, s, KV_LORA), lambda bi, i: (bi, 0, 0)),
                  pl.BlockSpec((1, s, IDX_HEADS * IDX_DIM), lambda bi, i: (bi, 0, 0)),
                  pl.BlockSpec((1, Q_BLOCK, LANES), lambda bi, i: (bi, i, 0)),
                  full(w_uq.shape), full(w_qi.shape), full(w_uv_pad.shape)],
        out_specs=pl.BlockSpec((1, Q_BLOCK, aw), lambda bi, i: (bi, i, 0)),
        out_shape=jax.ShapeDtypeStruct((b, s, aw), BF16),
        scratch_shapes=[pltpu.VMEM((N_HEADS, Q_BLOCK, KV_LORA), BF16),
                        pltpu.VMEM((IDX_HEADS, Q_BLOCK, IDX_HEADS * IDX_DIM), BF16),
                        pltpu.VMEM((IDX_HEADS, Q_BLOCK, 1), F32),
                        pltpu.VMEM((Q_BLOCK, s), F32),
                        pltpu.VMEM((Q_BLOCK, s), F32),
                        pltpu.VMEM((Q_BLOCK, aw), F32)],
        compiler_params=pltpu.CompilerParams(dimension_semantics=("parallel", "parallel"),
                                             vmem_limit_bytes=VMEM_LIMIT),
        name="attn",
    )(cq, ckv, ki, wi, w_uq, w_qi, w_uv_pad)


def _merge_kernel(x_ref, ya_ref, gas_ref, pb_ref, wbra_ref, wo_ref, gffn_ref,
                  wr_hi_ref, wr_lo_ref, br_ref,
                  x1_ref, h2p_ref, idx_ref, gate_ref, rank_ref, cnt_ref, base_scr):
    step = pl.program_id(0)
    tm, d = x_ref.shape

    @pl.when(step == 0)
    def _():
        base_scr[...] = jnp.zeros_like(base_scr)

    a = _dot(ya_ref[...], wbra_ref[...])
    merged = gas_ref[...].astype(F32) * a + pb_ref[...].astype(F32)
    x1 = x_ref[...] + _dot(merged.astype(BF16), wo_ref[...])
    x1_ref[...] = x1
    h2 = _rms(x1, gffn_ref[...])

    h_hi = h2.astype(BF16)
    h_bits = lax.bitcast_convert_type(h_hi.astype(F32), jnp.int32)
    h2p_ref[...] = (lax.shift_right_logical(h_bits[:, :d // 2], 16)
                    | (h_bits[:, d // 2:] & jnp.int32(-65536)))
    h_lo = (h2 - h_hi.astype(F32)).astype(BF16)
    logits = (_dot(h_hi, wr_hi_ref[...]) + _dot(h_hi, wr_lo_ref[...])
              + _dot(h_lo, wr_hi_ref[...]) + br_ref[...])

    lane_e = lax.broadcasted_iota(jnp.int32, (tm, N_EXPERTS), 1).astype(F32)
    lane_o = lax.broadcasted_iota(jnp.int32, (tm, LANES), 1)
    work = logits
    vals, idxs = [], []
    onehot = jnp.zeros((tm, N_EXPERTS), F32)
    for _ in range(TOP_K):
        m = jnp.max(work, axis=1, keepdims=True)
        idx = jnp.min(jnp.where(work == m, lane_e, float(N_EXPERTS)), axis=1, keepdims=True)
        hit = lane_e == idx
        onehot = onehot + jnp.where(hit, 1.0, 0.0)
        work = jnp.where(hit, -jnp.inf, work)
        vals.append(m)
        idxs.append(idx)
    exps = [jnp.exp(v - vals[0]) for v in vals]
    denom = exps[0] + exps[1] + exps[2] + exps[3]

    r = lax.broadcasted_iota(jnp.int32, (tm, tm), 0)
    c = lax.broadcasted_iota(jnp.int32, (tm, tm), 1)
    tri = jnp.where(c < r, 1.0, 0.0).astype(BF16)
    rank_full = _dot(tri, onehot.astype(BF16)) + base_scr[...]

    idx_out = jnp.zeros((tm, LANES), jnp.int32)
    gate_out = jnp.zeros((tm, LANES), F32)
    rank_out = jnp.zeros((tm, LANES), jnp.int32)
    for k in range(TOP_K):
        rk = jnp.sum(jnp.where(lane_e == idxs[k], rank_full, 0.0), axis=1, keepdims=True)
        idx_out = jnp.where(lane_o == k, idxs[k].astype(jnp.int32), idx_out)
        gate_out = jnp.where(lane_o == k, exps[k] / denom, gate_out)
        rank_out = jnp.where(lane_o == k, rk.astype(jnp.int32), rank_out)
    idx_ref[...] = idx_out
    gate_ref[...] = gate_out
    rank_ref[...] = rank_out

    base_scr[...] = base_scr[...] + jnp.sum(onehot, axis=0, keepdims=True)
    cnt_ref[...] = base_scr[...].astype(jnp.int32)


def _merge(x2, ya, gas, pb, w_br_a, w_o, g_ffn, wr_hi, wr_lo, b_router, tm):
    n, d = x2.shape
    aw = ya.shape[1]

    def full(shape):
        return pl.BlockSpec(shape, lambda i: (0,) * len(shape))

    def rows(width):
        return pl.BlockSpec((tm, width), lambda i: (i, 0))

    return pl.pallas_call(
        _merge_kernel,
        grid=(n // tm,),
        in_specs=[rows(d), rows(aw), rows(d), rows(d), full(w_br_a.shape), full(w_o.shape),
                  full((1, d)), full(wr_hi.shape), full(wr_lo.shape), full((1, N_EXPERTS))],
        out_specs=[rows(d), rows(d // 2), rows(LANES), rows(LANES), rows(LANES),
                   full((1, N_EXPERTS))],
        out_shape=[jax.ShapeDtypeStruct((n, d), F32),
                   jax.ShapeDtypeStruct((n, d // 2), jnp.int32),
                   jax.ShapeDtypeStruct((n, LANES), jnp.int32),
                   jax.ShapeDtypeStruct((n, LANES), F32),
                   jax.ShapeDtypeStruct((n, LANES), jnp.int32),
                   jax.ShapeDtypeStruct((1, N_EXPERTS), jnp.int32)],
        scratch_shapes=[pltpu.VMEM((1, N_EXPERTS), F32)],
        compiler_params=pltpu.CompilerParams(dimension_semantics=("arbitrary",),
                                             vmem_limit_bytes=VMEM_LIMIT),
        name="merge_route",
    )(x2, ya, gas, pb, w_br_a, w_o, g_ffn, wr_hi, wr_lo, b_router)


def _sc_mesh():
    info = plsc.get_sparse_core_info()
    mesh = plsc.VectorSubcoreMesh(core_axis_name="c", subcore_axis_name="s")
    return mesh, info.num_cores, info.num_cores * info.num_subcores


def _sc_dispatch(src, dest, pad_idx, n_out_rows):
    n, w = src.shape
    mesh, n_cores, n_workers = _sc_mesh()
    per_w = n // n_workers
    n_chunks = per_w // SC_ROWS
    n_pad_chunks = pad_idx.size // (n_workers * SC_ROWS)

    @functools.partial(
        pl.kernel, mesh=mesh,
        out_type=jax.ShapeDtypeStruct((n_out_rows, w), jnp.int32),
        scratch_types=[pltpu.VMEM((n_chunks, TOP_K, SC_ROWS), jnp.int32),
                       pltpu.VMEM((n_pad_chunks, SC_ROWS), jnp.int32),
                       pltpu.VMEM((SC_ROWS, w), jnp.int32),
                       pltpu.VMEM((SC_ROWS, w), jnp.int32)],
        name="sc_dispatch",
    )
    def run(src_hbm, dest_hbm, pad_hbm, zeros_hbm, out_hbm, idx_v, pad_v, rows_v, zero_v):
        wid = lax.axis_index("s") * n_cores + lax.axis_index("c")
        pltpu.sync_copy(dest_hbm.at[wid], idx_v)
        pltpu.sync_copy(pad_hbm.at[wid], pad_v)
        pltpu.sync_copy(zeros_hbm, zero_v)

        @pl.loop(0, n_pad_chunks)
        def _(c):
            pltpu.sync_copy(zero_v, out_hbm.at[pad_v.at[c]])

        @pl.loop(0, n_chunks)
        def _(c):
            pltpu.sync_copy(src_hbm.at[pl.ds(wid * per_w + c * SC_ROWS, SC_ROWS)], rows_v)
            for k in range(TOP_K):
                pltpu.sync_copy(rows_v, out_hbm.at[idx_v.at[c, k]])

    dest_w = dest.reshape(n_workers, n_chunks, SC_ROWS, TOP_K).transpose(0, 1, 3, 2)
    return run(src, dest_w, pad_idx.reshape(n_workers, n_pad_chunks, SC_ROWS),
               jnp.zeros((SC_ROWS, w), jnp.int32))


def _sc_gather(table, idx):
    n_rows = idx.shape[0]
    w = table.shape[1]
    mesh, n_cores, n_workers = _sc_mesh()
    per_w = n_rows // n_workers
    n_chunks = per_w // SC_ROWS
    assert n_chunks % 2 == 0 and n_chunks * SC_ROWS * n_workers == n_rows

    @functools.partial(
        pl.kernel, mesh=mesh,
        out_type=jax.ShapeDtypeStruct((n_rows, w), table.dtype),
        scratch_types=[pltpu.VMEM((n_chunks, SC_ROWS), jnp.int32),
                       pltpu.VMEM((SC_ROWS, w), table.dtype),
                       pltpu.VMEM((SC_ROWS, w), table.dtype),
                       pltpu.SemaphoreType.DMA, pltpu.SemaphoreType.DMA],
        name="sc_gather",
    )
    def run(table_hbm, idx_hbm, out_hbm, idx_v, buf0, buf1, sem0, sem1):
        wid = lax.axis_index("s") * n_cores + lax.axis_index("c")
        base = wid * per_w
        pltpu.sync_copy(idx_hbm.at[wid], idx_v)

        def gather(c, buf, sem):
            return pltpu.make_async_copy(table_hbm.at[idx_v.at[c]], buf, sem)

        gather(0, buf0, sem0).start()

        @pl.loop(0, n_chunks, step=2)
        def _(c):
            gather(c + 1, buf1, sem1).start()
            gather(c, buf0, sem0).wait()
            pltpu.sync_copy(buf0, out_hbm.at[pl.ds(base + c * SC_ROWS, SC_ROWS)])

            @pl.when(c + 2 < n_chunks)
            def _():
                gather(c + 2, buf0, sem0).start()

            gather(c + 1, buf1, sem1).wait()
            pltpu.sync_copy(buf1, out_hbm.at[pl.ds(base + (c + 1) * SC_ROWS, SC_ROWS)])

    return run(table, idx.reshape(n_workers, n_chunks, SC_ROWS))


def _expert_kernel(be_ref, nu_ref, xs_ref, wgu_ref, bgu_ref, wd_ref, bd_ref, y_ref):
    del be_ref
    f = wd_ref.shape[1]
    half = xs_ref.shape[1]

    @pl.when(pl.program_id(0) < nu_ref[0])
    def _():
        words = xs_ref[...]
        x_lo = lax.bitcast_convert_type(lax.shift_left(words, 16), F32).astype(BF16)
        x_hi = lax.bitcast_convert_type(words & jnp.int32(-65536), F32).astype(BF16)
        gu = (_dot(x_lo, wgu_ref[0, 0:half, :]) + _dot(x_hi, wgu_ref[0, half:2 * half, :])
              + bgu_ref[0])
        gate = jnp.minimum(gu[:, :f], SWIGLU_LIMIT)
        up = jnp.clip(gu[:, f:], -SWIGLU_LIMIT, SWIGLU_LIMIT)
        act = (up + 1.0) * (gate * jax.nn.sigmoid(SWIGLU_ALPHA * gate))
        y_ref[...] = _dot(act.astype(BF16), wd_ref[0]) + bd_ref[0]


def _experts(block_e, n_used, xs, n_blocks, w_gu, b_gu, w_down, b_down):
    half = xs.shape[1]
    d = 2 * half
    f2 = w_gu.shape[2]
    f = w_down.shape[1]

    def blk(i, be, nu):
        return (jnp.minimum(i, nu[0] - 1), 0)

    def expert(i, be, nu):
        return (be[jnp.minimum(i, nu[0] - 1)], 0, 0)

    grid_spec = pltpu.PrefetchScalarGridSpec(
        num_scalar_prefetch=2,
        grid=(n_blocks,),
        in_specs=[pl.BlockSpec((EXPERT_BLOCK, half), blk),
                  pl.BlockSpec((1, d, f2), expert),
                  pl.BlockSpec((1, 1, f2), expert),
                  pl.BlockSpec((1, f, d), expert),
                  pl.BlockSpec((1, 1, d), expert)],
        out_specs=pl.BlockSpec((EXPERT_BLOCK, d), blk),
    )
    return pl.pallas_call(
        _expert_kernel,
        grid_spec=grid_spec,
        out_shape=jax.ShapeDtypeStruct((n_blocks * EXPERT_BLOCK, d), F32),
        compiler_params=pltpu.CompilerParams(dimension_semantics=("arbitrary",),
                                             vmem_limit_bytes=VMEM_LIMIT),
        name="experts",
    )(block_e, n_used, xs, w_gu, b_gu, w_down, b_down)


def _combine_kernel(yg_ref, x1_ref, gate_ref, gfin_ref, out_ref):
    gates = gate_ref[...]
    acc = x1_ref[...]
    for k in range(TOP_K):
        acc = acc + gates[:, k:k + 1] * yg_ref[k]
    out_ref[...] = _rms(acc, gfin_ref[...])


def _combine(yg, x1, gates, g_final, tm):
    n, d = x1.shape
    return pl.pallas_call(
        _combine_kernel,
        grid=(n // tm,),
        in_specs=[pl.BlockSpec((TOP_K, tm, d), lambda i: (0, i, 0)),
                  pl.BlockSpec((tm, d), lambda i: (i, 0)),
                  pl.BlockSpec((tm, LANES), lambda i: (i, 0)),
                  pl.BlockSpec((1, d), lambda i: (0, 0))],
        out_specs=pl.BlockSpec((tm, d), lambda i: (i, 0)),
        out_shape=jax.ShapeDtypeStruct((n, d), F32),
        compiler_params=pltpu.CompilerParams(dimension_semantics=("parallel",),
                                             vmem_limit_bytes=VMEM_LIMIT),
        name="combine",
    )(yg, x1, gates, g_final)


def _pack_in_proj(w_in, d):
    offs = [0]
    for width in (Q_LORA, KV_LORA, IDX_DIM, IDX_HEADS, SGU_WIDTH, SGU_WIDTH, d, d):
        offs.append(offs[-1] + width)
    wq, wkv, wki, wwi, wu, wv, wga, wgb = [w_in[:, offs[j]:offs[j + 1]] for j in range(8)]
    wki_rep = jnp.tile(wki, (1, IDX_HEADS))
    wwi_pad = jnp.pad(wwi, ((0, 0), (0, LANES - IDX_HEADS)))
    return jnp.concatenate([wq, wkv, wki_rep, wwi_pad, wu, wv, wga, wgb], axis=1).astype(BF16)


def _layer(x, g_mix, w_in, g_cq, g_ckv, w_uq, w_uv, w_q_idx, g_kidx, b_kidx, g_sgu, b_sgu,
           w_spatial, b_spatial, w_br_a, w_br_b, w_o, g_ffn, w_router, b_router, w_gu, b_gu,
           w_down, b_down, g_final):
    b, s, d = x.shape
    n = b * s
    x2 = x.reshape(n, d)
    row = lambda v: v.reshape(1, -1).astype(F32)

    w_in_p = _pack_in_proj(w_in, d)
    ws_pair = w_spatial.reshape(SGU_GROUPS // 2, 2, SGU_CHUNK, SGU_CHUNK).transpose(
        0, 2, 1, 3).reshape(SGU_GROUPS // 2, SGU_CHUNK, 2 * SGU_CHUNK)
    bsp = jnp.repeat(b_spatial.T, SGU_GROUP_DIM, axis=1)
    cq, ckv, ki, wi, gas, pb = _inproj(
        x2, row(g_mix), w_in_p, row(g_cq), row(g_ckv), row(jnp.tile(g_kidx, IDX_HEADS)),
        row(jnp.tile(b_kidx, IDX_HEADS)), row(g_sgu), row(b_sgu), ws_pair, bsp,
        w_br_b.astype(BF16), tm=512)

    head_eye = jnp.eye(N_HEADS, dtype=F32)
    w_uv_pad = (w_uv[:, :, None, :] * head_eye[:, None, :, None]).reshape(
        N_HEADS, KV_LORA, N_HEADS * HEAD_DIM)
    n_buckets = 4 if (s // Q_BLOCK) % 4 == 0 else 1
    ya = _attn(cq.reshape(b, s, -1), ckv.reshape(b, s, -1), ki.reshape(b, s, -1),
               wi.reshape(b, s, -1), w_uq.astype(BF16), w_q_idx.astype(BF16),
               w_uv_pad.astype(BF16), n_buckets)

    wr_hi = w_router.astype(BF16)
    wr_lo = (w_router - wr_hi.astype(F32)).astype(BF16)
    x1, h2p, idx_p, gate_p, rank_p, counts = _merge(
        x2, ya.reshape(n, -1), gas, pb, w_br_a.astype(BF16), w_o.astype(BF16), row(g_ffn),
        wr_hi, wr_lo, row(b_router), tm=512)

    counts = counts[0]
    top_idx = idx_p[:, :TOP_K]
    padded = (counts + EXPERT_BLOCK - 1) // EXPERT_BLOCK * EXPERT_BLOCK
    padded_end = jnp.cumsum(padded)
    padded_start = padded_end - padded
    dest = padded_start[top_idx] + rank_p[:, :TOP_K]
    nk = n * TOP_K
    nb = -(-(nk + N_EXPERTS * EXPERT_BLOCK) // EXPERT_BLOCK)
    block_row0 = jnp.arange(nb, dtype=jnp.int32) * EXPERT_BLOCK
    block_e = jnp.minimum(
        jnp.sum((padded_end[None, :] <= block_row0[:, None]).astype(jnp.int32), axis=1),
        N_EXPERTS - 1).astype(jnp.int32)
    n_used = (padded_end[-1:] // EXPERT_BLOCK).astype(jnp.int32)
    j = jnp.arange(EXPERT_BLOCK, dtype=jnp.int32)[None, :]
    pad_idx = jnp.where(j < (padded - counts)[:, None], (padded_start + counts)[:, None] + j,
                        nb * EXPERT_BLOCK + j).astype(jnp.int32)

    xs = _sc_dispatch(h2p, dest, pad_idx, (nb + 1) * EXPERT_BLOCK)
    ybuf = _experts(block_e, n_used, xs, nb, w_gu.astype(BF16),
                    b_gu.reshape(N_EXPERTS, 1, -1), w_down.astype(BF16),
                    b_down.reshape(N_EXPERTS, 1, -1))
    yg = _sc_gather(ybuf, dest.T.reshape(-1))
    out = _combine(yg.reshape(TOP_K, n, d), x1, gate_p, row(g_final), tm=256)
    return out.reshape(b, s, d)


def kernel(x, g_mix, w_in, g_cq, g_ckv, w_uq, w_uv, w_q_idx, g_kidx, b_kidx, g_sgu, b_sgu,
           w_spatial, b_spatial, w_br_a, w_br_b, w_o, g_ffn, w_router, b_router, w_gu, b_gu,
           w_down, b_down, g_final):
    assert g_mix.shape[0] == 1, "single-layer block"
    return _layer(x, g_mix[0], w_in[0], g_cq[0], g_ckv[0], w_uq[0], w_uv[0], w_q_idx[0],
                  g_kidx[0], b_kidx[0], g_sgu[0], b_sgu[0], w_spatial[0], b_spatial[0],
                  w_br_a[0], w_br_b[0], w_o[0], g_ffn[0], w_router[0], b_router[0], w_gu[0],
                  b_gu[0], w_down[0], b_down[0], g_final)
```

```python
import functools

import jax
import jax.numpy as jnp
from jax import lax
from jax.experimental import pallas as pl
from jax.experimental.pallas import tpu as pltpu
from jax.experimental.pallas import tpu_sc as plsc

EPS = 1e-6
CHUNK = 64
N_HEADS = 8
HEAD_DIM = 64
Q_LORA = 256
KV_LORA = 128
IDX_HEADS = 8
IDX_DIM = 32
TOPK_MAX = 256
Q_BLOCK = 128
ATTN_SCALE = KV_LORA ** -0.5
IDX_SCALE = (IDX_HEADS * IDX_DIM) ** -0.5
SGU_CHUNK = 128
SGU_GROUPS = 8
SGU_WIDTH = 512
SGU_GROUP_DIM = SGU_WIDTH // SGU_GROUPS
N_EXPERTS = 32
TOP_K = 4
SWIGLU_LIMIT = 7.0
SWIGLU_ALPHA = 1.702
EXPERT_BLOCK = 512

SC_ROWS = 32
LANES = 128
VMEM_LIMIT = 56 * 1024 * 1024
INT_MIN = -(2 ** 31)
CODE_NEG_INF = INT_MIN + 0x7FFFFF

BF16 = jnp.bfloat16
F32 = jnp.float32


def _dot(a, b):
    return jnp.dot(a, b, preferred_element_type=F32)


def _dot_nt(a, b):
    return lax.dot_general(a, b, (((1,), (1,)), ((), ())), preferred_element_type=F32)


def _rms(x, g):
    return x * lax.rsqrt(jnp.mean(x * x, axis=-1, keepdims=True) + EPS) * g


def _layer_norm(x, g, b):
    mu = jnp.mean(x, axis=-1, keepdims=True)
    xc = x - mu
    var = jnp.mean(xc * xc, axis=-1, keepdims=True)
    return xc * lax.rsqrt(var + EPS) * g + b


C_Q = 0
C_KV = C_Q + Q_LORA
C_KI = C_KV + KV_LORA
C_WI = C_KI + IDX_HEADS * IDX_DIM
C_U = C_WI + LANES
C_V = C_U + SGU_WIDTH
D_IN_P_BASE = C_V + SGU_WIDTH


def _inproj_kernel(x_ref, gmix_ref, w_ref, gcq_ref, gckv_ref, gki_ref, bki_ref,
                   gsgu_ref, bsgu_ref, wsp_ref, bsp_ref, wbrb_ref,
                   cq_ref, ckv_ref, ki_ref, wi_ref, gas_ref, pb_ref):
    tm, d = x_ref.shape
    c_ga = D_IN_P_BASE
    c_gb = c_ga + d
    h = _rms(x_ref[...], gmix_ref[...]).astype(BF16)

    def proj(lo, width):
        return _dot(h, w_ref[:, lo:lo + width])

    cq_ref[...] = _rms(proj(C_Q, Q_LORA), gcq_ref[...]).astype(BF16)
    ckv_ref[...] = _rms(proj(C_KV, KV_LORA), gckv_ref[...]).astype(BF16)
    ki_ref[...] = _layer_norm(proj(C_KI, IDX_HEADS * IDX_DIM), gki_ref[...],
                              bki_ref[...]).astype(BF16)
    wi_ref[...] = proj(C_WI, LANES) * IDX_SCALE
    gas_ref[...] = jax.nn.sigmoid(proj(c_ga, d)).astype(BF16)

    u = jax.nn.gelu(proj(C_U, SGU_WIDTH))
    v = _layer_norm(jax.nn.gelu(proj(C_V, SGU_WIDTH)), gsgu_ref[...], bsgu_ref[...])

    row = lax.broadcasted_iota(jnp.int32, (SGU_CHUNK, 2 * SGU_CHUNK), 0)
    col = lax.broadcasted_iota(jnp.int32, (SGU_CHUNK, 2 * SGU_CHUNK), 1) % SGU_CHUNK
    causal = (row // CHUNK) >= (col // CHUNK)
    lane = lax.broadcasted_iota(jnp.int32, (SGU_CHUNK, LANES), 1)
    left = lane < SGU_GROUP_DIM
    n_tiles = SGU_WIDTH // LANES
    ws = [jnp.where(causal, wsp_ref[j], 0.0).astype(BF16) for j in range(n_tiles)]
    yb_chunks = []
    for c in range(tm // SGU_CHUNK):
        tiles = []
        for j in range(n_tiles):
            blk = v[c * SGU_CHUNK:(c + 1) * SGU_CHUNK, j * LANES:(j + 1) * LANES]
            stacked = jnp.concatenate(
                [jnp.where(left, blk, 0.0), jnp.where(left, 0.0, blk)], axis=0).astype(BF16)
            tiles.append(_dot(ws[j], stacked))
        s = jnp.concatenate(tiles, axis=1) + bsp_ref[...]
        yb_chunks.append(u[c * SGU_CHUNK:(c + 1) * SGU_CHUNK, :] * s)
    yb = jnp.concatenate(yb_chunks, axis=0).astype(BF16)
    pb_ref[...] = (jax.nn.sigmoid(proj(c_gb, d)) * _dot(yb, wbrb_ref[...])).astype(BF16)


def _inproj(x2, g_mix, w_in_p, g_cq, g_ckv, g_ki, b_ki, g_sgu, b_sgu, ws_pair, bsp, w_br_b,
            tm):
    n, d = x2.shape
    d_in_p = w_in_p.shape[1]

    def full(shape):
        return pl.BlockSpec(shape, lambda i: (0,) * len(shape))

    def rows(width):
        return pl.BlockSpec((tm, width), lambda i: (i, 0))

    return pl.pallas_call(
        _inproj_kernel,
        grid=(n // tm,),
        in_specs=[rows(d), full((1, d)), full((d, d_in_p)), full((1, Q_LORA)),
                  full((1, KV_LORA)), full((1, IDX_HEADS * IDX_DIM)),
                  full((1, IDX_HEADS * IDX_DIM)), full((1, SGU_WIDTH)), full((1, SGU_WIDTH)),
                  full(ws_pair.shape), full(bsp.shape), full(w_br_b.shape)],
        out_specs=[rows(Q_LORA), rows(KV_LORA), rows(IDX_HEADS * IDX_DIM), rows(LANES),
                   rows(d), rows(d)],
        out_shape=[jax.ShapeDtypeStruct((n, Q_LORA), BF16),
                   jax.ShapeDtypeStruct((n, KV_LORA), BF16),
                   jax.ShapeDtypeStruct((n, IDX_HEADS * IDX_DIM), BF16),
                   jax.ShapeDtypeStruct((n, LANES), F32),
                   jax.ShapeDtypeStruct((n, d), BF16),
                   jax.ShapeDtypeStruct((n, d), BF16)],
        compiler_params=pltpu.CompilerParams(dimension_semantics=("parallel",),
                                             vmem_limit_bytes=VMEM_LIMIT),
        name="inproj",
    )(x2, g_mix, w_in_p, g_cq, g_ckv, g_ki, b_ki, g_sgu, b_sgu, ws_pair, bsp, w_br_b)


def _attn_body(s_eff, topk, cq_ref, ckv_ref, ki_ref, wi_ref, wuq_ref, wqi_ref, wuv_ref,
               ya_ref, q_scr, qi_scr, wcol_scr, sc_scr, bias_scr, acc_scr, s_buf, p_buf):
    i = pl.program_id(1)
    cq = cq_ref[0]
    q = _dot(cq, wuq_ref[...])
    qi = _dot(cq, wqi_ref[...]).astype(BF16)
    wi = wi_ref[0]
    head_of_lane = lax.broadcasted_iota(jnp.int32, qi.shape, 1) // IDX_DIM
    for h in range(N_HEADS):
        q_scr[h] = q[:, h * KV_LORA:(h + 1) * KV_LORA].astype(BF16)
    for h in range(IDX_HEADS):
        qi_scr[h] = jnp.where(head_of_lane == h, qi, jnp.zeros_like(qi))
        wcol_scr[h] = wi[:, h:h + 1]

    def index_dots(h):
        s_buf[h % 2] = _dot_nt(qi_scr[h], ki_ref[0, 0:s_eff, :])

    index_dots(0)
    for h in range(IDX_HEADS):
        if h + 1 < IDX_HEADS:
            index_dots(h + 1)
        term = wcol_scr[h] * jnp.maximum(s_buf[h % 2], 0.0)
        bias_scr[...] = term if h == 0 else bias_scr[...] + term

    rowq = lax.broadcasted_iota(jnp.int32, (Q_BLOCK, s_eff), 0)
    colk = lax.broadcasted_iota(jnp.int32, (Q_BLOCK, s_eff), 1)
    allowed = (colk // CHUNK) <= ((i * Q_BLOCK + rowq) // CHUNK)
    sc_scr[...] = jnp.where(allowed, bias_scr[...], -jnp.inf)

    def code_to_float(code):
        bits = jnp.where(code < 0, code ^ jnp.int32(0x7FFFFFFF), code)
        return lax.bitcast_convert_type(bits, F32)

    def count_ge(thr):
        return jnp.sum(jnp.where(sc_scr[...] >= thr, 1.0, 0.0), axis=1, keepdims=True)

    kf = float(topk)
    code = jnp.where(count_ge(jnp.zeros((Q_BLOCK, 1), F32)) >= kf,
                     jnp.int32(0), jnp.int32(INT_MIN))
    for bit in range(30, -1, -1):
        cand = code + jnp.int32(1 << bit)
        feasible = (count_ge(code_to_float(cand)) >= kf) | (cand <= jnp.int32(CODE_NEG_INF))
        code = jnp.where(feasible, cand, code)

    thr = code_to_float(code)
    thr_up = code_to_float(code + 1)
    sel = sc_scr[...] >= thr
    n_ge = jnp.sum(jnp.where(sel, 1.0, 0.0), axis=1, keepdims=True)
    bias_scr[...] = jnp.where(sel & allowed, 0.0, -jnp.inf)
    tie_rows = (n_ge > kf) & (code > jnp.int32(CODE_NEG_INF))
    any_tie = jnp.max(jnp.where(tie_rows, 1.0, 0.0))

    @pl.when(any_tie > 0.0)
    def _():
        sc = sc_scr[...]
        gt = sc >= thr_up
        eq = (sc >= thr) & jnp.logical_not(gt)
        need = kf - jnp.sum(jnp.where(gt, 1.0, 0.0), axis=1, keepdims=True)
        nbits = (s_eff - 1).bit_length()

        def tie_step(it, j):
            cand = j + jnp.left_shift(jnp.int32(1), jnp.int32(nbits - 1) - it)
            v = sc_scr[...]
            tie = (v >= thr) & jnp.logical_not(v >= thr_up) & (colk < cand)
            cnt = jnp.sum(jnp.where(tie, 1.0, 0.0), axis=1, keepdims=True)
            return jnp.where(cnt < need, cand, j)

        jmax = lax.fori_loop(0, nbits, tie_step, jnp.zeros((Q_BLOCK, 1), jnp.int32))
        keep = gt | (eq & (colk <= jmax))
        bias_scr[...] = jnp.where(keep & allowed, 0.0, -jnp.inf)

    acc_scr[...] = jnp.zeros_like(acc_scr)

    def logits(h):
        s = _dot_nt(q_scr[h], ckv_ref[0, 0:s_eff, :]) * ATTN_SCALE + bias_scr[...]
        s_buf[h % 2] = s

    def softmax(h):
        s = s_buf[h % 2]
        p = jnp.exp(s - jnp.max(s, axis=1, keepdims=True))
        p_buf[h % 2] = p.astype(BF16)
        return jnp.sum(p, axis=1, keepdims=True)

    def values(h, l):
        o = _dot(p_buf[h % 2], ckv_ref[0, 0:s_eff, :]) / l
        acc_scr[...] += _dot(o.astype(BF16), wuv_ref[h])

    logits(0)
    for h in range(N_HEADS):
        if h + 1 < N_HEADS:
            logits(h + 1)
        values(h, softmax(h))
    ya_ref[0] = acc_scr[...].astype(BF16)


def _attn_kernel(n_buckets, topk, cq_ref, ckv_ref, ki_ref, wi_ref, wuq_ref, wqi_ref, wuv_ref,
                 ya_ref, q_scr, qi_scr, wcol_scr, sc_scr, bias_scr, acc_scr, s_buf, p_buf):
    s = ckv_ref.shape[1]
    nq = s // Q_BLOCK
    per = nq // n_buckets
    i = pl.program_id(1)
    for k in range(n_buckets):
        s_eff = (k + 1) * per * Q_BLOCK

        @pl.when((i >= k * per) & (i < (k + 1) * per))
        def _(s_eff=s_eff):
            _attn_body(s_eff, topk, cq_ref, ckv_ref, ki_ref, wi_ref, wuq_ref, wqi_ref,
                       wuv_ref, ya_ref, q_scr, qi_scr, wcol_scr,
                       sc_scr.at[:, 0:s_eff], bias_scr.at[:, 0:s_eff], acc_scr,
                       s_buf.at[:, :, 0:s_eff], p_buf.at[:, :, 0:s_eff])


def _attn(cq, ckv, ki, wi, w_uq, w_qi, w_uv_pad, n_buckets):
    b, s, _ = cq.shape
    nq = s // Q_BLOCK
    topk = min(TOPK_MAX, s // 4)
    aw = N_HEADS * HEAD_DIM

    def full(shape):
        return pl.BlockSpec(shape, lambda bi, i: (0,) * len(shape))

    return pl.pallas_call(
        functools.partial(_attn_kernel, n_buckets, topk),
        grid=(b, nq),
        in_specs=[pl.BlockSpec((1, Q_BLOCK, Q_LORA), lambda bi, i: (bi, i, 0)),
                  pl.BlockSpec((1, s, KV_LORA), lambda bi, i: (bi, 0, 0)),
                  pl.BlockSpec((1, s, IDX_HEADS * IDX_DIM), lambda bi, i: (bi, 0, 0)),
                  pl.BlockSpec((1, Q_BLOCK, LANES), lambda bi, i: (bi, i, 0)),
                  full(w_uq.shape), full(w_qi.shape), full(w_uv_pad.shape)],
        out_specs=pl.BlockSpec((1, Q_BLOCK, aw), lambda bi, i: (bi, i, 0)),
        out_shape=jax.ShapeDtypeStruct((b, s, aw), BF16),
        scratch_shapes=[pltpu.VMEM((N_HEADS, Q_BLOCK, KV_LORA), BF16),
                        pltpu.VMEM((IDX_HEADS, Q_BLOCK, IDX_HEADS * IDX_DIM), BF16),
                        pltpu.VMEM((IDX_HEADS, Q_BLOCK, 1), F32),
                        pltpu.VMEM((Q_BLOCK, s), F32),
                        pltpu.VMEM((Q_BLOCK, s), F32),
                        pltpu.VMEM((Q_BLOCK, aw), F32),
                        pltpu.VMEM((2, Q_BLOCK, s), F32),
                        pltpu.VMEM((2, Q_BLOCK, s), BF16)],
        compiler_params=pltpu.CompilerParams(dimension_semantics=("parallel", "parallel"),
                                             vmem_limit_bytes=VMEM_LIMIT),
        name="attn",
    )(cq, ckv, ki, wi, w_uq, w_qi, w_uv_pad)


def _merge_kernel(x_ref, ya_ref, gas_ref, pb_ref, wbra_ref, wo_ref, gffn_ref,
                  wr_hi_ref, wr_lo_ref, br_ref,
                  x1_ref, h2p_ref, idx_ref, gate_ref, rank_ref, cnt_ref, base_scr):
    step = pl.program_id(0)
    tm, d = x_ref.shape

    @pl.when(step == 0)
    def _():
        base_scr[...] = jnp.zeros_like(base_scr)

    a = _dot(ya_ref[...], wbra_ref[...])
    merged = gas_ref[...].astype(F32) * a + pb_ref[...].astype(F32)
    x1 = x_ref[...] + _dot(merged.astype(BF16), wo_ref[...])
    x1_ref[...] = x1
    h2 = _rms(x1, gffn_ref[...])

    h_hi = h2.astype(BF16)
    h_bits = lax.bitcast_convert_type(h_hi.astype(F32), jnp.int32)
    h2p_ref[...] = (lax.shift_right_logical(h_bits[:, :d // 2], 16)
                    | (h_bits[:, d // 2:] & jnp.int32(-65536)))
    h_lo = (h2 - h_hi.astype(F32)).astype(BF16)
    logits = (_dot(h_hi, wr_hi_ref[...]) + _dot(h_hi, wr_lo_ref[...])
              + _dot(h_lo, wr_hi_ref[...]) + br_ref[...])

    lane_e = lax.broadcasted_iota(jnp.int32, (tm, N_EXPERTS), 1).astype(F32)
    lane_o = lax.broadcasted_iota(jnp.int32, (tm, LANES), 1)
    work = logits
    vals, idxs = [], []
    onehot = jnp.zeros((tm, N_EXPERTS), F32)
    for _ in range(TOP_K):
        m = jnp.max(work, axis=1, keepdims=True)
        idx = jnp.min(jnp.where(work == m, lane_e, float(N_EXPERTS)), axis=1, keepdims=True)
        hit = lane_e == idx
        onehot = onehot + jnp.where(hit, 1.0, 0.0)
        work = jnp.where(hit, -jnp.inf, work)
        vals.append(m)
        idxs.append(idx)
    exps = [jnp.exp(v - vals[0]) for v in vals]
    denom = exps[0] + exps[1] + exps[2] + exps[3]

    r = lax.broadcasted_iota(jnp.int32, (tm, tm), 0)
    c = lax.broadcasted_iota(jnp.int32, (tm, tm), 1)
    tri = jnp.where(c < r, 1.0, 0.0).astype(BF16)
    rank_full = _dot(tri, onehot.astype(BF16)) + base_scr[...]

    idx_out = jnp.zeros((tm, LANES), jnp.int32)
    gate_out = jnp.zeros((tm, LANES), F32)
    rank_out = jnp.zeros((tm, LANES), jnp.int32)
    for k in range(TOP_K):
        rk = jnp.sum(jnp.where(lane_e == idxs[k], rank_full, 0.0), axis=1, keepdims=True)
        idx_out = jnp.where(lane_o == k, idxs[k].astype(jnp.int32), idx_out)
        gate_out = jnp.where(lane_o == k, exps[k] / denom, gate_out)
        rank_out = jnp.where(lane_o == k, rk.astype(jnp.int32), rank_out)
    idx_ref[...] = idx_out
    gate_ref[...] = gate_out
    rank_ref[...] = rank_out

    base_scr[...] = base_scr[...] + jnp.sum(onehot, axis=0, keepdims=True)
    cnt_ref[...] = base_scr[...].astype(jnp.int32)


def _merge(x2, ya, gas, pb, w_br_a, w_o, g_ffn, wr_hi, wr_lo, b_router, tm):
    n, d = x2.shape
    aw = ya.shape[1]

    def full(shape):
        return pl.BlockSpec(shape, lambda i: (0,) * len(shape))

    def rows(width):
        return pl.BlockSpec((tm, width), lambda i: (i, 0))

    return pl.pallas_call(
        _merge_kernel,
        grid=(n // tm,),
        in_specs=[rows(d), rows(aw), rows(d), rows(d), full(w_br_a.shape), full(w_o.shape),
                  full((1, d)), full(wr_hi.shape), full(wr_lo.shape), full((1, N_EXPERTS))],
        out_specs=[rows(d), rows(d // 2), rows(LANES), rows(LANES), rows(LANES),
                   full((1, N_EXPERTS))],
        out_shape=[jax.ShapeDtypeStruct((n, d), F32),
                   jax.ShapeDtypeStruct((n, d // 2), jnp.int32),
                   jax.ShapeDtypeStruct((n, LANES), jnp.int32),
                   jax.ShapeDtypeStruct((n, LANES), F32),
                   jax.ShapeDtypeStruct((n, LANES), jnp.int32),
                   jax.ShapeDtypeStruct((1, N_EXPERTS), jnp.int32)],
        scratch_shapes=[pltpu.VMEM((1, N_EXPERTS), F32)],
        compiler_params=pltpu.CompilerParams(dimension_semantics=("arbitrary",),
                                             vmem_limit_bytes=VMEM_LIMIT),
        name="merge_route",
    )(x2, ya, gas, pb, w_br_a, w_o, g_ffn, wr_hi, wr_lo, b_router)


def _sc_mesh():
    info = plsc.get_sparse_core_info()
    mesh = plsc.VectorSubcoreMesh(core_axis_name="c", subcore_axis_name="s")
    return mesh, info.num_cores, info.num_cores * info.num_subcores


def _sc_dispatch(src, dest, pad_idx, n_out_rows):
    n, w = src.shape
    mesh, n_cores, n_workers = _sc_mesh()
    per_w = n // n_workers
    n_chunks = per_w // SC_ROWS
    n_pad_chunks = pad_idx.size // (n_workers * SC_ROWS)

    @functools.partial(
        pl.kernel, mesh=mesh,
        out_type=jax.ShapeDtypeStruct((n_out_rows, w), jnp.int32),
        scratch_types=[pltpu.VMEM((n_chunks, TOP_K, SC_ROWS), jnp.int32),
                       pltpu.VMEM((n_pad_chunks, SC_ROWS), jnp.int32),
                       pltpu.VMEM((SC_ROWS, w), jnp.int32),
                       pltpu.VMEM((SC_ROWS, w), jnp.int32)],
        name="sc_dispatch",
    )
    def run(src_hbm, dest_hbm, pad_hbm, zeros_hbm, out_hbm, idx_v, pad_v, rows_v, zero_v):
        wid = lax.axis_index("s") * n_cores + lax.axis_index("c")
        pltpu.sync_copy(dest_hbm.at[wid], idx_v)
        pltpu.sync_copy(pad_hbm.at[wid], pad_v)
        pltpu.sync_copy(zeros_hbm, zero_v)

        @pl.loop(0, n_pad_chunks)
        def _(c):
            pltpu.sync_copy(zero_v, out_hbm.at[pad_v.at[c]])

        @pl.loop(0, n_chunks)
        def _(c):
            pltpu.sync_copy(src_hbm.at[pl.ds(wid * per_w + c * SC_ROWS, SC_ROWS)], rows_v)
            for k in range(TOP_K):
                pltpu.sync_copy(rows_v, out_hbm.at[idx_v.at[c, k]])

    dest_w = dest.reshape(n_workers, n_chunks, SC_ROWS, TOP_K).transpose(0, 1, 3, 2)
    return run(src, dest_w, pad_idx.reshape(n_workers, n_pad_chunks, SC_ROWS),
               jnp.zeros((SC_ROWS, w), jnp.int32))


def _sc_gather(table, idx):
    n_rows = idx.shape[0]
    w = table.shape[1]
    mesh, n_cores, n_workers = _sc_mesh()
    per_w = n_rows // n_workers
    n_chunks = per_w // SC_ROWS
    assert n_chunks % 2 == 0 and n_chunks * SC_ROWS * n_workers == n_rows

    @functools.partial(
        pl.kernel, mesh=mesh,
        out_type=jax.ShapeDtypeStruct((n_rows, w), table.dtype),
        scratch_types=[pltpu.VMEM((n_chunks, SC_ROWS), jnp.int32),
                       pltpu.VMEM((SC_ROWS, w), table.dtype),
                       pltpu.VMEM((SC_ROWS, w), table.dtype),
                       pltpu.SemaphoreType.DMA, pltpu.SemaphoreType.DMA],
        name="sc_gather",
    )
    def run(table_hbm, idx_hbm, out_hbm, idx_v, buf0, buf1, sem0, sem1):
        wid = lax.axis_index("s") * n_cores + lax.axis_index("c")
        base = wid * per_w
        pltpu.sync_copy(idx_hbm.at[wid], idx_v)

        def gather(c, buf, sem):
            return pltpu.make_async_copy(table_hbm.at[idx_v.at[c]], buf, sem)

        gather(0, buf0, sem0).start()

        @pl.loop(0, n_chunks, step=2)
        def _(c):
            gather(c + 1, buf1, sem1).start()
            gather(c, buf0, sem0).wait()
            pltpu.sync_copy(buf0, out_hbm.at[pl.ds(base + c * SC_ROWS, SC_ROWS)])

            @pl.when(c + 2 < n_chunks)
            def _():
                gather(c + 2, buf0, sem0).start()

            gather(c + 1, buf1, sem1).wait()
            pltpu.sync_copy(buf1, out_hbm.at[pl.ds(base + (c + 1) * SC_ROWS, SC_ROWS)])

    return run(table, idx.reshape(n_workers, n_chunks, SC_ROWS))


def _expert_kernel(be_ref, nu_ref, xs_ref, wgu_ref, bgu_ref, wd_ref, bd_ref, y_ref,
                   wgu_bf, wd_bf):
    i = pl.program_id(0)
    f = wd_ref.shape[1]
    half = xs_ref.shape[1]
    live = i < nu_ref[0]

    @pl.when(live & ((i == 0) | (be_ref[i] != be_ref[jnp.maximum(i - 1, 0)])))
    def _():
        wgu_bf[...] = wgu_ref[0].astype(BF16)
        wd_bf[...] = wd_ref[0].astype(BF16)

    @pl.when(live)
    def _():
        words = xs_ref[...]
        x_lo = lax.bitcast_convert_type(lax.shift_left(words, 16), F32).astype(BF16)
        x_hi = lax.bitcast_convert_type(words & jnp.int32(-65536), F32).astype(BF16)
        gu = (_dot(x_lo, wgu_bf[0:half, :]) + _dot(x_hi, wgu_bf[half:2 * half, :])
              + bgu_ref[0])
        gate = jnp.minimum(gu[:, :f], SWIGLU_LIMIT)
        up = jnp.clip(gu[:, f:], -SWIGLU_LIMIT, SWIGLU_LIMIT)
        act = (up + 1.0) * (gate * jax.nn.sigmoid(SWIGLU_ALPHA * gate))
        y_ref[...] = _dot(act.astype(BF16), wd_bf[...]) + bd_ref[0]


def _experts(block_e, n_used, xs, n_blocks, w_gu, b_gu, w_down, b_down):
    half = xs.shape[1]
    d = 2 * half
    f2 = w_gu.shape[2]
    f = w_down.shape[1]

    def blk(i, be, nu):
        return (jnp.minimum(i, nu[0] - 1), 0)

    def expert(i, be, nu):
        return (be[jnp.minimum(i, nu[0] - 1)], 0, 0)

    grid_spec = pltpu.PrefetchScalarGridSpec(
        num_scalar_prefetch=2,
        grid=(n_blocks,),
        in_specs=[pl.BlockSpec((EXPERT_BLOCK, half), blk),
                  pl.BlockSpec((1, d, f2), expert),
                  pl.BlockSpec((1, 1, f2), expert),
                  pl.BlockSpec((1, f, d), expert),
                  pl.BlockSpec((1, 1, d), expert)],
        out_specs=pl.BlockSpec((EXPERT_BLOCK, d), blk),
        scratch_shapes=[pltpu.VMEM((d, f2), BF16), pltpu.VMEM((f, d), BF16)],
    )
    return pl.pallas_call(
        _expert_kernel,
        grid_spec=grid_spec,
        out_shape=jax.ShapeDtypeStruct((n_blocks * EXPERT_BLOCK, d), F32),
        compiler_params=pltpu.CompilerParams(dimension_semantics=("arbitrary",),
                                             vmem_limit_bytes=VMEM_LIMIT),
        name="experts",
    )(block_e, n_used, xs, w_gu, b_gu, w_down, b_down)


def _combine_kernel(yg_ref, x1_ref, gate_ref, gfin_ref, out_ref):
    gates = gate_ref[...]
    acc = x1_ref[...]
    for k in range(TOP_K):
        acc = acc + gates[:, k:k + 1] * yg_ref[k]
    out_ref[...] = _rms(acc, gfin_ref[...])


def _combine(yg, x1, gates, g_final, tm):
    n, d = x1.shape
    return pl.pallas_call(
        _combine_kernel,
        grid=(n // tm,),
        in_specs=[pl.BlockSpec((TOP_K, tm, d), lambda i: (0, i, 0)),
                  pl.BlockSpec((tm, d), lambda i: (i, 0)),
                  pl.BlockSpec((tm, LANES), lambda i: (i, 0)),
                  pl.BlockSpec((1, d), lambda i: (0, 0))],
        out_specs=pl.BlockSpec((tm, d), lambda i: (i, 0)),
        out_shape=jax.ShapeDtypeStruct((n, d), F32),
        compiler_params=pltpu.CompilerParams(dimension_semantics=("parallel",),
                                             vmem_limit_bytes=VMEM_LIMIT),
        name="combine",
    )(yg, x1, gates, g_final)


def _pack_in_proj(w_in, d):
    offs = [0]
    for width in (Q_LORA, KV_LORA, IDX_DIM, IDX_HEADS, SGU_WIDTH, SGU_WIDTH, d, d):
        offs.append(offs[-1] + width)
    wq, wkv, wki, wwi, wu, wv, wga, wgb = [w_in[:, offs[j]:offs[j + 1]] for j in range(8)]
    wki_rep = jnp.tile(wki, (1, IDX_HEADS))
    wwi_pad = jnp.pad(wwi, ((0, 0), (0, LANES - IDX_HEADS)))
    return jnp.concatenate([wq, wkv, wki_rep, wwi_pad, wu, wv, wga, wgb], axis=1).astype(BF16)


def _layer(x, g_mix, w_in, g_cq, g_ckv, w_uq, w_uv, w_q_idx, g_kidx, b_kidx, g_sgu, b_sgu,
           w_spatial, b_spatial, w_br_a, w_br_b, w_o, g_ffn, w_router, b_router, w_gu, b_gu,
           w_down, b_down, g_final):
    b, s, d = x.shape
    n = b * s
    x2 = x.reshape(n, d)
    row = lambda v: v.reshape(1, -1).astype(F32)

    w_in_p = _pack_in_proj(w_in, d)
    ws_pair = w_spatial.reshape(SGU_GROUPS // 2, 2, SGU_CHUNK, SGU_CHUNK).transpose(
        0, 2, 1, 3).reshape(SGU_GROUPS // 2, SGU_CHUNK, 2 * SGU_CHUNK)
    bsp = jnp.repeat(b_spatial.T, SGU_GROUP_DIM, axis=1)
    cq, ckv, ki, wi, gas, pb = _inproj(
        x2, row(g_mix), w_in_p, row(g_cq), row(g_ckv), row(jnp.tile(g_kidx, IDX_HEADS)),
        row(jnp.tile(b_kidx, IDX_HEADS)), row(g_sgu), row(b_sgu), ws_pair, bsp,
        w_br_b.astype(BF16), tm=512)

    head_eye = jnp.eye(N_HEADS, dtype=F32)
    w_uv_pad = (w_uv[:, :, None, :] * head_eye[:, None, :, None]).reshape(
        N_HEADS, KV_LORA, N_HEADS * HEAD_DIM)
    n_buckets = 4 if (s // Q_BLOCK) % 4 == 0 else 1
    ya = _attn(cq.reshape(b, s, -1), ckv.reshape(b, s, -1), ki.reshape(b, s, -1),
               wi.reshape(b, s, -1), w_uq.astype(BF16), w_q_idx.astype(BF16),
               w_uv_pad.astype(BF16), n_buckets)

    wr_hi = w_router.astype(BF16)
    wr_lo = (w_router - wr_hi.astype(F32)).astype(BF16)
    x1, h2p, idx_p, gate_p, rank_p, counts = _merge(
        x2, ya.reshape(n, -1), gas, pb, w_br_a.astype(BF16), w_o.astype(BF16), row(g_ffn),
        wr_hi, wr_lo, row(b_router), tm=512)

    counts = counts[0]
    top_idx = idx_p[:, :TOP_K]
    padded = (counts + EXPERT_BLOCK - 1) // EXPERT_BLOCK * EXPERT_BLOCK
    padded_end = jnp.cumsum(padded)
    padded_start = padded_end - padded
    dest = padded_start[top_idx] + rank_p[:, :TOP_K]
    nk = n * TOP_K
    nb = -(-(nk + N_EXPERTS * EXPERT_BLOCK) // EXPERT_BLOCK)
    block_row0 = jnp.arange(nb, dtype=jnp.int32) * EXPERT_BLOCK
    block_e = jnp.minimum(
        jnp.sum((padded_end[None, :] <= block_row0[:, None]).astype(jnp.int32), axis=1),
        N_EXPERTS - 1).astype(jnp.int32)
    n_used = (padded_end[-1:] // EXPERT_BLOCK).astype(jnp.int32)
    j = jnp.arange(EXPERT_BLOCK, dtype=jnp.int32)[None, :]
    pad_idx = jnp.where(j < (padded - counts)[:, None], (padded_start + counts)[:, None] + j,
                        nb * EXPERT_BLOCK + j).astype(jnp.int32)

    xs = _sc_dispatch(h2p, dest, pad_idx, (nb + 1) * EXPERT_BLOCK)
    ybuf = _experts(block_e, n_used, xs, nb, w_gu, b_gu.reshape(N_EXPERTS, 1, -1), w_down,
                    b_down.reshape(N_EXPERTS, 1, -1))
    yg = _sc_gather(ybuf, dest.T.reshape(-1))
    out = _combine(yg.reshape(TOP_K, n, d), x1, gate_p, row(g_final), tm=256)
    return out.reshape(b, s, d)


def kernel(x, g_mix, w_in, g_cq, g_ckv, w_uq, w_uv, w_q_idx, g_kidx, b_kidx, g_sgu, b_sgu,
           w_spatial, b_spatial, w_br_a, w_br_b, w_o, g_ffn, w_router, b_router, w_gu, b_gu,
           w_down, b_down, g_final):
    assert g_mix.shape[0] == 1, "single-layer block"
    return _layer(x, g_mix[0], w_in[0], g_cq[0], g_ckv[0], w_uq[0], w_uv[0], w_q_idx[0],
                  g_kidx[0], b_kidx[0], g_sgu[0], b_sgu[0], w_spatial[0], b_spatial[0],
                  w_br_a[0], w_br_b[0], w_o[0], g_ffn[0], w_router[0], b_router[0], w_gu[0],
                  b_gu[0], w_down[0], b_down[0], g_final)
```

```python
import functools

import jax
import jax.numpy as jnp
from jax import lax
from jax.experimental import pallas as pl
from jax.experimental.pallas import tpu as pltpu
from jax.experimental.pallas import tpu_sc as plsc

EPS = 1e-6
CHUNK = 64
N_HEADS = 8
HEAD_DIM = 64
Q_LORA = 256
KV_LORA = 128
IDX_HEADS = 8
IDX_DIM = 32
TOPK_MAX = 256
Q_BLOCK = 128
ATTN_SCALE = KV_LORA ** -0.5
IDX_SCALE = (IDX_HEADS * IDX_DIM) ** -0.5
SGU_CHUNK = 128
SGU_GROUPS = 8
SGU_WIDTH = 512
SGU_GROUP_DIM = SGU_WIDTH // SGU_GROUPS
N_EXPERTS = 32
TOP_K = 4
SWIGLU_LIMIT = 7.0
SWIGLU_ALPHA = 1.702
EXPERT_BLOCK = 512

SC_ROWS = 32
LANES = 128
VMEM_LIMIT = 56 * 1024 * 1024
INT_MIN = -(2 ** 31)
CODE_NEG_INF = INT_MIN + 0x7FFFFF

BF16 = jnp.bfloat16
F32 = jnp.float32


def _dot(a, b):
    return jnp.dot(a, b, preferred_element_type=F32)


def _dot_nt(a, b):
    return lax.dot_general(a, b, (((1,), (1,)), ((), ())), preferred_element_type=F32)


def _rms(x, g):
    return x * lax.rsqrt(jnp.mean(x * x, axis=-1, keepdims=True) + EPS) * g


def _pack_bf16_pairs(x):
    half = x.shape[1] // 2
    bits = lax.bitcast_convert_type(x.astype(BF16).astype(F32), jnp.int32)
    return lax.shift_right_logical(bits[:, :half], 16) | (bits[:, half:] & jnp.int32(-65536))


def _unpack_bf16_pairs(words):
    lo = lax.bitcast_convert_type(lax.shift_left(words, 16), F32)
    hi = lax.bitcast_convert_type(words & jnp.int32(-65536), F32)
    return lo, hi


def _layer_norm(x, g, b):
    mu = jnp.mean(x, axis=-1, keepdims=True)
    xc = x - mu
    var = jnp.mean(xc * xc, axis=-1, keepdims=True)
    return xc * lax.rsqrt(var + EPS) * g + b


C_Q = 0
C_KV = C_Q + Q_LORA
C_KI = C_KV + KV_LORA
C_WI = C_KI + IDX_HEADS * IDX_DIM
C_U = C_WI + LANES
C_V = C_U + SGU_WIDTH
D_IN_P_BASE = C_V + SGU_WIDTH


def _inproj_kernel(x_ref, gmix_ref, w_ref, gcq_ref, gckv_ref, gki_ref, bki_ref,
                   gsgu_ref, bsgu_ref, wsp_ref, bsp_ref, wbrb_ref,
                   cq_ref, ckv_ref, ki_ref, wi_ref, gas_ref, pb_ref):
    tm, d = x_ref.shape
    c_ga = D_IN_P_BASE
    c_gb = c_ga + d
    h = _rms(x_ref[...], gmix_ref[...]).astype(BF16)

    def proj(lo, width):
        return _dot(h, w_ref[:, lo:lo + width])

    cq_ref[...] = _rms(proj(C_Q, Q_LORA), gcq_ref[...]).astype(BF16)
    ckv_ref[...] = _rms(proj(C_KV, KV_LORA), gckv_ref[...]).astype(BF16)
    ki_ref[...] = _layer_norm(proj(C_KI, IDX_HEADS * IDX_DIM), gki_ref[...],
                              bki_ref[...]).astype(BF16)
    wi_ref[...] = proj(C_WI, LANES) * IDX_SCALE
    gas_ref[...] = jax.nn.sigmoid(proj(c_ga, d)).astype(BF16)

    u = jax.nn.gelu(proj(C_U, SGU_WIDTH))
    v = _layer_norm(jax.nn.gelu(proj(C_V, SGU_WIDTH)), gsgu_ref[...], bsgu_ref[...])

    row = lax.broadcasted_iota(jnp.int32, (SGU_CHUNK, 2 * SGU_CHUNK), 0)
    col = lax.broadcasted_iota(jnp.int32, (SGU_CHUNK, 2 * SGU_CHUNK), 1) % SGU_CHUNK
    causal = (row // CHUNK) >= (col // CHUNK)
    lane = lax.broadcasted_iota(jnp.int32, (SGU_CHUNK, LANES), 1)
    left = lane < SGU_GROUP_DIM
    n_tiles = SGU_WIDTH // LANES
    ws = [jnp.where(causal, wsp_ref[j], 0.0).astype(BF16) for j in range(n_tiles)]
    yb_chunks = []
    for c in range(tm // SGU_CHUNK):
        tiles = []
        for j in range(n_tiles):
            blk = v[c * SGU_CHUNK:(c + 1) * SGU_CHUNK, j * LANES:(j + 1) * LANES]
            stacked = jnp.concatenate(
                [jnp.where(left, blk, 0.0), jnp.where(left, 0.0, blk)], axis=0).astype(BF16)
            tiles.append(_dot(ws[j], stacked))
        s = jnp.concatenate(tiles, axis=1) + bsp_ref[...]
        yb_chunks.append(u[c * SGU_CHUNK:(c + 1) * SGU_CHUNK, :] * s)
    yb = jnp.concatenate(yb_chunks, axis=0).astype(BF16)
    pb_ref[...] = (jax.nn.sigmoid(proj(c_gb, d)) * _dot(yb, wbrb_ref[...])).astype(BF16)


def _inproj(x2, g_mix, w_in_p, g_cq, g_ckv, g_ki, b_ki, g_sgu, b_sgu, ws_pair, bsp, w_br_b,
            tm):
    n, d = x2.shape
    d_in_p = w_in_p.shape[1]

    def full(shape):
        return pl.BlockSpec(shape, lambda i: (0,) * len(shape))

    def rows(width):
        return pl.BlockSpec((tm, width), lambda i: (i, 0))

    return pl.pallas_call(
        _inproj_kernel,
        grid=(n // tm,),
        in_specs=[rows(d), full((1, d)), full((d, d_in_p)), full((1, Q_LORA)),
                  full((1, KV_LORA)), full((1, IDX_HEADS * IDX_DIM)),
                  full((1, IDX_HEADS * IDX_DIM)), full((1, SGU_WIDTH)), full((1, SGU_WIDTH)),
                  full(ws_pair.shape), full(bsp.shape), full(w_br_b.shape)],
        out_specs=[rows(Q_LORA), rows(KV_LORA), rows(IDX_HEADS * IDX_DIM), rows(LANES),
                   rows(d), rows(d)],
        out_shape=[jax.ShapeDtypeStruct((n, Q_LORA), BF16),
                   jax.ShapeDtypeStruct((n, KV_LORA), BF16),
                   jax.ShapeDtypeStruct((n, IDX_HEADS * IDX_DIM), BF16),
                   jax.ShapeDtypeStruct((n, LANES), F32),
                   jax.ShapeDtypeStruct((n, d), BF16),
                   jax.ShapeDtypeStruct((n, d), BF16)],
        compiler_params=pltpu.CompilerParams(dimension_semantics=("parallel",),
                                             vmem_limit_bytes=VMEM_LIMIT),
        name="inproj",
    )(x2, g_mix, w_in_p, g_cq, g_ckv, g_ki, b_ki, g_sgu, b_sgu, ws_pair, bsp, w_br_b)


def _attn_body(s_eff, topk, cq_ref, ckv_ref, ki_ref, wi_ref, wuq_ref, wqi_ref, wuv_ref,
               ya_ref, q_scr, qi_scr, wcol_scr, sc_scr, bias_scr, acc_scr, s_buf, p_buf,
               kva_scr):
    i = pl.program_id(1)
    cq = cq_ref[0]
    q = _dot(cq, wuq_ref[...])
    qi = _dot(cq, wqi_ref[...]).astype(BF16)
    wi = wi_ref[0]
    head_of_lane = lax.broadcasted_iota(jnp.int32, qi.shape, 1) // IDX_DIM
    for h in range(N_HEADS):
        q_scr[h] = (q[:, h * KV_LORA:(h + 1) * KV_LORA] * ATTN_SCALE).astype(BF16)
    for h in range(IDX_HEADS):
        qi_scr[h] = jnp.where(head_of_lane == h, qi, jnp.zeros_like(qi))
        wcol_scr[h] = wi[:, h:h + 1]

    def index_dots(h):
        s_buf[h % 2] = _dot_nt(qi_scr[h], ki_ref[0, 0:s_eff, :])

    index_dots(0)
    for h in range(IDX_HEADS):
        if h + 1 < IDX_HEADS:
            index_dots(h + 1)
        term = wcol_scr[h] * jnp.maximum(s_buf[h % 2], 0.0)
        bias_scr[...] = term if h == 0 else bias_scr[...] + term

    rowq = lax.broadcasted_iota(jnp.int32, (Q_BLOCK, s_eff), 0)
    colk = lax.broadcasted_iota(jnp.int32, (Q_BLOCK, s_eff), 1)
    allowed = (colk // CHUNK) <= ((i * Q_BLOCK + rowq) // CHUNK)
    sc_scr[...] = jnp.where(allowed, bias_scr[...], -jnp.inf)

    def code_to_float(code):
        bits = jnp.where(code < 0, code ^ jnp.int32(0x7FFFFFFF), code)
        return lax.bitcast_convert_type(bits, F32)

    def count_ge(thr):
        return jnp.sum(jnp.where(sc_scr[...] >= thr, 1.0, 0.0), axis=1, keepdims=True)

    kf = float(topk)
    code = jnp.where(count_ge(jnp.zeros((Q_BLOCK, 1), F32)) >= kf,
                     jnp.int32(0), jnp.int32(INT_MIN))
    for bit in range(30, -1, -1):
        cand = code + jnp.int32(1 << bit)
        feasible = (count_ge(code_to_float(cand)) >= kf) | (cand <= jnp.int32(CODE_NEG_INF))
        code = jnp.where(feasible, cand, code)

    thr = code_to_float(code)
    thr_up = code_to_float(code + 1)
    sel = sc_scr[...] >= thr
    n_ge = jnp.sum(jnp.where(sel, 1.0, 0.0), axis=1, keepdims=True)
    bias_scr[...] = jnp.where(sel & allowed, 0.0, -jnp.inf)
    tie_rows = (n_ge > kf) & (code > jnp.int32(CODE_NEG_INF))
    any_tie = jnp.max(jnp.where(tie_rows, 1.0, 0.0))

    @pl.when(any_tie > 0.0)
    def _():
        sc = sc_scr[...]
        gt = sc >= thr_up
        eq = (sc >= thr) & jnp.logical_not(gt)
        need = kf - jnp.sum(jnp.where(gt, 1.0, 0.0), axis=1, keepdims=True)
        nbits = (s_eff - 1).bit_length()

        def tie_step(it, j):
            cand = j + jnp.left_shift(jnp.int32(1), jnp.int32(nbits - 1) - it)
            v = sc_scr[...]
            tie = (v >= thr) & jnp.logical_not(v >= thr_up) & (colk < cand)
            cnt = jnp.sum(jnp.where(tie, 1.0, 0.0), axis=1, keepdims=True)
            return jnp.where(cnt < need, cand, j)

        jmax = lax.fori_loop(0, nbits, tie_step, jnp.zeros((Q_BLOCK, 1), jnp.int32))
        keep = gt | (eq & (colk <= jmax))
        bias_scr[...] = jnp.where(keep & allowed, 0.0, -jnp.inf)

    acc_scr[...] = jnp.zeros_like(acc_scr)

    def logits(h):
        s_buf[h % 2] = _dot_nt(q_scr[h], ckv_ref[0, 0:s_eff, :]) + bias_scr[...]

    def softmax(h):
        s = s_buf[h % 2]
        p_buf[h % 2] = jnp.exp(s - jnp.max(s, axis=1, keepdims=True)).astype(BF16)

    def values(h):
        ol = _dot(p_buf[h % 2], kva_scr[0:s_eff, :])
        o = ol[:, :KV_LORA] / ol[:, KV_LORA:]
        acc_scr[...] += _dot(o.astype(BF16), wuv_ref[h])

    logits(0)
    for h in range(N_HEADS):
        if h + 1 < N_HEADS:
            logits(h + 1)
        softmax(h)
        values(h)
    ya_ref[0] = acc_scr[...].astype(BF16)


def _attn_kernel(n_buckets, topk, cq_ref, ckv_ref, ki_ref, wi_ref, wuq_ref, wqi_ref, wuv_ref,
                 ya_ref, q_scr, qi_scr, wcol_scr, sc_scr, bias_scr, acc_scr, s_buf, p_buf,
                 kva_scr):
    s = ckv_ref.shape[1]
    nq = s // Q_BLOCK
    per = nq // n_buckets
    i = pl.program_id(1)

    @pl.when(i == 0)
    def _():
        kva_scr[:, 0:KV_LORA] = ckv_ref[0]
        kva_scr[:, KV_LORA:] = jnp.ones((s, KV_LORA), BF16)

    for k in range(n_buckets):
        s_eff = (k + 1) * per * Q_BLOCK

        @pl.when((i >= k * per) & (i < (k + 1) * per))
        def _(s_eff=s_eff):
            _attn_body(s_eff, topk, cq_ref, ckv_ref, ki_ref, wi_ref, wuq_ref, wqi_ref,
                       wuv_ref, ya_ref, q_scr, qi_scr, wcol_scr,
                       sc_scr.at[:, 0:s_eff], bias_scr.at[:, 0:s_eff], acc_scr,
                       s_buf.at[:, :, 0:s_eff], p_buf.at[:, :, 0:s_eff], kva_scr)


def _attn(cq, ckv, ki, wi, w_uq, w_qi, w_uv_pad, n_buckets):
    b, s, _ = cq.shape
    nq = s // Q_BLOCK
    topk = min(TOPK_MAX, s // 4)
    aw = N_HEADS * HEAD_DIM

    def full(shape):
        return pl.BlockSpec(shape, lambda bi, i: (0,) * len(shape))

    return pl.pallas_call(
        functools.partial(_attn_kernel, n_buckets, topk),
        grid=(b, nq),
        in_specs=[pl.BlockSpec((1, Q_BLOCK, Q_LORA), lambda bi, i: (bi, i, 0)),
                  pl.BlockSpec((1, s, KV_LORA), lambda bi, i: (bi, 0, 0)),
                  pl.BlockSpec((1, s, IDX_HEADS * IDX_DIM), lambda bi, i: (bi, 0, 0)),
                  pl.BlockSpec((1, Q_BLOCK, LANES), lambda bi, i: (bi, i, 0)),
                  full(w_uq.shape), full(w_qi.shape), full(w_uv_pad.shape)],
        out_specs=pl.BlockSpec((1, Q_BLOCK, aw), lambda bi, i: (bi, i, 0)),
        out_shape=jax.ShapeDtypeStruct((b, s, aw), BF16),
        scratch_shapes=[pltpu.VMEM((N_HEADS, Q_BLOCK, KV_LORA), BF16),
                        pltpu.VMEM((IDX_HEADS, Q_BLOCK, IDX_HEADS * IDX_DIM), BF16),
                        pltpu.VMEM((IDX_HEADS, Q_BLOCK, 1), F32),
                        pltpu.VMEM((Q_BLOCK, s), F32),
                        pltpu.VMEM((Q_BLOCK, s), F32),
                        pltpu.VMEM((Q_BLOCK, aw), F32),
                        pltpu.VMEM((2, Q_BLOCK, s), F32),
                        pltpu.VMEM((2, Q_BLOCK, s), BF16),
                        pltpu.VMEM((s, 2 * KV_LORA), BF16)],
        compiler_params=pltpu.CompilerParams(dimension_semantics=("parallel", "arbitrary"),
                                             vmem_limit_bytes=VMEM_LIMIT),
        name="attn",
    )(cq, ckv, ki, wi, w_uq, w_qi, w_uv_pad)


def _merge_kernel(x_ref, ya_ref, gas_ref, pb_ref, wbra_ref, wo_ref, gffn_ref,
                  wr_hi_ref, wr_lo_ref, br_ref,
                  x1_ref, h2p_ref, idx_ref, gate_ref, rank_ref, cnt_ref, base_scr):
    step = pl.program_id(0)
    tm, d = x_ref.shape

    @pl.when(step == 0)
    def _():
        base_scr[...] = jnp.zeros_like(base_scr)

    a = _dot(ya_ref[...], wbra_ref[...])
    merged = gas_ref[...].astype(F32) * a + pb_ref[...].astype(F32)
    x1 = x_ref[...] + _dot(merged.astype(BF16), wo_ref[...])
    x1_ref[...] = x1
    h2 = _rms(x1, gffn_ref[...])

    h_hi = h2.astype(BF16)
    h2p_ref[...] = _pack_bf16_pairs(h2)
    h_lo = (h2 - h_hi.astype(F32)).astype(BF16)
    logits = (_dot(h_hi, wr_hi_ref[...]) + _dot(h_hi, wr_lo_ref[...])
              + _dot(h_lo, wr_hi_ref[...]) + br_ref[...])

    lane_e = lax.broadcasted_iota(jnp.int32, (tm, N_EXPERTS), 1).astype(F32)
    lane_o = lax.broadcasted_iota(jnp.int32, (tm, LANES), 1)
    work = logits
    vals, idxs = [], []
    onehot = jnp.zeros((tm, N_EXPERTS), F32)
    for _ in range(TOP_K):
        m = jnp.max(work, axis=1, keepdims=True)
        idx = jnp.min(jnp.where(work == m, lane_e, float(N_EXPERTS)), axis=1, keepdims=True)
        hit = lane_e == idx
        onehot = onehot + jnp.where(hit, 1.0, 0.0)
        work = jnp.where(hit, -jnp.inf, work)
        vals.append(m)
        idxs.append(idx)
    exps = [jnp.exp(v - vals[0]) for v in vals]
    denom = exps[0] + exps[1] + exps[2] + exps[3]

    r = lax.broadcasted_iota(jnp.int32, (tm, tm), 0)
    c = lax.broadcasted_iota(jnp.int32, (tm, tm), 1)
    tri = jnp.where(c < r, 1.0, 0.0).astype(BF16)
    rank_full = _dot(tri, onehot.astype(BF16)) + base_scr[...]

    idx_out = jnp.zeros((tm, LANES), jnp.int32)
    gate_out = jnp.zeros((tm, LANES), F32)
    rank_out = jnp.zeros((tm, LANES), jnp.int32)
    for k in range(TOP_K):
        rk = jnp.sum(jnp.where(lane_e == idxs[k], rank_full, 0.0), axis=1, keepdims=True)
        idx_out = jnp.where(lane_o == k, idxs[k].astype(jnp.int32), idx_out)
        gate_out = jnp.where(lane_o == k, exps[k] / denom, gate_out)
        rank_out = jnp.where(lane_o == k, rk.astype(jnp.int32), rank_out)
    idx_ref[...] = idx_out
    gate_ref[...] = gate_out
    rank_ref[...] = rank_out

    base_scr[...] = base_scr[...] + jnp.sum(onehot, axis=0, keepdims=True)
    cnt_ref[...] = base_scr[...].astype(jnp.int32)


def _merge(x2, ya, gas, pb, w_br_a, w_o, g_ffn, wr_hi, wr_lo, b_router, tm):
    n, d = x2.shape
    aw = ya.shape[1]

    def full(shape):
        return pl.BlockSpec(shape, lambda i: (0,) * len(shape))

    def rows(width):
        return pl.BlockSpec((tm, width), lambda i: (i, 0))

    return pl.pallas_call(
        _merge_kernel,
        grid=(n // tm,),
        in_specs=[rows(d), rows(aw), rows(d), rows(d), full(w_br_a.shape), full(w_o.shape),
                  full((1, d)), full(wr_hi.shape), full(wr_lo.shape), full((1, N_EXPERTS))],
        out_specs=[rows(d), rows(d // 2), rows(LANES), rows(LANES), rows(LANES),
                   full((1, N_EXPERTS))],
        out_shape=[jax.ShapeDtypeStruct((n, d), F32),
                   jax.ShapeDtypeStruct((n, d // 2), jnp.int32),
                   jax.ShapeDtypeStruct((n, LANES), jnp.int32),
                   jax.ShapeDtypeStruct((n, LANES), F32),
                   jax.ShapeDtypeStruct((n, LANES), jnp.int32),
                   jax.ShapeDtypeStruct((1, N_EXPERTS), jnp.int32)],
        scratch_shapes=[pltpu.VMEM((1, N_EXPERTS), F32)],
        compiler_params=pltpu.CompilerParams(dimension_semantics=("arbitrary",),
                                             vmem_limit_bytes=VMEM_LIMIT),
        name="merge_route",
    )(x2, ya, gas, pb, w_br_a, w_o, g_ffn, wr_hi, wr_lo, b_router)


def _sc_mesh():
    info = plsc.get_sparse_core_info()
    mesh = plsc.VectorSubcoreMesh(core_axis_name="c", subcore_axis_name="s")
    return mesh, info.num_cores, info.num_cores * info.num_subcores


def _sc_dispatch(src, dest, pad_idx, n_out_rows):
    n, w = src.shape
    mesh, n_cores, n_workers = _sc_mesh()
    per_w = n // n_workers
    n_chunks = per_w // SC_ROWS
    n_pad_chunks = pad_idx.size // (n_workers * SC_ROWS)

    @functools.partial(
        pl.kernel, mesh=mesh,
        out_type=jax.ShapeDtypeStruct((n_out_rows, w), jnp.int32),
        scratch_types=[pltpu.VMEM((n_chunks, TOP_K, SC_ROWS), jnp.int32),
                       pltpu.VMEM((n_pad_chunks, SC_ROWS), jnp.int32),
                       pltpu.VMEM((SC_ROWS, w), jnp.int32),
                       pltpu.VMEM((SC_ROWS, w), jnp.int32)],
        name="sc_dispatch",
    )
    def run(src_hbm, dest_hbm, pad_hbm, zeros_hbm, out_hbm, idx_v, pad_v, rows_v, zero_v):
        wid = lax.axis_index("s") * n_cores + lax.axis_index("c")
        pltpu.sync_copy(dest_hbm.at[wid], idx_v)
        pltpu.sync_copy(pad_hbm.at[wid], pad_v)
        pltpu.sync_copy(zeros_hbm, zero_v)

        @pl.loop(0, n_pad_chunks)
        def _(c):
            pltpu.sync_copy(zero_v, out_hbm.at[pad_v.at[c]])

        @pl.loop(0, n_chunks)
        def _(c):
            pltpu.sync_copy(src_hbm.at[pl.ds(wid * per_w + c * SC_ROWS, SC_ROWS)], rows_v)
            for k in range(TOP_K):
                pltpu.sync_copy(rows_v, out_hbm.at[idx_v.at[c, k]])

    dest_w = dest.reshape(n_workers, n_chunks, SC_ROWS, TOP_K).transpose(0, 1, 3, 2)
    return run(src, dest_w, pad_idx.reshape(n_workers, n_pad_chunks, SC_ROWS),
               jnp.zeros((SC_ROWS, w), jnp.int32))


def _sc_gather(table, idx):
    n_rows = idx.shape[0]
    w = table.shape[1]
    mesh, n_cores, n_workers = _sc_mesh()
    per_w = n_rows // n_workers
    n_chunks = per_w // SC_ROWS
    assert n_chunks % 2 == 0 and n_chunks * SC_ROWS * n_workers == n_rows

    @functools.partial(
        pl.kernel, mesh=mesh,
        out_type=jax.ShapeDtypeStruct((n_rows, w), table.dtype),
        scratch_types=[pltpu.VMEM((n_chunks, SC_ROWS), jnp.int32),
                       pltpu.VMEM((SC_ROWS, w), table.dtype),
                       pltpu.VMEM((SC_ROWS, w), table.dtype),
                       pltpu.SemaphoreType.DMA, pltpu.SemaphoreType.DMA],
        name="sc_gather",
    )
    def run(table_hbm, idx_hbm, out_hbm, idx_v, buf0, buf1, sem0, sem1):
        wid = lax.axis_index("s") * n_cores + lax.axis_index("c")
        base = wid * per_w
        pltpu.sync_copy(idx_hbm.at[wid], idx_v)

        def gather(c, buf, sem):
            return pltpu.make_async_copy(table_hbm.at[idx_v.at[c]], buf, sem)

        gather(0, buf0, sem0).start()

        @pl.loop(0, n_chunks, step=2)
        def _(c):
            gather(c + 1, buf1, sem1).start()
            gather(c, buf0, sem0).wait()
            pltpu.sync_copy(buf0, out_hbm.at[pl.ds(base + c * SC_ROWS, SC_ROWS)])

            @pl.when(c + 2 < n_chunks)
            def _():
                gather(c + 2, buf0, sem0).start()

            gather(c + 1, buf1, sem1).wait()
            pltpu.sync_copy(buf1, out_hbm.at[pl.ds(base + (c + 1) * SC_ROWS, SC_ROWS)])

    return run(table, idx.reshape(n_workers, n_chunks, SC_ROWS))


def _expert_kernel(be_ref, nu_ref, xs_ref, wgu_ref, bgu_ref, wd_ref, bd_ref, y_ref,
                   wgu_bf, wd_bf):
    i = pl.program_id(0)
    f = wd_ref.shape[1]
    half = xs_ref.shape[1]
    live = i < nu_ref[0]

    @pl.when(live & ((i == 0) | (be_ref[i] != be_ref[jnp.maximum(i - 1, 0)])))
    def _():
        wgu_bf[...] = wgu_ref[0].astype(BF16)
        wd_bf[...] = wd_ref[0].astype(BF16)

    @pl.when(live)
    def _():
        x_lo, x_hi = _unpack_bf16_pairs(xs_ref[...])
        gu = (_dot(x_lo.astype(BF16), wgu_bf[0:half, :])
              + _dot(x_hi.astype(BF16), wgu_bf[half:2 * half, :])
              + bgu_ref[0])
        gate = jnp.minimum(gu[:, :f], SWIGLU_LIMIT)
        up = jnp.clip(gu[:, f:], -SWIGLU_LIMIT, SWIGLU_LIMIT)
        act = (up + 1.0) * (gate * jax.nn.sigmoid(SWIGLU_ALPHA * gate))
        y = _dot(act.astype(BF16), wd_bf[...]) + bd_ref[0]
        y_ref[...] = _pack_bf16_pairs(y)


def _experts(block_e, n_used, xs, n_blocks, w_gu, b_gu, w_down, b_down):
    half = xs.shape[1]
    d = 2 * half
    f2 = w_gu.shape[2]
    f = w_down.shape[1]

    def blk(i, be, nu):
        return (jnp.minimum(i, nu[0] - 1), 0)

    def expert(i, be, nu):
        return (be[jnp.minimum(i, nu[0] - 1)], 0, 0)

    grid_spec = pltpu.PrefetchScalarGridSpec(
        num_scalar_prefetch=2,
        grid=(n_blocks,),
        in_specs=[pl.BlockSpec((EXPERT_BLOCK, half), blk),
                  pl.BlockSpec((1, d, f2), expert),
                  pl.BlockSpec((1, 1, f2), expert),
                  pl.BlockSpec((1, f, d), expert),
                  pl.BlockSpec((1, 1, d), expert)],
        out_specs=pl.BlockSpec((EXPERT_BLOCK, half), blk),
        scratch_shapes=[pltpu.VMEM((d, f2), BF16), pltpu.VMEM((f, d), BF16)],
    )
    return pl.pallas_call(
        _expert_kernel,
        grid_spec=grid_spec,
        out_shape=jax.ShapeDtypeStruct((n_blocks * EXPERT_BLOCK, half), jnp.int32),
        compiler_params=pltpu.CompilerParams(dimension_semantics=("arbitrary",),
                                             vmem_limit_bytes=VMEM_LIMIT),
        name="experts",
    )(block_e, n_used, xs, w_gu, b_gu, w_down, b_down)


def _combine_kernel(yg_ref, x1_ref, gate_ref, gfin_ref, out_ref):
    gates = gate_ref[...]
    half = yg_ref.shape[2]
    acc_lo = x1_ref[:, :half]
    acc_hi = x1_ref[:, half:]
    for k in range(TOP_K):
        y_lo, y_hi = _unpack_bf16_pairs(yg_ref[k])
        acc_lo = acc_lo + gates[:, k:k + 1] * y_lo
        acc_hi = acc_hi + gates[:, k:k + 1] * y_hi
    out_ref[...] = _rms(jnp.concatenate([acc_lo, acc_hi], axis=1), gfin_ref[...])


def _combine(yg, x1, gates, g_final, tm):
    n, d = x1.shape
    return pl.pallas_call(
        _combine_kernel,
        grid=(n // tm,),
        in_specs=[pl.BlockSpec((TOP_K, tm, d // 2), lambda i: (0, i, 0)),
                  pl.BlockSpec((tm, d), lambda i: (i, 0)),
                  pl.BlockSpec((tm, LANES), lambda i: (i, 0)),
                  pl.BlockSpec((1, d), lambda i: (0, 0))],
        out_specs=pl.BlockSpec((tm, d), lambda i: (i, 0)),
        out_shape=jax.ShapeDtypeStruct((n, d), F32),
        compiler_params=pltpu.CompilerParams(dimension_semantics=("parallel",),
                                             vmem_limit_bytes=VMEM_LIMIT),
        name="combine",
    )(yg, x1, gates, g_final)


def _pack_in_proj(w_in, d):
    offs = [0]
    for width in (Q_LORA, KV_LORA, IDX_DIM, IDX_HEADS, SGU_WIDTH, SGU_WIDTH, d, d):
        offs.append(offs[-1] + width)
    wq, wkv, wki, wwi, wu, wv, wga, wgb = [w_in[:, offs[j]:offs[j + 1]] for j in range(8)]
    wki_rep = jnp.tile(wki, (1, IDX_HEADS))
    wwi_pad = jnp.pad(wwi, ((0, 0), (0, LANES - IDX_HEADS)))
    return jnp.concatenate([wq, wkv, wki_rep, wwi_pad, wu, wv, wga, wgb], axis=1).astype(BF16)


def _layer(x, g_mix, w_in, g_cq, g_ckv, w_uq, w_uv, w_q_idx, g_kidx, b_kidx, g_sgu, b_sgu,
           w_spatial, b_spatial, w_br_a, w_br_b, w_o, g_ffn, w_router, b_router, w_gu, b_gu,
           w_down, b_down, g_final):
    b, s, d = x.shape
    n = b * s
    x2 = x.reshape(n, d)
    row = lambda v: v.reshape(1, -1).astype(F32)

    w_in_p = _pack_in_proj(w_in, d)
    ws_pair = w_spatial.reshape(SGU_GROUPS // 2, 2, SGU_CHUNK, SGU_CHUNK).transpose(
        0, 2, 1, 3).reshape(SGU_GROUPS // 2, SGU_CHUNK, 2 * SGU_CHUNK)
    bsp = jnp.repeat(b_spatial.T, SGU_GROUP_DIM, axis=1)
    cq, ckv, ki, wi, gas, pb = _inproj(
        x2, row(g_mix), w_in_p, row(g_cq), row(g_ckv), row(jnp.tile(g_kidx, IDX_HEADS)),
        row(jnp.tile(b_kidx, IDX_HEADS)), row(g_sgu), row(b_sgu), ws_pair, bsp,
        w_br_b.astype(BF16), tm=512)

    head_eye = jnp.eye(N_HEADS, dtype=F32)
    w_uv_pad = (w_uv[:, :, None, :] * head_eye[:, None, :, None]).reshape(
        N_HEADS, KV_LORA, N_HEADS * HEAD_DIM)
    n_buckets = next(nbk for nbk in (8, 4, 2, 1) if (s // Q_BLOCK) % nbk == 0)
    ya = _attn(cq.reshape(b, s, -1), ckv.reshape(b, s, -1), ki.reshape(b, s, -1),
               wi.reshape(b, s, -1), w_uq.astype(BF16), w_q_idx.astype(BF16),
               w_uv_pad.astype(BF16), n_buckets)

    wr_hi = w_router.astype(BF16)
    wr_lo = (w_router - wr_hi.astype(F32)).astype(BF16)
    x1, h2p, idx_p, gate_p, rank_p, counts = _merge(
        x2, ya.reshape(n, -1), gas, pb, w_br_a.astype(BF16), w_o.astype(BF16), row(g_ffn),
        wr_hi, wr_lo, row(b_router), tm=512)

    counts = counts[0]
    top_idx = idx_p[:, :TOP_K]
    padded = (counts + EXPERT_BLOCK - 1) // EXPERT_BLOCK * EXPERT_BLOCK
    padded_end = jnp.cumsum(padded)
    padded_start = padded_end - padded
    dest = padded_start[top_idx] + rank_p[:, :TOP_K]
    nk = n * TOP_K
    nb = -(-(nk + N_EXPERTS * EXPERT_BLOCK) // EXPERT_BLOCK)
    block_row0 = jnp.arange(nb, dtype=jnp.int32) * EXPERT_BLOCK
    block_e = jnp.minimum(
        jnp.sum((padded_end[None, :] <= block_row0[:, None]).astype(jnp.int32), axis=1),
        N_EXPERTS - 1).astype(jnp.int32)
    n_used = (padded_end[-1:] // EXPERT_BLOCK).astype(jnp.int32)
    j = jnp.arange(EXPERT_BLOCK, dtype=jnp.int32)[None, :]
    pad_idx = jnp.where(j < (padded - counts)[:, None], (padded_start + counts)[:, None] + j,
                        nb * EXPERT_BLOCK + j).astype(jnp.int32)

    xs = _sc_dispatch(h2p, dest, pad_idx, (nb + 1) * EXPERT_BLOCK)
    ybuf = _experts(block_e, n_used, xs, nb, w_gu, b_gu.reshape(N_EXPERTS, 1, -1), w_down,
                    b_down.reshape(N_EXPERTS, 1, -1))
    yg = _sc_gather(ybuf, dest.T.reshape(-1))
    out = _combine(yg.reshape(TOP_K, n, d // 2), x1, gate_p, row(g_final), tm=256)
    return out.reshape(b, s, d)


def kernel(x, g_mix, w_in, g_cq, g_ckv, w_uq, w_uv, w_q_idx, g_kidx, b_kidx, g_sgu, b_sgu,
           w_spatial, b_spatial, w_br_a, w_br_b, w_o, g_ffn, w_router, b_router, w_gu, b_gu,
           w_down, b_down, g_final):
    assert g_mix.shape[0] == 1, "single-layer block"
    return _layer(x, g_mix[0], w_in[0], g_cq[0], g_ckv[0], w_uq[0], w_uv[0], w_q_idx[0],
                  g_kidx[0], b_kidx[0], g_sgu[0], b_sgu[0], w_spatial[0], b_spatial[0],
                  w_br_a[0], w_br_b[0], w_o[0], g_ffn[0], w_router[0], b_router[0], w_gu[0],
                  b_gu[0], w_down[0], b_down[0], g_final)
```

```python
import functools

import jax
import jax.numpy as jnp
from jax import lax
from jax.experimental import pallas as pl
from jax.experimental.pallas import tpu as pltpu
from jax.experimental.pallas import tpu_sc as plsc

EPS = 1e-6
CHUNK = 64
N_HEADS = 8
HEAD_DIM = 64
Q_LORA = 256
KV_LORA = 128
IDX_HEADS = 8
IDX_DIM = 32
TOPK_MAX = 256
Q_BLOCK = 128
ATTN_SCALE = KV_LORA ** -0.5
IDX_SCALE = (IDX_HEADS * IDX_DIM) ** -0.5
SGU_CHUNK = 128
SGU_GROUPS = 8
SGU_WIDTH = 512
SGU_GROUP_DIM = SGU_WIDTH // SGU_GROUPS
N_EXPERTS = 32
TOP_K = 4
SWIGLU_LIMIT = 7.0
SWIGLU_ALPHA = 1.702
EXPERT_BLOCK = 512

SC_ROWS = 32
LANES = 128
VMEM_LIMIT = 56 * 1024 * 1024
INT_MIN = -(2 ** 31)
CODE_NEG_INF = INT_MIN + 0x7FFFFF

BF16 = jnp.bfloat16
F32 = jnp.float32


def _dot(a, b):
    return jnp.dot(a, b, preferred_element_type=F32)


def _dot_nt(a, b):
    return lax.dot_general(a, b, (((1,), (1,)), ((), ())), preferred_element_type=F32)


def _rms(x, g):
    return x * lax.rsqrt(jnp.mean(x * x, axis=-1, keepdims=True) + EPS) * g


def _pack_bf16_pairs(x):
    half = x.shape[1] // 2
    bits = lax.bitcast_convert_type(x.astype(BF16).astype(F32), jnp.int32)
    return lax.shift_right_logical(bits[:, :half], 16) | (bits[:, half:] & jnp.int32(-65536))


def _unpack_bf16_pairs(words):
    lo = lax.bitcast_convert_type(lax.shift_left(words, 16), F32)
    hi = lax.bitcast_convert_type(words & jnp.int32(-65536), F32)
    return lo, hi


def _layer_norm(x, g, b):
    mu = jnp.mean(x, axis=-1, keepdims=True)
    xc = x - mu
    var = jnp.mean(xc * xc, axis=-1, keepdims=True)
    return xc * lax.rsqrt(var + EPS) * g + b


C_Q = 0
C_KV = C_Q + Q_LORA
C_KI = C_KV + KV_LORA
C_WI = C_KI + IDX_HEADS * IDX_DIM
C_U = C_WI + LANES
C_V = C_U + SGU_WIDTH
D_IN_P_BASE = C_V + SGU_WIDTH


def _inproj_kernel(x_ref, gmix_ref, w_ref, gcq_ref, gckv_ref, gki_ref, bki_ref,
                   gsgu_ref, bsgu_ref, wsp_ref, bsp_ref, wbrb_ref,
                   cq_ref, ckv_ref, ki_ref, wi_ref, gas_ref, pb_ref):
    tm, d = x_ref.shape
    c_ga = D_IN_P_BASE
    c_gb = c_ga + d
    h = _rms(x_ref[...], gmix_ref[...]).astype(BF16)

    def proj(lo, width):
        return _dot(h, w_ref[:, lo:lo + width])

    cq_ref[...] = _rms(proj(C_Q, Q_LORA), gcq_ref[...]).astype(BF16)
    ckv_ref[...] = _rms(proj(C_KV, KV_LORA), gckv_ref[...]).astype(BF16)
    ki_ref[...] = _layer_norm(proj(C_KI, IDX_HEADS * IDX_DIM), gki_ref[...],
                              bki_ref[...]).astype(BF16)
    wi_ref[...] = proj(C_WI, LANES) * IDX_SCALE
    gas_ref[...] = jax.nn.sigmoid(proj(c_ga, d)).astype(BF16)

    u = jax.nn.gelu(proj(C_U, SGU_WIDTH))
    v = _layer_norm(jax.nn.gelu(proj(C_V, SGU_WIDTH)), gsgu_ref[...], bsgu_ref[...])

    row = lax.broadcasted_iota(jnp.int32, (SGU_CHUNK, 2 * SGU_CHUNK), 0)
    col = lax.broadcasted_iota(jnp.int32, (SGU_CHUNK, 2 * SGU_CHUNK), 1) % SGU_CHUNK
    causal = (row // CHUNK) >= (col // CHUNK)
    lane = lax.broadcasted_iota(jnp.int32, (SGU_CHUNK, LANES), 1)
    left = lane < SGU_GROUP_DIM
    n_tiles = SGU_WIDTH // LANES
    ws = [jnp.where(causal, wsp_ref[j], 0.0).astype(BF16) for j in range(n_tiles)]
    yb_chunks = []
    for c in range(tm // SGU_CHUNK):
        tiles = []
        for j in range(n_tiles):
            blk = v[c * SGU_CHUNK:(c + 1) * SGU_CHUNK, j * LANES:(j + 1) * LANES]
            stacked = jnp.concatenate(
                [jnp.where(left, blk, 0.0), jnp.where(left, 0.0, blk)], axis=0).astype(BF16)
            tiles.append(_dot(ws[j], stacked))
        s = jnp.concatenate(tiles, axis=1) + bsp_ref[...]
        yb_chunks.append(u[c * SGU_CHUNK:(c + 1) * SGU_CHUNK, :] * s)
    yb = jnp.concatenate(yb_chunks, axis=0).astype(BF16)
    pb_ref[...] = (jax.nn.sigmoid(proj(c_gb, d)) * _dot(yb, wbrb_ref[...])).astype(BF16)


def _inproj(x2, g_mix, w_in_p, g_cq, g_ckv, g_ki, b_ki, g_sgu, b_sgu, ws_pair, bsp, w_br_b,
            tm):
    n, d = x2.shape
    d_in_p = w_in_p.shape[1]

    def full(shape):
        return pl.BlockSpec(shape, lambda i: (0,) * len(shape))

    def rows(width):
        return pl.BlockSpec((tm, width), lambda i: (i, 0))

    return pl.pallas_call(
        _inproj_kernel,
        grid=(n // tm,),
        in_specs=[rows(d), full((1, d)), full((d, d_in_p)), full((1, Q_LORA)),
                  full((1, KV_LORA)), full((1, IDX_HEADS * IDX_DIM)),
                  full((1, IDX_HEADS * IDX_DIM)), full((1, SGU_WIDTH)), full((1, SGU_WIDTH)),
                  full(ws_pair.shape), full(bsp.shape), full(w_br_b.shape)],
        out_specs=[rows(Q_LORA), rows(KV_LORA), rows(IDX_HEADS * IDX_DIM), rows(LANES),
                   rows(d), rows(d)],
        out_shape=[jax.ShapeDtypeStruct((n, Q_LORA), BF16),
                   jax.ShapeDtypeStruct((n, KV_LORA), BF16),
                   jax.ShapeDtypeStruct((n, IDX_HEADS * IDX_DIM), BF16),
                   jax.ShapeDtypeStruct((n, LANES), F32),
                   jax.ShapeDtypeStruct((n, d), BF16),
                   jax.ShapeDtypeStruct((n, d), BF16)],
        compiler_params=pltpu.CompilerParams(dimension_semantics=("parallel",),
                                             vmem_limit_bytes=VMEM_LIMIT),
        name="inproj",
    )(x2, g_mix, w_in_p, g_cq, g_ckv, g_ki, b_ki, g_sgu, b_sgu, ws_pair, bsp, w_br_b)


def _attn_body(s_eff, topk, cq_ref, ckv_ref, ki_ref, wi_ref, wuq_ref, wqi_ref, wuv_ref,
               ya_ref, q_scr, qi_scr, wcol_scr, sc_scr, bias_scr, acc_scr, s_buf, p_buf,
               kva_scr):
    i = pl.program_id(1)
    cq = cq_ref[0]
    q = _dot(cq, wuq_ref[...])
    qi = _dot(cq, wqi_ref[...]).astype(BF16)
    wi = wi_ref[0]
    head_of_lane = lax.broadcasted_iota(jnp.int32, qi.shape, 1) // IDX_DIM
    for h in range(N_HEADS):
        q_scr[h] = (q[:, h * KV_LORA:(h + 1) * KV_LORA] * ATTN_SCALE).astype(BF16)
    for h in range(IDX_HEADS):
        qi_scr[h] = jnp.where(head_of_lane == h, qi, jnp.zeros_like(qi))
        wcol_scr[h] = wi[:, h:h + 1]

    def index_dots(h):
        s_buf[h % 2] = _dot_nt(qi_scr[h], ki_ref[0, 0:s_eff, :])

    index_dots(0)
    for h in range(IDX_HEADS):
        if h + 1 < IDX_HEADS:
            index_dots(h + 1)
        term = wcol_scr[h] * jnp.maximum(s_buf[h % 2], 0.0)
        bias_scr[...] = term if h == 0 else bias_scr[...] + term

    rowq = lax.broadcasted_iota(jnp.int32, (Q_BLOCK, s_eff), 0)
    colk = lax.broadcasted_iota(jnp.int32, (Q_BLOCK, s_eff), 1)
    allowed = (colk // CHUNK) <= ((i * Q_BLOCK + rowq) // CHUNK)
    sc_scr[...] = jnp.where(allowed, bias_scr[...], -jnp.inf)

    def code_to_float(code):
        bits = jnp.where(code < 0, code ^ jnp.int32(0x7FFFFFFF), code)
        return lax.bitcast_convert_type(bits, F32)

    def count_ge(thr):
        return jnp.sum(jnp.where(sc_scr[...] >= thr, 1.0, 0.0), axis=1, keepdims=True)

    kf = float(topk)
    code = jnp.where(count_ge(jnp.zeros((Q_BLOCK, 1), F32)) >= kf,
                     jnp.int32(0), jnp.int32(INT_MIN))
    for bit in range(30, -1, -1):
        cand = code + jnp.int32(1 << bit)
        feasible = (count_ge(code_to_float(cand)) >= kf) | (cand <= jnp.int32(CODE_NEG_INF))
        code = jnp.where(feasible, cand, code)

    thr = code_to_float(code)
    thr_up = code_to_float(code + 1)
    sel = sc_scr[...] >= thr
    n_ge = jnp.sum(jnp.where(sel, 1.0, 0.0), axis=1, keepdims=True)
    bias_scr[...] = jnp.where(sel & allowed, 0.0, -jnp.inf)
    tie_rows = (n_ge > kf) & (code > jnp.int32(CODE_NEG_INF))
    any_tie = jnp.max(jnp.where(tie_rows, 1.0, 0.0))

    @pl.when(any_tie > 0.0)
    def _():
        sc = sc_scr[...]
        gt = sc >= thr_up
        eq = (sc >= thr) & jnp.logical_not(gt)
        need = kf - jnp.sum(jnp.where(gt, 1.0, 0.0), axis=1, keepdims=True)
        nbits = (s_eff - 1).bit_length()

        def tie_step(it, j):
            cand = j + jnp.left_shift(jnp.int32(1), jnp.int32(nbits - 1) - it)
            v = sc_scr[...]
            tie = (v >= thr) & jnp.logical_not(v >= thr_up) & (colk < cand)
            cnt = jnp.sum(jnp.where(tie, 1.0, 0.0), axis=1, keepdims=True)
            return jnp.where(cnt < need, cand, j)

        jmax = lax.fori_loop(0, nbits, tie_step, jnp.zeros((Q_BLOCK, 1), jnp.int32))
        keep = gt | (eq & (colk <= jmax))
        bias_scr[...] = jnp.where(keep & allowed, 0.0, -jnp.inf)

    acc_scr[...] = jnp.zeros_like(acc_scr)

    def logits(h):
        s_buf[h % 2] = _dot_nt(q_scr[h], ckv_ref[0, 0:s_eff, :]) + bias_scr[...]

    def softmax(h):
        s = s_buf[h % 2]
        p_buf[h % 2] = jnp.exp(s - jnp.max(s, axis=1, keepdims=True)).astype(BF16)

    def values(h):
        ol = _dot(p_buf[h % 2], kva_scr[0:s_eff, :])
        o = ol[:, :KV_LORA] / ol[:, KV_LORA:]
        acc_scr[...] += _dot(o.astype(BF16), wuv_ref[h])

    logits(0)
    for h in range(N_HEADS):
        if h + 1 < N_HEADS:
            logits(h + 1)
        softmax(h)
        values(h)
    ya_ref[0] = acc_scr[...].astype(BF16)


def _attn_kernel(n_buckets, topk, cq_ref, ckv_ref, ki_ref, wi_ref, wuq_ref, wqi_ref, wuv_ref,
                 ya_ref, q_scr, qi_scr, wcol_scr, sc_scr, bias_scr, acc_scr, s_buf, p_buf,
                 kva_scr):
    s = ckv_ref.shape[1]
    nq = s // Q_BLOCK
    per = nq // n_buckets
    i = pl.program_id(1)

    @pl.when(i == 0)
    def _():
        kva_scr[:, 0:KV_LORA] = ckv_ref[0]
        kva_scr[:, KV_LORA:] = jnp.ones((s, KV_LORA), BF16)

    for k in range(n_buckets):
        s_eff = (k + 1) * per * Q_BLOCK

        @pl.when((i >= k * per) & (i < (k + 1) * per))
        def _(s_eff=s_eff):
            _attn_body(s_eff, topk, cq_ref, ckv_ref, ki_ref, wi_ref, wuq_ref, wqi_ref,
                       wuv_ref, ya_ref, q_scr, qi_scr, wcol_scr,
                       sc_scr.at[:, 0:s_eff], bias_scr.at[:, 0:s_eff], acc_scr,
                       s_buf.at[:, :, 0:s_eff], p_buf.at[:, :, 0:s_eff], kva_scr)


def _attn(cq, ckv, ki, wi, w_uq, w_qi, w_uv_pad, n_buckets):
    b, s, _ = cq.shape
    nq = s // Q_BLOCK
    topk = min(TOPK_MAX, s // 4)
    aw = N_HEADS * HEAD_DIM

    def full(shape):
        return pl.BlockSpec(shape, lambda bi, i: (0,) * len(shape))

    return pl.pallas_call(
        functools.partial(_attn_kernel, n_buckets, topk),
        grid=(b, nq),
        in_specs=[pl.BlockSpec((1, Q_BLOCK, Q_LORA), lambda bi, i: (bi, i, 0)),
                  pl.BlockSpec((1, s, KV_LORA), lambda bi, i: (bi, 0, 0)),
                  pl.BlockSpec((1, s, IDX_HEADS * IDX_DIM), lambda bi, i: (bi, 0, 0)),
                  pl.BlockSpec((1, Q_BLOCK, LANES), lambda bi, i: (bi, i, 0)),
                  full(w_uq.shape), full(w_qi.shape), full(w_uv_pad.shape)],
        out_specs=pl.BlockSpec((1, Q_BLOCK, aw), lambda bi, i: (bi, i, 0)),
        out_shape=jax.ShapeDtypeStruct((b, s, aw), BF16),
        scratch_shapes=[pltpu.VMEM((N_HEADS, Q_BLOCK, KV_LORA), BF16),
                        pltpu.VMEM((IDX_HEADS, Q_BLOCK, IDX_HEADS * IDX_DIM), BF16),
                        pltpu.VMEM((IDX_HEADS, Q_BLOCK, 1), F32),
                        pltpu.VMEM((Q_BLOCK, s), F32),
                        pltpu.VMEM((Q_BLOCK, s), F32),
                        pltpu.VMEM((Q_BLOCK, aw), F32),
                        pltpu.VMEM((2, Q_BLOCK, s), F32),
                        pltpu.VMEM((2, Q_BLOCK, s), BF16),
                        pltpu.VMEM((s, 2 * KV_LORA), BF16)],
        compiler_params=pltpu.CompilerParams(dimension_semantics=("parallel", "arbitrary"),
                                             vmem_limit_bytes=VMEM_LIMIT),
        name="attn",
    )(cq, ckv, ki, wi, w_uq, w_qi, w_uv_pad)


def _merge_kernel(x_ref, ya_ref, gas_ref, pb_ref, wbra_ref, wo_ref, gffn_ref,
                  wr_hi_ref, wr_lo_ref, br_ref,
                  x1_ref, h2p_ref, idx_ref, gate_ref, rank_ref, cnt_ref, base_scr):
    step = pl.program_id(0)
    tm, d = x_ref.shape

    @pl.when(step == 0)
    def _():
        base_scr[...] = jnp.zeros_like(base_scr)

    a = _dot(ya_ref[...], wbra_ref[...])
    merged = gas_ref[...].astype(F32) * a + pb_ref[...].astype(F32)
    x1 = x_ref[...] + _dot(merged.astype(BF16), wo_ref[...])
    x1_ref[...] = x1
    h2 = _rms(x1, gffn_ref[...])

    h_hi = h2.astype(BF16)
    h2p_ref[...] = _pack_bf16_pairs(h2)
    h_lo = (h2 - h_hi.astype(F32)).astype(BF16)
    logits = (_dot(h_hi, wr_hi_ref[...]) + _dot(h_hi, wr_lo_ref[...])
              + _dot(h_lo, wr_hi_ref[...]) + br_ref[...])

    lane_e = lax.broadcasted_iota(jnp.int32, (tm, N_EXPERTS), 1).astype(F32)
    lane_o = lax.broadcasted_iota(jnp.int32, (tm, LANES), 1)
    work = logits
    vals, idxs = [], []
    onehot = jnp.zeros((tm, N_EXPERTS), F32)
    for _ in range(TOP_K):
        m = jnp.max(work, axis=1, keepdims=True)
        idx = jnp.min(jnp.where(work == m, lane_e, float(N_EXPERTS)), axis=1, keepdims=True)
        hit = lane_e == idx
        onehot = onehot + jnp.where(hit, 1.0, 0.0)
        work = jnp.where(hit, -jnp.inf, work)
        vals.append(m)
        idxs.append(idx)
    exps = [jnp.exp(v - vals[0]) for v in vals]
    denom = exps[0] + exps[1] + exps[2] + exps[3]

    r = lax.broadcasted_iota(jnp.int32, (tm, tm), 0)
    c = lax.broadcasted_iota(jnp.int32, (tm, tm), 1)
    tri = jnp.where(c < r, 1.0, 0.0).astype(BF16)
    rank_full = _dot(tri, onehot.astype(BF16)) + base_scr[...]

    idx_out = jnp.zeros((tm, LANES), jnp.int32)
    gate_out = jnp.zeros((tm, LANES), F32)
    rank_out = jnp.zeros((tm, LANES), jnp.int32)
    for k in range(TOP_K):
        rk = jnp.sum(jnp.where(lane_e == idxs[k], rank_full, 0.0), axis=1, keepdims=True)
        idx_out = jnp.where(lane_o == k, idxs[k].astype(jnp.int32), idx_out)
        gate_out = jnp.where(lane_o == k, exps[k] / denom, gate_out)
        rank_out = jnp.where(lane_o == k, rk.astype(jnp.int32), rank_out)
    idx_ref[...] = idx_out
    gate_ref[...] = gate_out
    rank_ref[...] = rank_out

    base_scr[...] = base_scr[...] + jnp.sum(onehot, axis=0, keepdims=True)
    cnt_ref[...] = base_scr[...].astype(jnp.int32)


def _merge(x2, ya, gas, pb, w_br_a, w_o, g_ffn, wr_hi, wr_lo, b_router, tm):
    n, d = x2.shape
    aw = ya.shape[1]

    def full(shape):
        return pl.BlockSpec(shape, lambda i: (0,) * len(shape))

    def rows(width):
        return pl.BlockSpec((tm, width), lambda i: (i, 0))

    return pl.pallas_call(
        _merge_kernel,
        grid=(n // tm,),
        in_specs=[rows(d), rows(aw), rows(d), rows(d), full(w_br_a.shape), full(w_o.shape),
                  full((1, d)), full(wr_hi.shape), full(wr_lo.shape), full((1, N_EXPERTS))],
        out_specs=[rows(d), rows(d // 2), rows(LANES), rows(LANES), rows(LANES),
                   full((1, N_EXPERTS))],
        out_shape=[jax.ShapeDtypeStruct((n, d), F32),
                   jax.ShapeDtypeStruct((n, d // 2), jnp.int32),
                   jax.ShapeDtypeStruct((n, LANES), jnp.int32),
                   jax.ShapeDtypeStruct((n, LANES), F32),
                   jax.ShapeDtypeStruct((n, LANES), jnp.int32),
                   jax.ShapeDtypeStruct((1, N_EXPERTS), jnp.int32)],
        scratch_shapes=[pltpu.VMEM((1, N_EXPERTS), F32)],
        compiler_params=pltpu.CompilerParams(dimension_semantics=("arbitrary",),
                                             vmem_limit_bytes=VMEM_LIMIT),
        name="merge_route",
    )(x2, ya, gas, pb, w_br_a, w_o, g_ffn, wr_hi, wr_lo, b_router)


def _sc_mesh():
    info = plsc.get_sparse_core_info()
    mesh = plsc.VectorSubcoreMesh(core_axis_name="c", subcore_axis_name="s")
    return mesh, info.num_cores, info.num_cores * info.num_subcores


def _sc_dispatch(src, dest, pad_idx, n_out_rows):
    n, w = src.shape
    mesh, n_cores, n_workers = _sc_mesh()
    per_w = n // n_workers
    n_chunks = per_w // SC_ROWS
    n_pad_chunks = pad_idx.size // (n_workers * SC_ROWS)

    @functools.partial(
        pl.kernel, mesh=mesh,
        out_type=jax.ShapeDtypeStruct((n_out_rows, w), jnp.int32),
        scratch_types=[pltpu.VMEM((n_chunks, TOP_K, SC_ROWS), jnp.int32),
                       pltpu.VMEM((n_pad_chunks, SC_ROWS), jnp.int32),
                       pltpu.VMEM((SC_ROWS, w), jnp.int32),
                       pltpu.VMEM((SC_ROWS, w), jnp.int32)],
        name="sc_dispatch",
    )
    def run(src_hbm, dest_hbm, pad_hbm, zeros_hbm, out_hbm, idx_v, pad_v, rows_v, zero_v):
        wid = lax.axis_index("s") * n_cores + lax.axis_index("c")
        pltpu.sync_copy(dest_hbm.at[wid], idx_v)
        pltpu.sync_copy(pad_hbm.at[wid], pad_v)
        pltpu.sync_copy(zeros_hbm, zero_v)

        @pl.loop(0, n_pad_chunks)
        def _(c):
            pltpu.sync_copy(zero_v, out_hbm.at[pad_v.at[c]])

        @pl.loop(0, n_chunks)
        def _(c):
            pltpu.sync_copy(src_hbm.at[pl.ds(wid * per_w + c * SC_ROWS, SC_ROWS)], rows_v)
            for k in range(TOP_K):
                pltpu.sync_copy(rows_v, out_hbm.at[idx_v.at[c, k]])

    dest_w = dest.reshape(n_workers, n_chunks, SC_ROWS, TOP_K).transpose(0, 1, 3, 2)
    return run(src, dest_w, pad_idx.reshape(n_workers, n_pad_chunks, SC_ROWS),
               jnp.zeros((SC_ROWS, w), jnp.int32))


def _sc_gather(table, idx):
    n_rows = idx.shape[0]
    w = table.shape[1]
    mesh, n_cores, n_workers = _sc_mesh()
    per_w = n_rows // n_workers
    n_chunks = per_w // SC_ROWS
    assert n_chunks % 2 == 0 and n_chunks * SC_ROWS * n_workers == n_rows

    @functools.partial(
        pl.kernel, mesh=mesh,
        out_type=jax.ShapeDtypeStruct((n_rows, w), table.dtype),
        scratch_types=[pltpu.VMEM((n_chunks, SC_ROWS), jnp.int32),
                       pltpu.VMEM((SC_ROWS, w), table.dtype),
                       pltpu.VMEM((SC_ROWS, w), table.dtype),
                       pltpu.SemaphoreType.DMA, pltpu.SemaphoreType.DMA],
        name="sc_gather",
    )
    def run(table_hbm, idx_hbm, out_hbm, idx_v, buf0, buf1, sem0, sem1):
        wid = lax.axis_index("s") * n_cores + lax.axis_index("c")
        base = wid * per_w
        pltpu.sync_copy(idx_hbm.at[wid], idx_v)

        def gather(c, buf, sem):
            return pltpu.make_async_copy(table_hbm.at[idx_v.at[c]], buf, sem)

        gather(0, buf0, sem0).start()

        @pl.loop(0, n_chunks, step=2)
        def _(c):
            gather(c + 1, buf1, sem1).start()
            gather(c, buf0, sem0).wait()
            pltpu.sync_copy(buf0, out_hbm.at[pl.ds(base + c * SC_ROWS, SC_ROWS)])

            @pl.when(c + 2 < n_chunks)
            def _():
                gather(c + 2, buf0, sem0).start()

            gather(c + 1, buf1, sem1).wait()
            pltpu.sync_copy(buf1, out_hbm.at[pl.ds(base + (c + 1) * SC_ROWS, SC_ROWS)])

    return run(table, idx.reshape(n_workers, n_chunks, SC_ROWS))


def _expert_kernel(be_ref, nu_ref, xs_ref, wgu_ref, bgu_ref, wd_ref, bd_ref, y_ref,
                   wgu_bf, wd_bf):
    i = pl.program_id(0)
    f = wd_ref.shape[1]
    half = xs_ref.shape[1]
    live = i < nu_ref[0]

    @pl.when(live & ((i == 0) | (be_ref[i] != be_ref[jnp.maximum(i - 1, 0)])))
    def _():
        wgu_bf[...] = wgu_ref[0].astype(BF16)
        wd_bf[...] = wd_ref[0].astype(BF16)

    @pl.when(live)
    def _():
        x_lo, x_hi = _unpack_bf16_pairs(xs_ref[...])
        gu = (_dot(x_lo.astype(BF16), wgu_bf[0:half, :])
              + _dot(x_hi.astype(BF16), wgu_bf[half:2 * half, :])
              + bgu_ref[0])
        gate = jnp.minimum(gu[:, :f], SWIGLU_LIMIT)
        up = jnp.clip(gu[:, f:], -SWIGLU_LIMIT, SWIGLU_LIMIT)
        act = (up + 1.0) * (gate * jax.nn.sigmoid(SWIGLU_ALPHA * gate))
        y = _dot(act.astype(BF16), wd_bf[...]) + bd_ref[0]
        y_ref[...] = _pack_bf16_pairs(y)


def _experts(block_e, n_used, xs, n_blocks, w_gu, b_gu, w_down, b_down):
    half = xs.shape[1]
    d = 2 * half
    f2 = w_gu.shape[2]
    f = w_down.shape[1]

    def blk(i, be, nu):
        return (jnp.minimum(i, nu[0] - 1), 0)

    def expert(i, be, nu):
        return (be[jnp.minimum(i, nu[0] - 1)], 0, 0)

    grid_spec = pltpu.PrefetchScalarGridSpec(
        num_scalar_prefetch=2,
        grid=(n_blocks,),
        in_specs=[pl.BlockSpec((EXPERT_BLOCK, half), blk),
                  pl.BlockSpec((1, d, f2), expert),
                  pl.BlockSpec((1, 1, f2), expert),
                  pl.BlockSpec((1, f, d), expert),
                  pl.BlockSpec((1, 1, d), expert)],
        out_specs=pl.BlockSpec((EXPERT_BLOCK, half), blk),
        scratch_shapes=[pltpu.VMEM((d, f2), BF16), pltpu.VMEM((f, d), BF16)],
    )
    return pl.pallas_call(
        _expert_kernel,
        grid_spec=grid_spec,
        out_shape=jax.ShapeDtypeStruct((n_blocks * EXPERT_BLOCK, half), jnp.int32),
        compiler_params=pltpu.CompilerParams(dimension_semantics=("arbitrary",),
                                             vmem_limit_bytes=VMEM_LIMIT),
        name="experts",
    )(block_e, n_used, xs, w_gu, b_gu, w_down, b_down)


def _combine_kernel(yg_ref, x1_ref, gate_ref, gfin_ref, out_ref):
    gates = gate_ref[...]
    half = yg_ref.shape[2]
    acc_lo = x1_ref[:, :half]
    acc_hi = x1_ref[:, half:]
    for k in range(TOP_K):
        y_lo, y_hi = _unpack_bf16_pairs(yg_ref[k])
        acc_lo = acc_lo + gates[:, k:k + 1] * y_lo
        acc_hi = acc_hi + gates[:, k:k + 1] * y_hi
    out_ref[...] = _rms(jnp.concatenate([acc_lo, acc_hi], axis=1), gfin_ref[...])


def _combine(yg, x1, gates, g_final, tm):
    n, d = x1.shape
    return pl.pallas_call(
        _combine_kernel,
        grid=(n // tm,),
        in_specs=[pl.BlockSpec((TOP_K, tm, d // 2), lambda i: (0, i, 0)),
                  pl.BlockSpec((tm, d), lambda i: (i, 0)),
                  pl.BlockSpec((tm, LANES), lambda i: (i, 0)),
                  pl.BlockSpec((1, d), lambda i: (0, 0))],
        out_specs=pl.BlockSpec((tm, d), lambda i: (i, 0)),
        out_shape=jax.ShapeDtypeStruct((n, d), F32),
        compiler_params=pltpu.CompilerParams(dimension_semantics=("parallel",),
                                             vmem_limit_bytes=VMEM_LIMIT),
        name="combine",
    )(yg, x1, gates, g_final)


def _pack_in_proj(w_in, d):
    offs = [0]
    for width in (Q_LORA, KV_LORA, IDX_DIM, IDX_HEADS, SGU_WIDTH, SGU_WIDTH, d, d):
        offs.append(offs[-1] + width)
    wq, wkv, wki, wwi, wu, wv, wga, wgb = [w_in[:, offs[j]:offs[j + 1]] for j in range(8)]
    wki_rep = jnp.tile(wki, (1, IDX_HEADS))
    wwi_pad = jnp.pad(wwi, ((0, 0), (0, LANES - IDX_HEADS)))
    return jnp.concatenate([wq, wkv, wki_rep, wwi_pad, wu, wv, wga, wgb], axis=1).astype(BF16)


def _layer(x, g_mix, w_in, g_cq, g_ckv, w_uq, w_uv, w_q_idx, g_kidx, b_kidx, g_sgu, b_sgu,
           w_spatial, b_spatial, w_br_a, w_br_b, w_o, g_ffn, w_router, b_router, w_gu, b_gu,
           w_down, b_down, g_final):
    b, s, d = x.shape
    n = b * s
    x2 = x.reshape(n, d)
    row = lambda v: v.reshape(1, -1).astype(F32)

    w_in_p = _pack_in_proj(w_in, d)
    ws_pair = w_spatial.reshape(SGU_GROUPS // 2, 2, SGU_CHUNK, SGU_CHUNK).transpose(
        0, 2, 1, 3).reshape(SGU_GROUPS // 2, SGU_CHUNK, 2 * SGU_CHUNK)
    bsp = jnp.repeat(b_spatial.T, SGU_GROUP_DIM, axis=1)
    cq, ckv, ki, wi, gas, pb = _inproj(
        x2, row(g_mix), w_in_p, row(g_cq), row(g_ckv), row(jnp.tile(g_kidx, IDX_HEADS)),
        row(jnp.tile(b_kidx, IDX_HEADS)), row(g_sgu), row(b_sgu), ws_pair, bsp,
        w_br_b.astype(BF16), tm=512)

    head_eye = jnp.eye(N_HEADS, dtype=F32)
    w_uv_pad = (w_uv[:, :, None, :] * head_eye[:, None, :, None]).reshape(
        N_HEADS, KV_LORA, N_HEADS * HEAD_DIM)
    n_buckets = next(nbk for nbk in (4, 2, 1) if (s // Q_BLOCK) % nbk == 0)
    ya = _attn(cq.reshape(b, s, -1), ckv.reshape(b, s, -1), ki.reshape(b, s, -1),
               wi.reshape(b, s, -1), w_uq.astype(BF16), w_q_idx.astype(BF16),
               w_uv_pad.astype(BF16), n_buckets)

    wr_hi = w_router.astype(BF16)
    wr_lo = (w_router - wr_hi.astype(F32)).astype(BF16)
    x1, h2p, idx_p, gate_p, rank_p, counts = _merge(
        x2, ya.reshape(n, -1), gas, pb, w_br_a.astype(BF16), w_o.astype(BF16), row(g_ffn),
        wr_hi, wr_lo, row(b_router), tm=512)

    counts = counts[0]
    top_idx = idx_p[:, :TOP_K]
    padded = (counts + EXPERT_BLOCK - 1) // EXPERT_BLOCK * EXPERT_BLOCK
    padded_end = jnp.cumsum(padded)
    padded_start = padded_end - padded
    dest = padded_start[top_idx] + rank_p[:, :TOP_K]
    nk = n * TOP_K
    nb = -(-(nk + N_EXPERTS * EXPERT_BLOCK) // EXPERT_BLOCK)
    block_row0 = jnp.arange(nb, dtype=jnp.int32) * EXPERT_BLOCK
    block_e = jnp.minimum(
        jnp.sum((padded_end[None, :] <= block_row0[:, None]).astype(jnp.int32), axis=1),
        N_EXPERTS - 1).astype(jnp.int32)
    n_used = (padded_end[-1:] // EXPERT_BLOCK).astype(jnp.int32)
    j = jnp.arange(EXPERT_BLOCK, dtype=jnp.int32)[None, :]
    pad_idx = jnp.where(j < (padded - counts)[:, None], (padded_start + counts)[:, None] + j,
                        nb * EXPERT_BLOCK + j).astype(jnp.int32)

    xs = _sc_dispatch(h2p, dest, pad_idx, (nb + 1) * EXPERT_BLOCK)
    ybuf = _experts(block_e, n_used, xs, nb, w_gu, b_gu.reshape(N_EXPERTS, 1, -1), w_down,
                    b_down.reshape(N_EXPERTS, 1, -1))
    yg = _sc_gather(ybuf, dest.T.reshape(-1))
    out = _combine(yg.reshape(TOP_K, n, d // 2), x1, gate_p, row(g_final), tm=256)
    return out.reshape(b, s, d)


def kernel(x, g_mix, w_in, g_cq, g_ckv, w_uq, w_uv, w_q_idx, g_kidx, b_kidx, g_sgu, b_sgu,
           w_spatial, b_spatial, w_br_a, w_br_b, w_o, g_ffn, w_router, b_router, w_gu, b_gu,
           w_down, b_down, g_final):
    assert g_mix.shape[0] == 1, "single-layer block"
    return _layer(x, g_mix[0], w_in[0], g_cq[0], g_ckv[0], w_uq[0], w_uv[0], w_q_idx[0],
                  g_kidx[0], b_kidx[0], g_sgu[0], b_sgu[0], w_spatial[0], b_spatial[0],
                  w_br_a[0], w_br_b[0], w_o[0], g_ffn[0], w_router[0], b_router[0], w_gu[0],
                  b_gu[0], w_down[0], b_down[0], g_final)
```

```python
import functools

import jax
import jax.numpy as jnp
from jax import lax
from jax.experimental import pallas as pl
from jax.experimental.pallas import tpu as pltpu
from jax.experimental.pallas import tpu_sc as plsc

EPS = 1e-6
CHUNK = 64
N_HEADS = 8
HEAD_DIM = 64
Q_LORA = 256
KV_LORA = 128
IDX_HEADS = 8
IDX_DIM = 32
TOPK_MAX = 256
Q_BLOCK = 256
ATTN_SCALE = KV_LORA ** -0.5
IDX_SCALE = (IDX_HEADS * IDX_DIM) ** -0.5
SGU_CHUNK = 128
SGU_GROUPS = 8
SGU_WIDTH = 512
SGU_GROUP_DIM = SGU_WIDTH // SGU_GROUPS
N_EXPERTS = 32
TOP_K = 4
SWIGLU_LIMIT = 7.0
SWIGLU_ALPHA = 1.702
EXPERT_BLOCK = 512

RADIX4_MAX_KEYS = 1024
SC_ROWS = 32
LANES = 128
VMEM_LIMIT = 56 * 1024 * 1024
INT_MIN = -(2 ** 31)
CODE_NEG_INF = INT_MIN + 0x7FFFFF

BF16 = jnp.bfloat16
F32 = jnp.float32


def _dot(a, b):
    return jnp.dot(a, b, preferred_element_type=F32)


def _dot_nt(a, b):
    return lax.dot_general(a, b, (((1,), (1,)), ((), ())), preferred_element_type=F32)


def _rms(x, g):
    return x * lax.rsqrt(jnp.mean(x * x, axis=-1, keepdims=True) + EPS) * g


def _pack_bf16_pairs(x):
    half = x.shape[1] // 2
    bits = lax.bitcast_convert_type(x.astype(BF16).astype(F32), jnp.int32)
    return lax.shift_right_logical(bits[:, :half], 16) | (bits[:, half:] & jnp.int32(-65536))


def _unpack_bf16_pairs(words):
    lo = lax.bitcast_convert_type(lax.shift_left(words, 16), F32)
    hi = lax.bitcast_convert_type(words & jnp.int32(-65536), F32)
    return lo, hi


def _layer_norm(x, g, b):
    mu = jnp.mean(x, axis=-1, keepdims=True)
    xc = x - mu
    var = jnp.mean(xc * xc, axis=-1, keepdims=True)
    return xc * lax.rsqrt(var + EPS) * g + b


C_Q = 0
C_KV = C_Q + Q_LORA
C_KI = C_KV + KV_LORA
C_WI = C_KI + IDX_HEADS * IDX_DIM
C_U = C_WI + LANES
C_V = C_U + SGU_WIDTH
D_IN_P_BASE = C_V + SGU_WIDTH


def _inproj_kernel(x_ref, gmix_ref, w_ref, gcq_ref, gckv_ref, gki_ref, bki_ref,
                   gsgu_ref, bsgu_ref, wsp_ref, bsp_ref, wbrb_ref,
                   cq_ref, ckv_ref, ki_ref, wi_ref, gas_ref, pb_ref):
    tm, d = x_ref.shape
    c_ga = D_IN_P_BASE
    c_gb = c_ga + d
    h = _rms(x_ref[...], gmix_ref[...]).astype(BF16)

    def proj(lo, width):
        return _dot(h, w_ref[:, lo:lo + width])

    cq_ref[...] = _rms(proj(C_Q, Q_LORA), gcq_ref[...]).astype(BF16)
    ckv_ref[...] = _rms(proj(C_KV, KV_LORA), gckv_ref[...]).astype(BF16)
    ki_ref[...] = _layer_norm(proj(C_KI, IDX_HEADS * IDX_DIM), gki_ref[...],
                              bki_ref[...]).astype(BF16)
    wi_ref[...] = proj(C_WI, LANES) * IDX_SCALE
    gas_ref[...] = jax.nn.sigmoid(proj(c_ga, d)).astype(BF16)

    u = jax.nn.gelu(proj(C_U, SGU_WIDTH))
    v = _layer_norm(jax.nn.gelu(proj(C_V, SGU_WIDTH)), gsgu_ref[...], bsgu_ref[...])

    row = lax.broadcasted_iota(jnp.int32, (SGU_CHUNK, 2 * SGU_CHUNK), 0)
    col = lax.broadcasted_iota(jnp.int32, (SGU_CHUNK, 2 * SGU_CHUNK), 1) % SGU_CHUNK
    causal = (row // CHUNK) >= (col // CHUNK)
    lane = lax.broadcasted_iota(jnp.int32, (SGU_CHUNK, LANES), 1)
    left = lane < SGU_GROUP_DIM
    n_tiles = SGU_WIDTH // LANES
    ws = [jnp.where(causal, wsp_ref[j], 0.0).astype(BF16) for j in range(n_tiles)]
    yb_chunks = []
    for c in range(tm // SGU_CHUNK):
        tiles = []
        for j in range(n_tiles):
            blk = v[c * SGU_CHUNK:(c + 1) * SGU_CHUNK, j * LANES:(j + 1) * LANES]
            stacked = jnp.concatenate(
                [jnp.where(left, blk, 0.0), jnp.where(left, 0.0, blk)], axis=0).astype(BF16)
            tiles.append(_dot(ws[j], stacked))
        s = jnp.concatenate(tiles, axis=1) + bsp_ref[...]
        yb_chunks.append(u[c * SGU_CHUNK:(c + 1) * SGU_CHUNK, :] * s)
    yb = jnp.concatenate(yb_chunks, axis=0).astype(BF16)
    pb_ref[...] = (jax.nn.sigmoid(proj(c_gb, d)) * _dot(yb, wbrb_ref[...])).astype(BF16)


def _inproj(x2, g_mix, w_in_p, g_cq, g_ckv, g_ki, b_ki, g_sgu, b_sgu, ws_pair, bsp, w_br_b,
            tm):
    n, d = x2.shape
    d_in_p = w_in_p.shape[1]

    def full(shape):
        return pl.BlockSpec(shape, lambda i: (0,) * len(shape))

    def rows(width):
        return pl.BlockSpec((tm, width), lambda i: (i, 0))

    return pl.pallas_call(
        _inproj_kernel,
        grid=(n // tm,),
        in_specs=[rows(d), full((1, d)), full((d, d_in_p)), full((1, Q_LORA)),
                  full((1, KV_LORA)), full((1, IDX_HEADS * IDX_DIM)),
                  full((1, IDX_HEADS * IDX_DIM)), full((1, SGU_WIDTH)), full((1, SGU_WIDTH)),
                  full(ws_pair.shape), full(bsp.shape), full(w_br_b.shape)],
        out_specs=[rows(Q_LORA), rows(KV_LORA), rows(IDX_HEADS * IDX_DIM), rows(LANES),
                   rows(d), rows(d)],
        out_shape=[jax.ShapeDtypeStruct((n, Q_LORA), BF16),
                   jax.ShapeDtypeStruct((n, KV_LORA), BF16),
                   jax.ShapeDtypeStruct((n, IDX_HEADS * IDX_DIM), BF16),
                   jax.ShapeDtypeStruct((n, LANES), F32),
                   jax.ShapeDtypeStruct((n, d), BF16),
                   jax.ShapeDtypeStruct((n, d), BF16)],
        compiler_params=pltpu.CompilerParams(dimension_semantics=("parallel",),
                                             vmem_limit_bytes=VMEM_LIMIT),
        name="inproj",
    )(x2, g_mix, w_in_p, g_cq, g_ckv, g_ki, b_ki, g_sgu, b_sgu, ws_pair, bsp, w_br_b)


def _attn_body(s_eff, topk, q_block0, cq_ref, ckv_ref, ki_ref, wi_ref, wuq_ref, wqi_ref,
               wuv_ref, ya_ref, q_scr, qi_scr, wcol_scr, sc_scr, bias_scr, acc_scr, s_buf,
               p_buf, kva_scr, sc16_scr):
    i = pl.program_id(1) + q_block0
    cq = cq_ref[0]
    q = _dot(cq, wuq_ref[...])
    qi = _dot(cq, wqi_ref[...]).astype(BF16)
    wi = wi_ref[0]
    head_of_lane = lax.broadcasted_iota(jnp.int32, qi.shape, 1) // IDX_DIM
    for h in range(N_HEADS):
        q_scr[h] = (q[:, h * KV_LORA:(h + 1) * KV_LORA] * ATTN_SCALE).astype(BF16)
    for h in range(IDX_HEADS):
        qi_scr[h] = jnp.where(head_of_lane == h, qi, jnp.zeros_like(qi))
        wcol_scr[h] = wi[:, h:h + 1]

    def index_dots(h):
        s_buf[h % 2] = _dot_nt(qi_scr[h], ki_ref[0, 0:s_eff, :])

    index_dots(0)
    for h in range(IDX_HEADS):
        if h + 1 < IDX_HEADS:
            index_dots(h + 1)
        term = wcol_scr[h] * jnp.maximum(s_buf[h % 2], 0.0)
        bias_scr[...] = term if h == 0 else bias_scr[...] + term

    rowq = lax.broadcasted_iota(jnp.int32, (Q_BLOCK, s_eff), 0)
    colk = lax.broadcasted_iota(jnp.int32, (Q_BLOCK, s_eff), 1)
    allowed = (colk // CHUNK) <= ((i * Q_BLOCK + rowq) // CHUNK)
    sc_scr[...] = jnp.where(allowed, bias_scr[...], -jnp.inf)

    def code_to_bits(code):
        return jnp.where(code < 0, code ^ jnp.int32(0x7FFFFFFF), code)

    def code_to_float(code):
        return lax.bitcast_convert_type(code_to_bits(code), F32)

    upper_half = jnp.int32(-65536)
    sc_bits = lax.bitcast_convert_type(sc_scr[...], jnp.int32)
    sc16_scr[...] = lax.bitcast_convert_type(sc_bits & upper_half, F32).astype(BF16)

    def count_ge_upper(cand):
        thr16 = lax.bitcast_convert_type(code_to_bits(cand) & upper_half, F32).astype(BF16)
        part = None
        for t in range(s_eff // LANES):
            tile = sc16_scr[:, t * LANES:(t + 1) * LANES]
            ind = jnp.where(tile >= thr16, jnp.ones((), BF16), jnp.zeros((), BF16))
            part = ind if part is None else part + ind
        return jnp.sum(part.astype(F32), axis=1, keepdims=True)

    def count_ge(cand):
        return jnp.sum(jnp.where(sc_scr[...] >= code_to_float(cand), 1.0, 0.0), axis=1,
                       keepdims=True)

    kf = float(topk)
    code = jnp.where(count_ge_upper(jnp.zeros((Q_BLOCK, 1), jnp.int32)) >= kf,
                     jnp.int32(0), jnp.int32(INT_MIN))
    def feasible(cand, bit):
        cnt = count_ge_upper(cand) if bit >= 16 else count_ge(cand)
        return (cnt >= kf) | (cand <= jnp.int32(CODE_NEG_INF))

    bits_per_step = 2 if s_eff <= RADIX4_MAX_KEYS else 1
    bit = 30
    while bit >= 0:
        if bits_per_step == 2 and bit >= 1:
            lo = bit - 1
            digit = jnp.zeros((Q_BLOCK, 1), jnp.int32)
            for mult in (1, 2, 3):
                ok = feasible(code + jnp.int32(mult << lo), lo)
                digit = digit + jnp.where(ok, jnp.int32(1), jnp.int32(0))
            code = code + lax.shift_left(digit, jnp.int32(lo))
            bit -= 2
        else:
            cand = code + jnp.int32(1 << bit)
            code = jnp.where(feasible(cand, bit), cand, code)
            bit -= 1

    thr = code_to_float(code)
    thr_up = code_to_float(code + 1)
    sel = sc_scr[...] >= thr
    n_ge = jnp.sum(jnp.where(sel, 1.0, 0.0), axis=1, keepdims=True)
    bias_scr[...] = jnp.where(sel & allowed, 0.0, -jnp.inf)
    tie_rows = (n_ge > kf) & (code > jnp.int32(CODE_NEG_INF))
    any_tie = jnp.max(jnp.where(tie_rows, 1.0, 0.0))

    @pl.when(any_tie > 0.0)
    def _():
        sc = sc_scr[...]
        gt = sc >= thr_up
        eq = (sc >= thr) & jnp.logical_not(gt)
        need = kf - jnp.sum(jnp.where(gt, 1.0, 0.0), axis=1, keepdims=True)
        nbits = (s_eff - 1).bit_length()

        def tie_step(it, j):
            cand = j + jnp.left_shift(jnp.int32(1), jnp.int32(nbits - 1) - it)
            v = sc_scr[...]
            tie = (v >= thr) & jnp.logical_not(v >= thr_up) & (colk < cand)
            cnt = jnp.sum(jnp.where(tie, 1.0, 0.0), axis=1, keepdims=True)
            return jnp.where(cnt < need, cand, j)

        jmax = lax.fori_loop(0, nbits, tie_step, jnp.zeros((Q_BLOCK, 1), jnp.int32))
        keep = gt | (eq & (colk <= jmax))
        bias_scr[...] = jnp.where(keep & allowed, 0.0, -jnp.inf)

    acc_scr[...] = jnp.zeros_like(acc_scr)

    def logits(h):
        s_buf[h % 2] = _dot_nt(q_scr[h], ckv_ref[0, 0:s_eff, :]) + bias_scr[...]

    def softmax(h):
        s = s_buf[h % 2]
        p_buf[h % 2] = jnp.exp(s - jnp.max(s, axis=1, keepdims=True)).astype(BF16)

    def values(h):
        ol = _dot(p_buf[h % 2], kva_scr[0:s_eff, :])
        o = ol[:, :KV_LORA] / ol[:, KV_LORA:]
        acc_scr[...] += _dot(o.astype(BF16), wuv_ref[h])

    logits(0)
    for h in range(N_HEADS):
        if h + 1 < N_HEADS:
            logits(h + 1)
        softmax(h)
        values(h)
    ya_ref[0] = acc_scr[...].astype(BF16)


def _attn_kernel(s_eff, topk, q_block0, cq_ref, ckv_ref, ki_ref, wi_ref, wuq_ref, wqi_ref,
                 wuv_ref, ya_ref, q_scr, qi_scr, wcol_scr, sc_scr, bias_scr, acc_scr, s_buf,
                 p_buf, kva_scr, sc16_scr):
    @pl.when(pl.program_id(1) == 0)
    def _():
        kva_scr[:, 0:KV_LORA] = ckv_ref[0]
        kva_scr[:, KV_LORA:] = jnp.ones((s_eff, KV_LORA), BF16)

    _attn_body(s_eff, topk, q_block0, cq_ref, ckv_ref, ki_ref, wi_ref, wuq_ref, wqi_ref,
               wuv_ref, ya_ref, q_scr, qi_scr, wcol_scr, sc_scr, bias_scr, acc_scr, s_buf,
               p_buf, kva_scr, sc16_scr)


def _attn(cq, ckv, ki, wi, w_uq, w_qi, w_uv_pad, n_buckets):
    b, s, _ = cq.shape
    nq = s // Q_BLOCK
    per = nq // n_buckets
    topk = min(TOPK_MAX, s // 4)
    aw = N_HEADS * HEAD_DIM

    def full(shape):
        return pl.BlockSpec(shape, lambda bi, i: (0,) * len(shape))

    parts = []
    for k in range(n_buckets):
        s_eff = (k + 1) * per * Q_BLOCK
        q0 = k * per

        def q_rows(width, q0=q0):
            return pl.BlockSpec((1, Q_BLOCK, width), lambda bi, i: (bi, i + q0, 0))

        def keys(width, s_eff=s_eff):
            return pl.BlockSpec((1, s_eff, width), lambda bi, i: (bi, 0, 0))

        parts.append(pl.pallas_call(
            functools.partial(_attn_kernel, s_eff, topk, q0),
            grid=(b, per),
            in_specs=[q_rows(Q_LORA), keys(KV_LORA), keys(IDX_HEADS * IDX_DIM), q_rows(LANES),
                      full(w_uq.shape), full(w_qi.shape), full(w_uv_pad.shape)],
            out_specs=pl.BlockSpec((1, Q_BLOCK, aw), lambda bi, i: (bi, i, 0)),
            out_shape=jax.ShapeDtypeStruct((b, per * Q_BLOCK, aw), BF16),
            scratch_shapes=[pltpu.VMEM((N_HEADS, Q_BLOCK, KV_LORA), BF16),
                            pltpu.VMEM((IDX_HEADS, Q_BLOCK, IDX_HEADS * IDX_DIM), BF16),
                            pltpu.VMEM((IDX_HEADS, Q_BLOCK, 1), F32),
                            pltpu.VMEM((Q_BLOCK, s_eff), F32),
                            pltpu.VMEM((Q_BLOCK, s_eff), F32),
                            pltpu.VMEM((Q_BLOCK, aw), F32),
                            pltpu.VMEM((2, Q_BLOCK, s_eff), F32),
                            pltpu.VMEM((2, Q_BLOCK, s_eff), BF16),
                            pltpu.VMEM((s_eff, 2 * KV_LORA), BF16),
                            pltpu.VMEM((Q_BLOCK, s_eff), BF16)],
            compiler_params=pltpu.CompilerParams(
                dimension_semantics=("parallel", "arbitrary"), vmem_limit_bytes=VMEM_LIMIT),
            name=f"attn_keys{s_eff}",
        )(cq, ckv, ki, wi, w_uq, w_qi, w_uv_pad))
    return jnp.concatenate(parts, axis=1)


def _merge_kernel(x_ref, ya_ref, gas_ref, pb_ref, wbra_ref, wo_ref, gffn_ref,
                  wr_hi_ref, wr_lo_ref, br_ref,
                  x1_ref, h2p_ref, idx_ref, gate_ref, rank_ref, cnt_ref, base_scr):
    step = pl.program_id(0)
    tm, d = x_ref.shape

    @pl.when(step == 0)
    def _():
        base_scr[...] = jnp.zeros_like(base_scr)

    a = _dot(ya_ref[...], wbra_ref[...])
    merged = gas_ref[...].astype(F32) * a + pb_ref[...].astype(F32)
    x1 = x_ref[...] + _dot(merged.astype(BF16), wo_ref[...])
    x1_ref[...] = x1
    h2 = _rms(x1, gffn_ref[...])

    h_hi = h2.astype(BF16)
    h2p_ref[...] = _pack_bf16_pairs(h2)
    h_lo = (h2 - h_hi.astype(F32)).astype(BF16)
    logits = (_dot(h_hi, wr_hi_ref[...]) + _dot(h_hi, wr_lo_ref[...])
              + _dot(h_lo, wr_hi_ref[...]) + br_ref[...])

    lane_e = lax.broadcasted_iota(jnp.int32, (tm, N_EXPERTS), 1).astype(F32)
    lane_o = lax.broadcasted_iota(jnp.int32, (tm, LANES), 1)
    work = logits
    vals, idxs = [], []
    onehot = jnp.zeros((tm, N_EXPERTS), F32)
    for _ in range(TOP_K):
        m = jnp.max(work, axis=1, keepdims=True)
        idx = jnp.min(jnp.where(work == m, lane_e, float(N_EXPERTS)), axis=1, keepdims=True)
        hit = lane_e == idx
        onehot = onehot + jnp.where(hit, 1.0, 0.0)
        work = jnp.where(hit, -jnp.inf, work)
        vals.append(m)
        idxs.append(idx)
    exps = [jnp.exp(v - vals[0]) for v in vals]
    denom = exps[0] + exps[1] + exps[2] + exps[3]

    r = lax.broadcasted_iota(jnp.int32, (tm, tm), 0)
    c = lax.broadcasted_iota(jnp.int32, (tm, tm), 1)
    tri = jnp.where(c < r, 1.0, 0.0).astype(BF16)
    rank_full = _dot(tri, onehot.astype(BF16)) + base_scr[...]

    idx_out = jnp.zeros((tm, LANES), jnp.int32)
    gate_out = jnp.zeros((tm, LANES), F32)
    rank_out = jnp.zeros((tm, LANES), jnp.int32)
    for k in range(TOP_K):
        rk = jnp.sum(jnp.where(lane_e == idxs[k], rank_full, 0.0), axis=1, keepdims=True)
        idx_out = jnp.where(lane_o == k, idxs[k].astype(jnp.int32), idx_out)
        gate_out = jnp.where(lane_o == k, exps[k] / denom, gate_out)
        rank_out = jnp.where(lane_o == k, rk.astype(jnp.int32), rank_out)
    idx_ref[...] = idx_out
    gate_ref[...] = gate_out
    rank_ref[...] = rank_out

    base_scr[...] = base_scr[...] + jnp.sum(onehot, axis=0, keepdims=True)
    cnt_ref[...] = base_scr[...].astype(jnp.int32)


def _merge(x2, ya, gas, pb, w_br_a, w_o, g_ffn, wr_hi, wr_lo, b_router, tm):
    n, d = x2.shape
    aw = ya.shape[1]

    def full(shape):
        return pl.BlockSpec(shape, lambda i: (0,) * len(shape))

    def rows(width):
        return pl.BlockSpec((tm, width), lambda i: (i, 0))

    return pl.pallas_call(
        _merge_kernel,
        grid=(n // tm,),
        in_specs=[rows(d), rows(aw), rows(d), rows(d), full(w_br_a.shape), full(w_o.shape),
                  full((1, d)), full(wr_hi.shape), full(wr_lo.shape), full((1, N_EXPERTS))],
        out_specs=[rows(d), rows(d // 2), rows(LANES), rows(LANES), rows(LANES),
                   full((1, N_EXPERTS))],
        out_shape=[jax.ShapeDtypeStruct((n, d), F32),
                   jax.ShapeDtypeStruct((n, d // 2), jnp.int32),
                   jax.ShapeDtypeStruct((n, LANES), jnp.int32),
                   jax.ShapeDtypeStruct((n, LANES), F32),
                   jax.ShapeDtypeStruct((n, LANES), jnp.int32),
                   jax.ShapeDtypeStruct((1, N_EXPERTS), jnp.int32)],
        scratch_shapes=[pltpu.VMEM((1, N_EXPERTS), F32)],
        compiler_params=pltpu.CompilerParams(dimension_semantics=("arbitrary",),
                                             vmem_limit_bytes=VMEM_LIMIT),
        name="merge_route",
    )(x2, ya, gas, pb, w_br_a, w_o, g_ffn, wr_hi, wr_lo, b_router)


def _sc_mesh():
    info = plsc.get_sparse_core_info()
    mesh = plsc.VectorSubcoreMesh(core_axis_name="c", subcore_axis_name="s")
    return mesh, info.num_cores, info.num_cores * info.num_subcores


def _sc_dispatch(src, dest, pad_idx, n_out_rows):
    n, w = src.shape
    mesh, n_cores, n_workers = _sc_mesh()
    per_w = n // n_workers
    n_chunks = per_w // SC_ROWS
    n_pad_chunks = pad_idx.size // (n_workers * SC_ROWS)

    @functools.partial(
        pl.kernel, mesh=mesh,
        out_type=jax.ShapeDtypeStruct((n_out_rows, w), jnp.int32),
        scratch_types=[pltpu.VMEM((n_chunks, TOP_K, SC_ROWS), jnp.int32),
                       pltpu.VMEM((n_pad_chunks, SC_ROWS), jnp.int32),
                       pltpu.VMEM((SC_ROWS, w), jnp.int32),
                       pltpu.VMEM((SC_ROWS, w), jnp.int32)],
        name="sc_dispatch",
    )
    def run(src_hbm, dest_hbm, pad_hbm, zeros_hbm, out_hbm, idx_v, pad_v, rows_v, zero_v):
        wid = lax.axis_index("s") * n_cores + lax.axis_index("c")
        pltpu.sync_copy(dest_hbm.at[wid], idx_v)
        pltpu.sync_copy(pad_hbm.at[wid], pad_v)
        pltpu.sync_copy(zeros_hbm, zero_v)

        @pl.loop(0, n_pad_chunks)
        def _(c):
            pltpu.sync_copy(zero_v, out_hbm.at[pad_v.at[c]])

        @pl.loop(0, n_chunks)
        def _(c):
            pltpu.sync_copy(src_hbm.at[pl.ds(wid * per_w + c * SC_ROWS, SC_ROWS)], rows_v)
            for k in range(TOP_K):
                pltpu.sync_copy(rows_v, out_hbm.at[idx_v.at[c, k]])

    dest_w = dest.reshape(n_workers, n_chunks, SC_ROWS, TOP_K).transpose(0, 1, 3, 2)
    return run(src, dest_w, pad_idx.reshape(n_workers, n_pad_chunks, SC_ROWS),
               jnp.zeros((SC_ROWS, w), jnp.int32))


def _sc_gather(table, idx):
    n_rows = idx.shape[0]
    w = table.shape[1]
    mesh, n_cores, n_workers = _sc_mesh()
    per_w = n_rows // n_workers
    n_chunks = per_w // SC_ROWS
    assert n_chunks % 2 == 0 and n_chunks * SC_ROWS * n_workers == n_rows

    @functools.partial(
        pl.kernel, mesh=mesh,
        out_type=jax.ShapeDtypeStruct((n_rows, w), table.dtype),
        scratch_types=[pltpu.VMEM((n_chunks, SC_ROWS), jnp.int32),
                       pltpu.VMEM((SC_ROWS, w), table.dtype),
                       pltpu.VMEM((SC_ROWS, w), table.dtype),
                       pltpu.SemaphoreType.DMA, pltpu.SemaphoreType.DMA],
        name="sc_gather",
    )
    def run(table_hbm, idx_hbm, out_hbm, idx_v, buf0, buf1, sem0, sem1):
        wid = lax.axis_index("s") * n_cores + lax.axis_index("c")
        base = wid * per_w
        pltpu.sync_copy(idx_hbm.at[wid], idx_v)

        def gather(c, buf, sem):
            return pltpu.make_async_copy(table_hbm.at[idx_v.at[c]], buf, sem)

        gather(0, buf0, sem0).start()

        @pl.loop(0, n_chunks, step=2)
        def _(c):
            gather(c + 1, buf1, sem1).start()
            gather(c, buf0, sem0).wait()
            pltpu.sync_copy(buf0, out_hbm.at[pl.ds(base + c * SC_ROWS, SC_ROWS)])

            @pl.when(c + 2 < n_chunks)
            def _():
                gather(c + 2, buf0, sem0).start()

            gather(c + 1, buf1, sem1).wait()
            pltpu.sync_copy(buf1, out_hbm.at[pl.ds(base + (c + 1) * SC_ROWS, SC_ROWS)])

    return run(table, idx.reshape(n_workers, n_chunks, SC_ROWS))


def _expert_kernel(be_ref, nu_ref, xs_ref, wgu_ref, bgu_ref, wd_ref, bd_ref, y_ref,
                   wgu_bf, wd_bf):
    i = pl.program_id(0)
    f = wd_ref.shape[1]
    half = xs_ref.shape[1]
    live = i < nu_ref[0]

    @pl.when(live & ((i == 0) | (be_ref[i] != be_ref[jnp.maximum(i - 1, 0)])))
    def _():
        wgu_bf[...] = wgu_ref[0].astype(BF16)
        wd_bf[...] = wd_ref[0].astype(BF16)

    @pl.when(live)
    def _():
        x_lo, x_hi = _unpack_bf16_pairs(xs_ref[...])
        gu = (_dot(x_lo.astype(BF16), wgu_bf[0:half, :])
              + _dot(x_hi.astype(BF16), wgu_bf[half:2 * half, :])
              + bgu_ref[0])
        gate = jnp.minimum(gu[:, :f], SWIGLU_LIMIT)
        up = jnp.clip(gu[:, f:], -SWIGLU_LIMIT, SWIGLU_LIMIT)
        act = (up + 1.0) * (gate * jax.nn.sigmoid(SWIGLU_ALPHA * gate))
        y = _dot(act.astype(BF16), wd_bf[...]) + bd_ref[0]
        y_ref[...] = _pack_bf16_pairs(y)


def _experts(block_e, n_used, xs, n_blocks, w_gu, b_gu, w_down, b_down):
    half = xs.shape[1]
    d = 2 * half
    f2 = w_gu.shape[2]
    f = w_down.shape[1]

    def blk(i, be, nu):
        return (jnp.minimum(i, nu[0] - 1), 0)

    def expert(i, be, nu):
        return (be[jnp.minimum(i, nu[0] - 1)], 0, 0)

    grid_spec = pltpu.PrefetchScalarGridSpec(
        num_scalar_prefetch=2,
        grid=(n_blocks,),
        in_specs=[pl.BlockSpec((EXPERT_BLOCK, half), blk),
                  pl.BlockSpec((1, d, f2), expert),
                  pl.BlockSpec((1, 1, f2), expert),
                  pl.BlockSpec((1, f, d), expert),
                  pl.BlockSpec((1, 1, d), expert)],
        out_specs=pl.BlockSpec((EXPERT_BLOCK, half), blk),
        scratch_shapes=[pltpu.VMEM((d, f2), BF16), pltpu.VMEM((f, d), BF16)],
    )
    return pl.pallas_call(
        _expert_kernel,
        grid_spec=grid_spec,
        out_shape=jax.ShapeDtypeStruct((n_blocks * EXPERT_BLOCK, half), jnp.int32),
        compiler_params=pltpu.CompilerParams(dimension_semantics=("arbitrary",),
                                             vmem_limit_bytes=VMEM_LIMIT),
        name="experts",
    )(block_e, n_used, xs, w_gu, b_gu, w_down, b_down)


def _combine_kernel(yg_ref, x1_ref, gate_ref, gfin_ref, out_ref):
    gates = gate_ref[...]
    half = yg_ref.shape[2]
    acc_lo = x1_ref[:, :half]
    acc_hi = x1_ref[:, half:]
    for k in range(TOP_K):
        y_lo, y_hi = _unpack_bf16_pairs(yg_ref[k])
        acc_lo = acc_lo + gates[:, k:k + 1] * y_lo
        acc_hi = acc_hi + gates[:, k:k + 1] * y_hi
    out_ref[...] = _rms(jnp.concatenate([acc_lo, acc_hi], axis=1), gfin_ref[...])


def _combine(yg, x1, gates, g_final, tm):
    n, d = x1.shape
    return pl.pallas_call(
        _combine_kernel,
        grid=(n // tm,),
        in_specs=[pl.BlockSpec((TOP_K, tm, d // 2), lambda i: (0, i, 0)),
                  pl.BlockSpec((tm, d), lambda i: (i, 0)),
                  pl.BlockSpec((tm, LANES), lambda i: (i, 0)),
                  pl.BlockSpec((1, d), lambda i: (0, 0))],
        out_specs=pl.BlockSpec((tm, d), lambda i: (i, 0)),
        out_shape=jax.ShapeDtypeStruct((n, d), F32),
        compiler_params=pltpu.CompilerParams(dimension_semantics=("parallel",),
                                             vmem_limit_bytes=VMEM_LIMIT),
        name="combine",
    )(yg, x1, gates, g_final)


def _pack_in_proj(w_in, d):
    offs = [0]
    for width in (Q_LORA, KV_LORA, IDX_DIM, IDX_HEADS, SGU_WIDTH, SGU_WIDTH, d, d):
        offs.append(offs[-1] + width)
    wq, wkv, wki, wwi, wu, wv, wga, wgb = [w_in[:, offs[j]:offs[j + 1]] for j in range(8)]
    wki_rep = jnp.tile(wki, (1, IDX_HEADS))
    wwi_pad = jnp.pad(wwi, ((0, 0), (0, LANES - IDX_HEADS)))
    return jnp.concatenate([wq, wkv, wki_rep, wwi_pad, wu, wv, wga, wgb], axis=1).astype(BF16)


def _layer(x, g_mix, w_in, g_cq, g_ckv, w_uq, w_uv, w_q_idx, g_kidx, b_kidx, g_sgu, b_sgu,
           w_spatial, b_spatial, w_br_a, w_br_b, w_o, g_ffn, w_router, b_router, w_gu, b_gu,
           w_down, b_down, g_final):
    b, s, d = x.shape
    n = b * s
    x2 = x.reshape(n, d)
    row = lambda v: v.reshape(1, -1).astype(F32)

    w_in_p = _pack_in_proj(w_in, d)
    ws_pair = w_spatial.reshape(SGU_GROUPS // 2, 2, SGU_CHUNK, SGU_CHUNK).transpose(
        0, 2, 1, 3).reshape(SGU_GROUPS // 2, SGU_CHUNK, 2 * SGU_CHUNK)
    bsp = jnp.repeat(b_spatial.T, SGU_GROUP_DIM, axis=1)
    cq, ckv, ki, wi, gas, pb = _inproj(
        x2, row(g_mix), w_in_p, row(g_cq), row(g_ckv), row(jnp.tile(g_kidx, IDX_HEADS)),
        row(jnp.tile(b_kidx, IDX_HEADS)), row(g_sgu), row(b_sgu), ws_pair, bsp,
        w_br_b.astype(BF16), tm=512)

    head_eye = jnp.eye(N_HEADS, dtype=F32)
    w_uv_pad = (w_uv[:, :, None, :] * head_eye[:, None, :, None]).reshape(
        N_HEADS, KV_LORA, N_HEADS * HEAD_DIM)
    n_buckets = next(nbk for nbk in (8, 4, 2, 1) if (s // Q_BLOCK) % nbk == 0)
    ya = _attn(cq.reshape(b, s, -1), ckv.reshape(b, s, -1), ki.reshape(b, s, -1),
               wi.reshape(b, s, -1), w_uq.astype(BF16), w_q_idx.astype(BF16),
               w_uv_pad.astype(BF16), n_buckets)

    wr_hi = w_router.astype(BF16)
    wr_lo = (w_router - wr_hi.astype(F32)).astype(BF16)
    x1, h2p, idx_p, gate_p, rank_p, counts = _merge(
        x2, ya.reshape(n, -1), gas, pb, w_br_a.astype(BF16), w_o.astype(BF16), row(g_ffn),
        wr_hi, wr_lo, row(b_router), tm=512)

    counts = counts[0]
    top_idx = idx_p[:, :TOP_K]
    padded = (counts + EXPERT_BLOCK - 1) // EXPERT_BLOCK * EXPERT_BLOCK
    padded_end = jnp.cumsum(padded)
    padded_start = padded_end - padded
    dest = padded_start[top_idx] + rank_p[:, :TOP_K]
    nk = n * TOP_K
    nb = -(-(nk + N_EXPERTS * EXPERT_BLOCK) // EXPERT_BLOCK)
    block_row0 = jnp.arange(nb, dtype=jnp.int32) * EXPERT_BLOCK
    block_e = jnp.minimum(
        jnp.sum((padded_end[None, :] <= block_row0[:, None]).astype(jnp.int32), axis=1),
        N_EXPERTS - 1).astype(jnp.int32)
    n_used = (padded_end[-1:] // EXPERT_BLOCK).astype(jnp.int32)
    j = jnp.arange(EXPERT_BLOCK, dtype=jnp.int32)[None, :]
    pad_idx = jnp.where(j < (padded - counts)[:, None], (padded_start + counts)[:, None] + j,
                        nb * EXPERT_BLOCK + j).astype(jnp.int32)

    xs = _sc_dispatch(h2p, dest, pad_idx, (nb + 1) * EXPERT_BLOCK)
    ybuf = _experts(block_e, n_used, xs, nb, w_gu, b_gu.reshape(N_EXPERTS, 1, -1), w_down,
                    b_down.reshape(N_EXPERTS, 1, -1))
    yg = _sc_gather(ybuf, dest.T.reshape(-1))
    out = _combine(yg.reshape(TOP_K, n, d // 2), x1, gate_p, row(g_final), tm=256)
    return out.reshape(b, s, d)


def kernel(x, g_mix, w_in, g_cq, g_ckv, w_uq, w_uv, w_q_idx, g_kidx, b_kidx, g_sgu, b_sgu,
           w_spatial, b_spatial, w_br_a, w_br_b, w_o, g_ffn, w_router, b_router, w_gu, b_gu,
           w_down, b_down, g_final):
    assert g_mix.shape[0] == 1, "single-layer block"
    return _layer(x, g_mix[0], w_in[0], g_cq[0], g_ckv[0], w_uq[0], w_uv[0], w_q_idx[0],
                  g_kidx[0], b_kidx[0], g_sgu[0], b_sgu[0], w_spatial[0], b_spatial[0],
                  w_br_a[0], w_br_b[0], w_o[0], g_ffn[0], w_router[0], b_router[0], w_gu[0],
                  b_gu[0], w_down[0], b_down[0], g_final)
```

```python
import functools

import jax
import jax.numpy as jnp
from jax import lax
from jax.experimental import pallas as pl
from jax.experimental.pallas import tpu as pltpu
from jax.experimental.pallas import tpu_sc as plsc

EPS = 1e-6
CHUNK = 64
N_HEADS = 8
HEAD_DIM = 64
Q_LORA = 256
KV_LORA = 128
IDX_HEADS = 8
IDX_DIM = 32
TOPK_MAX = 256
Q_BLOCK = 256
ATTN_SCALE = KV_LORA ** -0.5
IDX_SCALE = (IDX_HEADS * IDX_DIM) ** -0.5
SGU_CHUNK = 128
SGU_GROUPS = 8
SGU_WIDTH = 512
SGU_GROUP_DIM = SGU_WIDTH // SGU_GROUPS
N_EXPERTS = 32
TOP_K = 4
SWIGLU_LIMIT = 7.0
SWIGLU_ALPHA = 1.702
EXPERT_BLOCK = 512

SC_ROWS = 32
LANES = 128
SUBLANES = 8
VMEM_LIMIT = 56 * 1024 * 1024
INT_MIN = -(2 ** 31)
CODE_NEG_INF = INT_MIN + 0x7FFFFF

BF16 = jnp.bfloat16
F32 = jnp.float32


def _dot(a, b):
    return jnp.dot(a, b, preferred_element_type=F32)


def _dot_nt(a, b):
    return lax.dot_general(a, b, (((1,), (1,)), ((), ())), preferred_element_type=F32)


def _rms(x, g):
    return x * lax.rsqrt(jnp.mean(x * x, axis=-1, keepdims=True) + EPS) * g


def _pack_bf16_pairs(x):
    half = x.shape[1] // 2
    bits = lax.bitcast_convert_type(x.astype(BF16).astype(F32), jnp.int32)
    return lax.shift_right_logical(bits[:, :half], 16) | (bits[:, half:] & jnp.int32(-65536))


def _unpack_bf16_pairs(words):
    lo = lax.bitcast_convert_type(lax.shift_left(words, 16), F32)
    hi = lax.bitcast_convert_type(words & jnp.int32(-65536), F32)
    return lo, hi


def _layer_norm(x, g, b):
    mu = jnp.mean(x, axis=-1, keepdims=True)
    xc = x - mu
    var = jnp.mean(xc * xc, axis=-1, keepdims=True)
    return xc * lax.rsqrt(var + EPS) * g + b


C_Q = 0
C_KV = C_Q + Q_LORA
C_KI = C_KV + KV_LORA
C_WI = C_KI + IDX_HEADS * IDX_DIM
C_U = C_WI + LANES
C_V = C_U + SGU_WIDTH
D_IN_P_BASE = C_V + SGU_WIDTH


def _inproj_kernel(x_ref, gmix_ref, w_ref, gcq_ref, gckv_ref, gki_ref, bki_ref,
                   gsgu_ref, bsgu_ref, wsp_ref, bsp_ref, wbrb_ref,
                   cq_ref, ckv_ref, ki_ref, wi_ref, gas_ref, pb_ref):
    tm, d = x_ref.shape
    c_ga = D_IN_P_BASE
    c_gb = c_ga + d
    h = _rms(x_ref[...], gmix_ref[...]).astype(BF16)

    def proj(lo, width):
        return _dot(h, w_ref[:, lo:lo + width])

    cq_ref[...] = _rms(proj(C_Q, Q_LORA), gcq_ref[...]).astype(BF16)
    ckv_ref[...] = _rms(proj(C_KV, KV_LORA), gckv_ref[...]).astype(BF16)
    ki_ref[...] = _layer_norm(proj(C_KI, IDX_HEADS * IDX_DIM), gki_ref[...],
                              bki_ref[...]).astype(BF16)
    wi_ref[...] = proj(C_WI, LANES) * IDX_SCALE
    gas_ref[...] = jax.nn.sigmoid(proj(c_ga, d)).astype(BF16)

    u = jax.nn.gelu(proj(C_U, SGU_WIDTH))
    v = _layer_norm(jax.nn.gelu(proj(C_V, SGU_WIDTH)), gsgu_ref[...], bsgu_ref[...])

    row = lax.broadcasted_iota(jnp.int32, (SGU_CHUNK, 2 * SGU_CHUNK), 0)
    col = lax.broadcasted_iota(jnp.int32, (SGU_CHUNK, 2 * SGU_CHUNK), 1) % SGU_CHUNK
    causal = (row // CHUNK) >= (col // CHUNK)
    lane = lax.broadcasted_iota(jnp.int32, (SGU_CHUNK, LANES), 1)
    left = lane < SGU_GROUP_DIM
    n_tiles = SGU_WIDTH // LANES
    ws = [jnp.where(causal, wsp_ref[j], 0.0).astype(BF16) for j in range(n_tiles)]
    yb_chunks = []
    for c in range(tm // SGU_CHUNK):
        tiles = []
        for j in range(n_tiles):
            blk = v[c * SGU_CHUNK:(c + 1) * SGU_CHUNK, j * LANES:(j + 1) * LANES]
            stacked = jnp.concatenate(
                [jnp.where(left, blk, 0.0), jnp.where(left, 0.0, blk)], axis=0).astype(BF16)
            tiles.append(_dot(ws[j], stacked))
        s = jnp.concatenate(tiles, axis=1) + bsp_ref[...]
        yb_chunks.append(u[c * SGU_CHUNK:(c + 1) * SGU_CHUNK, :] * s)
    yb = jnp.concatenate(yb_chunks, axis=0).astype(BF16)
    pb_ref[...] = (jax.nn.sigmoid(proj(c_gb, d)) * _dot(yb, wbrb_ref[...])).astype(BF16)


def _inproj(x2, g_mix, w_in_p, g_cq, g_ckv, g_ki, b_ki, g_sgu, b_sgu, ws_pair, bsp, w_br_b,
            tm):
    n, d = x2.shape
    d_in_p = w_in_p.shape[1]

    def full(shape):
        return pl.BlockSpec(shape, lambda i: (0,) * len(shape))

    def rows(width):
        return pl.BlockSpec((tm, width), lambda i: (i, 0))

    return pl.pallas_call(
        _inproj_kernel,
        grid=(n // tm,),
        in_specs=[rows(d), full((1, d)), full((d, d_in_p)), full((1, Q_LORA)),
                  full((1, KV_LORA)), full((1, IDX_HEADS * IDX_DIM)),
                  full((1, IDX_HEADS * IDX_DIM)), full((1, SGU_WIDTH)), full((1, SGU_WIDTH)),
                  full(ws_pair.shape), full(bsp.shape), full(w_br_b.shape)],
        out_specs=[rows(Q_LORA), rows(KV_LORA), rows(IDX_HEADS * IDX_DIM), rows(LANES),
                   rows(d), rows(d)],
        out_shape=[jax.ShapeDtypeStruct((n, Q_LORA), BF16),
                   jax.ShapeDtypeStruct((n, KV_LORA), BF16),
                   jax.ShapeDtypeStruct((n, IDX_HEADS * IDX_DIM), BF16),
                   jax.ShapeDtypeStruct((n, LANES), F32),
                   jax.ShapeDtypeStruct((n, d), BF16),
                   jax.ShapeDtypeStruct((n, d), BF16)],
        compiler_params=pltpu.CompilerParams(dimension_semantics=("parallel",),
                                             vmem_limit_bytes=VMEM_LIMIT),
        name="inproj",
    )(x2, g_mix, w_in_p, g_cq, g_ckv, g_ki, b_ki, g_sgu, b_sgu, ws_pair, bsp, w_br_b)


def _attn_body(s_eff, topk, q_block0, cq_ref, ckv_ref, ki_ref, wi_ref, wuq_ref, wqi_ref,
               wuv_ref, ya_ref, q_scr, qi_scr, wcol_scr, sc_scr, bias_scr, acc_scr, s_buf,
               p_buf, kva_scr, sc16_scr):
    i = pl.program_id(1) + q_block0
    cq = cq_ref[0]
    q = _dot(cq, wuq_ref[...])
    for h in range(N_HEADS):
        q_scr[h] = (q[:, h * KV_LORA:(h + 1) * KV_LORA] * ATTN_SCALE).astype(BF16)

    rowq = lax.broadcasted_iota(jnp.int32, (Q_BLOCK, s_eff), 0)
    colk = lax.broadcasted_iota(jnp.int32, (Q_BLOCK, s_eff), 1)
    allowed = (colk // CHUNK) <= ((i * Q_BLOCK + rowq) // CHUNK)

    if s_eff <= topk:
        bias_scr[...] = jnp.where(allowed, 0.0, -jnp.inf)
    else:
        _select_topk(s_eff, topk, cq, allowed, colk, ki_ref, wi_ref, wqi_ref, qi_scr, wcol_scr,
                     sc_scr, bias_scr, s_buf, sc16_scr)

    acc_scr[...] = jnp.zeros_like(acc_scr)

    def logits(h):
        s_buf[h % 2] = _dot_nt(q_scr[h], ckv_ref[0, 0:s_eff, :]) + bias_scr[...]

    def softmax(h):
        s = s_buf[h % 2]
        p_buf[h % 2] = jnp.exp(s - jnp.max(s, axis=1, keepdims=True)).astype(BF16)

    def values(h):
        ol = _dot(p_buf[h % 2], kva_scr[0:s_eff, :])
        o = ol[:, :KV_LORA] / ol[:, KV_LORA:]
        acc_scr[...] += _dot(o.astype(BF16), wuv_ref[h])

    logits(0)
    for h in range(N_HEADS):
        if h + 1 < N_HEADS:
            logits(h + 1)
        softmax(h)
        values(h)
    ya_ref[0] = acc_scr[...].astype(BF16)


def _select_topk(s_eff, topk, cq, allowed, colk, ki_ref, wi_ref, wqi_ref, qi_scr, wcol_scr,
                 sc_scr, bias_scr, s_buf, sc16_scr):
    qi = _dot(cq, wqi_ref[...]).astype(BF16)
    wi = wi_ref[0]
    head_of_lane = lax.broadcasted_iota(jnp.int32, qi.shape, 1) // IDX_DIM
    for h in range(IDX_HEADS):
        qi_scr[h] = jnp.where(head_of_lane == h, qi, jnp.zeros_like(qi))
        wcol_scr[h] = wi[:, h:h + 1]

    def index_dots(h):
        s_buf[h % 2] = _dot_nt(qi_scr[h], ki_ref[0, 0:s_eff, :])

    index_dots(0)
    for h in range(IDX_HEADS):
        if h + 1 < IDX_HEADS:
            index_dots(h + 1)
        term = wcol_scr[h] * jnp.maximum(s_buf[h % 2], 0.0)
        bias_scr[...] = term if h == 0 else bias_scr[...] + term

    sc_scr[...] = jnp.where(allowed, bias_scr[...], -jnp.inf)

    def code_to_bits(code):
        return jnp.where(code < 0, code ^ jnp.int32(0x7FFFFFFF), code)

    def code_to_float(code):
        return lax.bitcast_convert_type(code_to_bits(code), F32)

    upper_half = jnp.int32(-65536)
    sc_bits = lax.bitcast_convert_type(sc_scr[...], jnp.int32)
    sc16_scr[...] = lax.bitcast_convert_type(sc_bits & upper_half, F32).astype(BF16)

    def count_ge_upper(cand):
        thr16 = lax.bitcast_convert_type(code_to_bits(cand) & upper_half, F32).astype(BF16)
        part = None
        for t in range(s_eff // LANES):
            tile = sc16_scr[:, t * LANES:(t + 1) * LANES]
            ind = jnp.where(tile >= thr16, jnp.ones((), BF16), jnp.zeros((), BF16))
            part = ind if part is None else part + ind
        return jnp.sum(part.astype(F32), axis=1, keepdims=True)

    def count_ge(cand):
        return jnp.sum(jnp.where(sc_scr[...] >= code_to_float(cand), 1.0, 0.0), axis=1,
                       keepdims=True)

    kf = float(topk)
    code = jnp.where(count_ge_upper(jnp.zeros((Q_BLOCK, 1), jnp.int32)) >= kf,
                     jnp.int32(0), jnp.int32(INT_MIN))
    for bit in range(30, -1, -1):
        cand = code + jnp.int32(1 << bit)
        cnt = count_ge_upper(cand) if bit >= 16 else count_ge(cand)
        feasible = (cnt >= kf) | (cand <= jnp.int32(CODE_NEG_INF))
        code = jnp.where(feasible, cand, code)

    thr = code_to_float(code)
    thr_up = code_to_float(code + 1)
    sel = sc_scr[...] >= thr
    n_ge = jnp.sum(jnp.where(sel, 1.0, 0.0), axis=1, keepdims=True)
    bias_scr[...] = jnp.where(sel & allowed, 0.0, -jnp.inf)
    tie_rows = (n_ge > kf) & (code > jnp.int32(CODE_NEG_INF))
    any_tie = jnp.max(jnp.where(tie_rows, 1.0, 0.0))

    @pl.when(any_tie > 0.0)
    def _():
        sc = sc_scr[...]
        gt = sc >= thr_up
        eq = (sc >= thr) & jnp.logical_not(gt)
        need = kf - jnp.sum(jnp.where(gt, 1.0, 0.0), axis=1, keepdims=True)
        nbits = (s_eff - 1).bit_length()

        def tie_step(it, j):
            cand = j + jnp.left_shift(jnp.int32(1), jnp.int32(nbits - 1) - it)
            v = sc_scr[...]
            tie = (v >= thr) & jnp.logical_not(v >= thr_up) & (colk < cand)
            cnt = jnp.sum(jnp.where(tie, 1.0, 0.0), axis=1, keepdims=True)
            return jnp.where(cnt < need, cand, j)

        jmax = lax.fori_loop(0, nbits, tie_step, jnp.zeros((Q_BLOCK, 1), jnp.int32))
        keep = gt | (eq & (colk <= jmax))
        bias_scr[...] = jnp.where(keep & allowed, 0.0, -jnp.inf)


def _attn_kernel(s_eff, topk, q_block0, cq_ref, ckv_ref, ki_ref, wi_ref, wuq_ref, wqi_ref,
                 wuv_ref, ya_ref, q_scr, qi_scr, wcol_scr, sc_scr, bias_scr, acc_scr, s_buf,
                 p_buf, kva_scr, sc16_scr):
    @pl.when(pl.program_id(1) == 0)
    def _():
        kva_scr[:, 0:KV_LORA] = ckv_ref[0]
        kva_scr[:, KV_LORA:] = jnp.ones((s_eff, KV_LORA), BF16)

    _attn_body(s_eff, topk, q_block0, cq_ref, ckv_ref, ki_ref, wi_ref, wuq_ref, wqi_ref,
               wuv_ref, ya_ref, q_scr, qi_scr, wcol_scr, sc_scr, bias_scr, acc_scr, s_buf,
               p_buf, kva_scr, sc16_scr)


def _attn(cq, ckv, ki, wi, w_uq, w_qi, w_uv_pad, n_buckets):
    b, s, _ = cq.shape
    nq = s // Q_BLOCK
    per = nq // n_buckets
    topk = min(TOPK_MAX, s // 4)
    aw = N_HEADS * HEAD_DIM

    def full(shape):
        return pl.BlockSpec(shape, lambda bi, i: (0,) * len(shape))

    parts = []
    for k in range(n_buckets):
        s_eff = (k + 1) * per * Q_BLOCK
        q0 = k * per

        def q_rows(width, q0=q0):
            return pl.BlockSpec((1, Q_BLOCK, width), lambda bi, i: (bi, i + q0, 0))

        def keys(width, s_eff=s_eff):
            return pl.BlockSpec((1, s_eff, width), lambda bi, i: (bi, 0, 0))

        parts.append(pl.pallas_call(
            functools.partial(_attn_kernel, s_eff, topk, q0),
            grid=(b, per),
            in_specs=[q_rows(Q_LORA), keys(KV_LORA), keys(IDX_HEADS * IDX_DIM), q_rows(LANES),
                      full(w_uq.shape), full(w_qi.shape), full(w_uv_pad.shape)],
            out_specs=pl.BlockSpec((1, Q_BLOCK, aw), lambda bi, i: (bi, i, 0)),
            out_shape=jax.ShapeDtypeStruct((b, per * Q_BLOCK, aw), BF16),
            scratch_shapes=[pltpu.VMEM((N_HEADS, Q_BLOCK, KV_LORA), BF16),
                            pltpu.VMEM((IDX_HEADS, Q_BLOCK, IDX_HEADS * IDX_DIM), BF16),
                            pltpu.VMEM((IDX_HEADS, Q_BLOCK, 1), F32),
                            pltpu.VMEM((Q_BLOCK, s_eff), F32),
                            pltpu.VMEM((Q_BLOCK, s_eff), F32),
                            pltpu.VMEM((Q_BLOCK, aw), F32),
                            pltpu.VMEM((2, Q_BLOCK, s_eff), F32),
                            pltpu.VMEM((2, Q_BLOCK, s_eff), BF16),
                            pltpu.VMEM((s_eff, 2 * KV_LORA), BF16),
                            pltpu.VMEM((Q_BLOCK, s_eff), BF16)],
            compiler_params=pltpu.CompilerParams(
                dimension_semantics=("parallel", "arbitrary"), vmem_limit_bytes=VMEM_LIMIT),
            name=f"attn_keys{s_eff}",
        )(cq, ckv, ki, wi, w_uq, w_qi, w_uv_pad))
    return jnp.concatenate(parts, axis=1)


def _merge_kernel(x_ref, ya_ref, gas_ref, pb_ref, wbra_ref, wo_ref, gffn_ref,
                  wr_hi_ref, wr_lo_ref, br_ref,
                  x1_ref, h2p_ref, idx_ref, gate_ref, rank_ref, cnt_ref, base_scr):
    step = pl.program_id(0)
    tm, d = x_ref.shape

    @pl.when(step == 0)
    def _():
        base_scr[...] = jnp.zeros_like(base_scr)

    a = _dot(ya_ref[...], wbra_ref[...])
    merged = gas_ref[...].astype(F32) * a + pb_ref[...].astype(F32)
    x1 = x_ref[...] + _dot(merged.astype(BF16), wo_ref[...])
    x1_ref[...] = x1
    h2 = _rms(x1, gffn_ref[...])

    h_hi = h2.astype(BF16)
    h2p_ref[...] = _pack_bf16_pairs(h2)
    h_lo = (h2 - h_hi.astype(F32)).astype(BF16)
    logits = (_dot(h_hi, wr_hi_ref[...]) + _dot(h_hi, wr_lo_ref[...])
              + _dot(h_lo, wr_hi_ref[...]) + br_ref[...])

    lane_e = lax.broadcasted_iota(jnp.int32, (tm, N_EXPERTS), 1).astype(F32)
    lane_o = lax.broadcasted_iota(jnp.int32, (tm, LANES), 1)
    work = logits
    vals, idxs = [], []
    onehot = jnp.zeros((tm, N_EXPERTS), F32)
    for _ in range(TOP_K):
        m = jnp.max(work, axis=1, keepdims=True)
        idx = jnp.min(jnp.where(work == m, lane_e, float(N_EXPERTS)), axis=1, keepdims=True)
        hit = lane_e == idx
        onehot = onehot + jnp.where(hit, 1.0, 0.0)
        work = jnp.where(hit, -jnp.inf, work)
        vals.append(m)
        idxs.append(idx)
    exps = [jnp.exp(v - vals[0]) for v in vals]
    denom = exps[0] + exps[1] + exps[2] + exps[3]

    r = lax.broadcasted_iota(jnp.int32, (tm, tm), 0)
    c = lax.broadcasted_iota(jnp.int32, (tm, tm), 1)
    tri = jnp.where(c < r, 1.0, 0.0).astype(BF16)
    rank_full = _dot(tri, onehot.astype(BF16)) + base_scr[...]

    idx_out = jnp.zeros((tm, LANES), F32)
    gate_out = jnp.zeros((tm, LANES), F32)
    rank_out = jnp.zeros((tm, LANES), F32)
    for k in range(TOP_K):
        rk = jnp.sum(jnp.where(lane_e == idxs[k], rank_full, 0.0), axis=1, keepdims=True)
        idx_out = jnp.where(lane_o == k, idxs[k], idx_out)
        gate_out = jnp.where(lane_o == k, exps[k] / denom, gate_out)
        rank_out = jnp.where(lane_o == k, rk, rank_out)
    gate_ref[...] = gate_out
    idx_ref[...] = idx_out.T[0:SUBLANES, :].astype(jnp.int32)
    rank_ref[...] = rank_out.T[0:SUBLANES, :].astype(jnp.int32)

    base_scr[...] = base_scr[...] + jnp.sum(onehot, axis=0, keepdims=True)
    cnt_ref[...] = base_scr[...].astype(jnp.int32)


def _merge(x2, ya, gas, pb, w_br_a, w_o, g_ffn, wr_hi, wr_lo, b_router, tm):
    n, d = x2.shape
    aw = ya.shape[1]

    def full(shape):
        return pl.BlockSpec(shape, lambda i: (0,) * len(shape))

    def rows(width):
        return pl.BlockSpec((tm, width), lambda i: (i, 0))

    slots = pl.BlockSpec((SUBLANES, tm), lambda i: (0, i))
    return pl.pallas_call(
        _merge_kernel,
        grid=(n // tm,),
        in_specs=[rows(d), rows(aw), rows(d), rows(d), full(w_br_a.shape), full(w_o.shape),
                  full((1, d)), full(wr_hi.shape), full(wr_lo.shape), full((1, N_EXPERTS))],
        out_specs=[rows(d), rows(d // 2), slots, rows(LANES), slots, full((1, N_EXPERTS))],
        out_shape=[jax.ShapeDtypeStruct((n, d), F32),
                   jax.ShapeDtypeStruct((n, d // 2), jnp.int32),
                   jax.ShapeDtypeStruct((SUBLANES, n), jnp.int32),
                   jax.ShapeDtypeStruct((n, LANES), F32),
                   jax.ShapeDtypeStruct((SUBLANES, n), jnp.int32),
                   jax.ShapeDtypeStruct((1, N_EXPERTS), jnp.int32)],
        scratch_shapes=[pltpu.VMEM((1, N_EXPERTS), F32)],
        compiler_params=pltpu.CompilerParams(dimension_semantics=("arbitrary",),
                                             vmem_limit_bytes=VMEM_LIMIT),
        name="merge_route",
    )(x2, ya, gas, pb, w_br_a, w_o, g_ffn, wr_hi, wr_lo, b_router)


def _sc_mesh():
    info = plsc.get_sparse_core_info()
    mesh = plsc.VectorSubcoreMesh(core_axis_name="c", subcore_axis_name="s")
    return mesh, info.num_cores, info.num_cores * info.num_subcores


def _sc_dispatch(src, dest_t, pad_idx, n_out_rows):
    n, w = src.shape
    mesh, n_cores, n_workers = _sc_mesh()
    per_w = n // n_workers
    n_chunks = per_w // SC_ROWS
    n_pad_chunks = pad_idx.size // (n_workers * SC_ROWS)

    @functools.partial(
        pl.kernel, mesh=mesh,
        out_type=jax.ShapeDtypeStruct((n_out_rows, w), jnp.int32),
        scratch_types=[pltpu.VMEM((n_chunks, TOP_K, SC_ROWS), jnp.int32),
                       pltpu.VMEM((n_pad_chunks, SC_ROWS), jnp.int32),
                       pltpu.VMEM((SC_ROWS, w), jnp.int32),
                       pltpu.VMEM((SC_ROWS, w), jnp.int32)],
        name="sc_dispatch",
    )
    def run(src_hbm, dest_hbm, pad_hbm, zeros_hbm, out_hbm, idx_v, pad_v, rows_v, zero_v):
        wid = lax.axis_index("s") * n_cores + lax.axis_index("c")
        pltpu.sync_copy(dest_hbm.at[wid], idx_v)
        pltpu.sync_copy(pad_hbm.at[wid], pad_v)
        pltpu.sync_copy(zeros_hbm, zero_v)

        @pl.loop(0, n_pad_chunks)
        def _(c):
            pltpu.sync_copy(zero_v, out_hbm.at[pad_v.at[c]])

        @pl.loop(0, n_chunks)
        def _(c):
            pltpu.sync_copy(src_hbm.at[pl.ds(wid * per_w + c * SC_ROWS, SC_ROWS)], rows_v)
            for k in range(TOP_K):
                pltpu.sync_copy(rows_v, out_hbm.at[idx_v.at[c, k]])

    dest_w = dest_t.reshape(TOP_K, n_workers, n_chunks, SC_ROWS).transpose(1, 2, 0, 3)
    return run(src, dest_w, pad_idx.reshape(n_workers, n_pad_chunks, SC_ROWS),
               jnp.zeros((SC_ROWS, w), jnp.int32))


def _sc_gather(table, idx):
    n_rows = idx.shape[0]
    w = table.shape[1]
    mesh, n_cores, n_workers = _sc_mesh()
    per_w = n_rows // n_workers
    n_chunks = per_w // SC_ROWS
    assert n_chunks % 2 == 0 and n_chunks * SC_ROWS * n_workers == n_rows

    @functools.partial(
        pl.kernel, mesh=mesh,
        out_type=jax.ShapeDtypeStruct((n_rows, w), table.dtype),
        scratch_types=[pltpu.VMEM((n_chunks, SC_ROWS), jnp.int32),
                       pltpu.VMEM((SC_ROWS, w), table.dtype),
                       pltpu.VMEM((SC_ROWS, w), table.dtype),
                       pltpu.SemaphoreType.DMA, pltpu.SemaphoreType.DMA],
        name="sc_gather",
    )
    def run(table_hbm, idx_hbm, out_hbm, idx_v, buf0, buf1, sem0, sem1):
        wid = lax.axis_index("s") * n_cores + lax.axis_index("c")
        base = wid * per_w
        pltpu.sync_copy(idx_hbm.at[wid], idx_v)

        def gather(c, buf, sem):
            return pltpu.make_async_copy(table_hbm.at[idx_v.at[c]], buf, sem)

        gather(0, buf0, sem0).start()

        @pl.loop(0, n_chunks, step=2)
        def _(c):
            gather(c + 1, buf1, sem1).start()
            gather(c, buf0, sem0).wait()
            pltpu.sync_copy(buf0, out_hbm.at[pl.ds(base + c * SC_ROWS, SC_ROWS)])

            @pl.when(c + 2 < n_chunks)
            def _():
                gather(c + 2, buf0, sem0).start()

            gather(c + 1, buf1, sem1).wait()
            pltpu.sync_copy(buf1, out_hbm.at[pl.ds(base + (c + 1) * SC_ROWS, SC_ROWS)])

    return run(table, idx.reshape(n_workers, n_chunks, SC_ROWS))


def _expert_kernel(be_ref, nu_ref, xs_ref, wgu_ref, bgu_ref, wd_ref, bd_ref, y_ref,
                   wgu_bf, wd_bf):
    i = pl.program_id(0)
    f = wd_ref.shape[1]
    half = xs_ref.shape[1]
    live = i < nu_ref[0]

    @pl.when(live & ((i == 0) | (be_ref[i] != be_ref[jnp.maximum(i - 1, 0)])))
    def _():
        wgu_bf[...] = wgu_ref[0].astype(BF16)
        wd_bf[...] = wd_ref[0].astype(BF16)

    @pl.when(live)
    def _():
        x_lo, x_hi = _unpack_bf16_pairs(xs_ref[...])
        gu = (_dot(x_lo.astype(BF16), wgu_bf[0:half, :])
              + _dot(x_hi.astype(BF16), wgu_bf[half:2 * half, :])
              + bgu_ref[0])
        gate = jnp.minimum(gu[:, :f], SWIGLU_LIMIT)
        up = jnp.clip(gu[:, f:], -SWIGLU_LIMIT, SWIGLU_LIMIT)
        act = (up + 1.0) * (gate * jax.nn.sigmoid(SWIGLU_ALPHA * gate))
        y = _dot(act.astype(BF16), wd_bf[...]) + bd_ref[0]
        y_ref[...] = _pack_bf16_pairs(y)


def _experts(block_e, n_used, xs, n_blocks, w_gu, b_gu, w_down, b_down):
    half = xs.shape[1]
    d = 2 * half
    f2 = w_gu.shape[2]
    f = w_down.shape[1]

    def blk(i, be, nu):
        return (jnp.minimum(i, nu[0] - 1), 0)

    def expert(i, be, nu):
        return (be[jnp.minimum(i, nu[0] - 1)], 0, 0)

    grid_spec = pltpu.PrefetchScalarGridSpec(
        num_scalar_prefetch=2,
        grid=(n_blocks,),
        in_specs=[pl.BlockSpec((EXPERT_BLOCK, half), blk),
                  pl.BlockSpec((1, d, f2), expert),
                  pl.BlockSpec((1, 1, f2), expert),
                  pl.BlockSpec((1, f, d), expert),
                  pl.BlockSpec((1, 1, d), expert)],
        out_specs=pl.BlockSpec((EXPERT_BLOCK, half), blk),
        scratch_shapes=[pltpu.VMEM((d, f2), BF16), pltpu.VMEM((f, d), BF16)],
    )
    return pl.pallas_call(
        _expert_kernel,
        grid_spec=grid_spec,
        out_shape=jax.ShapeDtypeStruct((n_blocks * EXPERT_BLOCK, half), jnp.int32),
        compiler_params=pltpu.CompilerParams(dimension_semantics=("arbitrary",),
                                             vmem_limit_bytes=VMEM_LIMIT),
        name="experts",
    )(block_e, n_used, xs, w_gu, b_gu, w_down, b_down)


def _combine_kernel(yg_ref, x1_ref, gate_ref, gfin_ref, out_ref):
    gates = gate_ref[...]
    half = yg_ref.shape[2]
    acc_lo = x1_ref[:, :half]
    acc_hi = x1_ref[:, half:]
    for k in range(TOP_K):
        y_lo, y_hi = _unpack_bf16_pairs(yg_ref[k])
        acc_lo = acc_lo + gates[:, k:k + 1] * y_lo
        acc_hi = acc_hi + gates[:, k:k + 1] * y_hi
    out_ref[...] = _rms(jnp.concatenate([acc_lo, acc_hi], axis=1), gfin_ref[...])


def _combine(yg, x1, gates, g_final, tm):
    n, d = x1.shape
    return pl.pallas_call(
        _combine_kernel,
        grid=(n // tm,),
        in_specs=[pl.BlockSpec((TOP_K, tm, d // 2), lambda i: (0, i, 0)),
                  pl.BlockSpec((tm, d), lambda i: (i, 0)),
                  pl.BlockSpec((tm, LANES), lambda i: (i, 0)),
                  pl.BlockSpec((1, d), lambda i: (0, 0))],
        out_specs=pl.BlockSpec((tm, d), lambda i: (i, 0)),
        out_shape=jax.ShapeDtypeStruct((n, d), F32),
        compiler_params=pltpu.CompilerParams(dimension_semantics=("parallel",),
                                             vmem_limit_bytes=VMEM_LIMIT),
        name="combine",
    )(yg, x1, gates, g_final)


def _pack_in_proj(w_in, d):
    offs = [0]
    for width in (Q_LORA, KV_LORA, IDX_DIM, IDX_HEADS, SGU_WIDTH, SGU_WIDTH, d, d):
        offs.append(offs[-1] + width)
    wq, wkv, wki, wwi, wu, wv, wga, wgb = [w_in[:, offs[j]:offs[j + 1]] for j in range(8)]
    wki_rep = jnp.tile(wki, (1, IDX_HEADS))
    wwi_pad = jnp.pad(wwi, ((0, 0), (0, LANES - IDX_HEADS)))
    return jnp.concatenate([wq, wkv, wki_rep, wwi_pad, wu, wv, wga, wgb], axis=1).astype(BF16)


def _layer(x, g_mix, w_in, g_cq, g_ckv, w_uq, w_uv, w_q_idx, g_kidx, b_kidx, g_sgu, b_sgu,
           w_spatial, b_spatial, w_br_a, w_br_b, w_o, g_ffn, w_router, b_router, w_gu, b_gu,
           w_down, b_down, g_final):
    b, s, d = x.shape
    n = b * s
    x2 = x.reshape(n, d)
    row = lambda v: v.reshape(1, -1).astype(F32)

    w_in_p = _pack_in_proj(w_in, d)
    ws_pair = w_spatial.reshape(SGU_GROUPS // 2, 2, SGU_CHUNK, SGU_CHUNK).transpose(
        0, 2, 1, 3).reshape(SGU_GROUPS // 2, SGU_CHUNK, 2 * SGU_CHUNK)
    bsp = jnp.repeat(b_spatial.T, SGU_GROUP_DIM, axis=1)
    cq, ckv, ki, wi, gas, pb = _inproj(
        x2, row(g_mix), w_in_p, row(g_cq), row(g_ckv), row(jnp.tile(g_kidx, IDX_HEADS)),
        row(jnp.tile(b_kidx, IDX_HEADS)), row(g_sgu), row(b_sgu), ws_pair, bsp,
        w_br_b.astype(BF16), tm=512)

    head_eye = jnp.eye(N_HEADS, dtype=F32)
    w_uv_pad = (w_uv[:, :, None, :] * head_eye[:, None, :, None]).reshape(
        N_HEADS, KV_LORA, N_HEADS * HEAD_DIM)
    n_buckets = next(nbk for nbk in (8, 4, 2, 1) if (s // Q_BLOCK) % nbk == 0)
    ya = _attn(cq.reshape(b, s, -1), ckv.reshape(b, s, -1), ki.reshape(b, s, -1),
               wi.reshape(b, s, -1), w_uq.astype(BF16), w_q_idx.astype(BF16),
               w_uv_pad.astype(BF16), n_buckets)

    wr_hi = w_router.astype(BF16)
    wr_lo = (w_router - wr_hi.astype(F32)).astype(BF16)
    x1, h2p, idx_t, gate_p, rank_t, counts = _merge(
        x2, ya.reshape(n, -1), gas, pb, w_br_a.astype(BF16), w_o.astype(BF16), row(g_ffn),
        wr_hi, wr_lo, row(b_router), tm=512)

    counts = counts[0]
    padded = (counts + EXPERT_BLOCK - 1) // EXPERT_BLOCK * EXPERT_BLOCK
    padded_end = jnp.cumsum(padded)
    padded_start = padded_end - padded
    dest_t = padded_start[idx_t[:TOP_K]] + rank_t[:TOP_K]
    nk = n * TOP_K
    nb = -(-(nk + N_EXPERTS * EXPERT_BLOCK) // EXPERT_BLOCK)
    block_row0 = jnp.arange(nb, dtype=jnp.int32) * EXPERT_BLOCK
    block_e = jnp.minimum(
        jnp.sum((padded_end[None, :] <= block_row0[:, None]).astype(jnp.int32), axis=1),
        N_EXPERTS - 1).astype(jnp.int32)
    n_used = (padded_end[-1:] // EXPERT_BLOCK).astype(jnp.int32)
    j = jnp.arange(EXPERT_BLOCK, dtype=jnp.int32)[None, :]
    pad_idx = jnp.where(j < (padded - counts)[:, None], (padded_start + counts)[:, None] + j,
                        nb * EXPERT_BLOCK + j).astype(jnp.int32)

    xs = _sc_dispatch(h2p, dest_t, pad_idx, (nb + 1) * EXPERT_BLOCK)
    ybuf = _experts(block_e, n_used, xs, nb, w_gu, b_gu.reshape(N_EXPERTS, 1, -1), w_down,
                    b_down.reshape(N_EXPERTS, 1, -1))
    yg = _sc_gather(ybuf, dest_t.reshape(-1))
    out = _combine(yg.reshape(TOP_K, n, d // 2), x1, gate_p, row(g_final), tm=512)
    return out.reshape(b, s, d)


def kernel(x, g_mix, w_in, g_cq, g_ckv, w_uq, w_uv, w_q_idx, g_kidx, b_kidx, g_sgu, b_sgu,
           w_spatial, b_spatial, w_br_a, w_br_b, w_o, g_ffn, w_router, b_router, w_gu, b_gu,
           w_down, b_down, g_final):
    assert g_mix.shape[0] == 1, "single-layer block"
    return _layer(x, g_mix[0], w_in[0], g_cq[0], g_ckv[0], w_uq[0], w_uv[0], w_q_idx[0],
                  g_kidx[0], b_kidx[0], g_sgu[0], b_sgu[0], w_spatial[0], b_spatial[0],
                  w_br_a[0], w_br_b[0], w_o[0], g_ffn[0], w_router[0], b_router[0], w_gu[0],
                  b_gu[0], w_down[0], b_down[0], g_final)
```

```python
import functools

import jax
import jax.numpy as jnp
from jax import lax
from jax.experimental import pallas as pl
from jax.experimental.pallas import tpu as pltpu
from jax.experimental.pallas import tpu_sc as plsc

EPS = 1e-6
CHUNK = 64
N_HEADS = 8
HEAD_DIM = 64
Q_LORA = 256
KV_LORA = 128
IDX_HEADS = 8
IDX_DIM = 32
TOPK_MAX = 256
Q_BLOCK = 256
ATTN_SCALE = KV_LORA ** -0.5
IDX_SCALE = (IDX_HEADS * IDX_DIM) ** -0.5
SGU_CHUNK = 128
SGU_GROUPS = 8
SGU_WIDTH = 512
SGU_GROUP_DIM = SGU_WIDTH // SGU_GROUPS
N_EXPERTS = 32
TOP_K = 4
SWIGLU_LIMIT = 7.0
SWIGLU_ALPHA = 1.702
EXPERT_BLOCK = 512

SC_ROWS = 32
LANES = 128
SUBLANES = 8
VMEM_LIMIT = 56 * 1024 * 1024
INT_MIN = -(2 ** 31)
CODE_NEG_INF = INT_MIN + 0x7FFFFF

BF16 = jnp.bfloat16
F32 = jnp.float32


def _dot(a, b):
    return jnp.dot(a, b, preferred_element_type=F32)


def _dot_nt(a, b):
    return lax.dot_general(a, b, (((1,), (1,)), ((), ())), preferred_element_type=F32)


def _rms(x, g):
    return x * lax.rsqrt(jnp.mean(x * x, axis=-1, keepdims=True) + EPS) * g


def _pack_bf16_pairs(x):
    half = x.shape[1] // 2
    bits = lax.bitcast_convert_type(x.astype(BF16).astype(F32), jnp.int32)
    return lax.shift_right_logical(bits[:, :half], 16) | (bits[:, half:] & jnp.int32(-65536))


def _unpack_bf16_pairs(words):
    lo = lax.bitcast_convert_type(lax.shift_left(words, 16), F32)
    hi = lax.bitcast_convert_type(words & jnp.int32(-65536), F32)
    return lo, hi


def _layer_norm(x, g, b):
    mu = jnp.mean(x, axis=-1, keepdims=True)
    xc = x - mu
    var = jnp.mean(xc * xc, axis=-1, keepdims=True)
    return xc * lax.rsqrt(var + EPS) * g + b


C_Q = 0
C_KV = C_Q + Q_LORA
C_KI = C_KV + KV_LORA
C_WI = C_KI + IDX_HEADS * IDX_DIM
C_U = C_WI + LANES
C_V = C_U + SGU_WIDTH
D_IN_P_BASE = C_V + SGU_WIDTH


def _inproj_kernel(x_ref, gmix_ref, w_ref, gcq_ref, gckv_ref, gki_ref, bki_ref,
                   gsgu_ref, bsgu_ref, wsp_ref, bsp_ref, wbrb_ref,
                   cq_ref, ckv_ref, ki_ref, wi_ref, gas_ref, pb_ref):
    tm, d = x_ref.shape
    c_ga = D_IN_P_BASE
    c_gb = c_ga + d
    h = _rms(x_ref[...], gmix_ref[...]).astype(BF16)

    def proj(lo, width):
        return _dot(h, w_ref[:, lo:lo + width])

    cq_ref[...] = _rms(proj(C_Q, Q_LORA), gcq_ref[...]).astype(BF16)
    ckv_ref[...] = _rms(proj(C_KV, KV_LORA), gckv_ref[...]).astype(BF16)
    ki_ref[...] = _layer_norm(proj(C_KI, IDX_HEADS * IDX_DIM), gki_ref[...],
                              bki_ref[...]).astype(BF16)
    wi_ref[...] = proj(C_WI, LANES) * IDX_SCALE
    gas_ref[...] = jax.nn.sigmoid(proj(c_ga, d)).astype(BF16)

    u = jax.nn.gelu(proj(C_U, SGU_WIDTH))
    v = _layer_norm(jax.nn.gelu(proj(C_V, SGU_WIDTH)), gsgu_ref[...], bsgu_ref[...])

    row = lax.broadcasted_iota(jnp.int32, (SGU_CHUNK, 2 * SGU_CHUNK), 0)
    col = lax.broadcasted_iota(jnp.int32, (SGU_CHUNK, 2 * SGU_CHUNK), 1) % SGU_CHUNK
    causal = (row // CHUNK) >= (col // CHUNK)
    lane = lax.broadcasted_iota(jnp.int32, (SGU_CHUNK, LANES), 1)
    left = lane < SGU_GROUP_DIM
    n_tiles = SGU_WIDTH // LANES
    ws = [jnp.where(causal, wsp_ref[j], 0.0).astype(BF16) for j in range(n_tiles)]
    yb_chunks = []
    for c in range(tm // SGU_CHUNK):
        tiles = []
        for j in range(n_tiles):
            blk = v[c * SGU_CHUNK:(c + 1) * SGU_CHUNK, j * LANES:(j + 1) * LANES]
            stacked = jnp.concatenate(
                [jnp.where(left, blk, 0.0), jnp.where(left, 0.0, blk)], axis=0).astype(BF16)
            tiles.append(_dot(ws[j], stacked))
        s = jnp.concatenate(tiles, axis=1) + bsp_ref[...]
        yb_chunks.append(u[c * SGU_CHUNK:(c + 1) * SGU_CHUNK, :] * s)
    yb = jnp.concatenate(yb_chunks, axis=0).astype(BF16)
    pb_ref[...] = (jax.nn.sigmoid(proj(c_gb, d)) * _dot(yb, wbrb_ref[...])).astype(BF16)


def _inproj(x2, g_mix, w_in_p, g_cq, g_ckv, g_ki, b_ki, g_sgu, b_sgu, ws_pair, bsp, w_br_b,
            tm):
    n, d = x2.shape
    d_in_p = w_in_p.shape[1]

    def full(shape):
        return pl.BlockSpec(shape, lambda i: (0,) * len(shape))

    def rows(width):
        return pl.BlockSpec((tm, width), lambda i: (i, 0))

    return pl.pallas_call(
        _inproj_kernel,
        grid=(n // tm,),
        in_specs=[rows(d), full((1, d)), full((d, d_in_p)), full((1, Q_LORA)),
                  full((1, KV_LORA)), full((1, IDX_HEADS * IDX_DIM)),
                  full((1, IDX_HEADS * IDX_DIM)), full((1, SGU_WIDTH)), full((1, SGU_WIDTH)),
                  full(ws_pair.shape), full(bsp.shape), full(w_br_b.shape)],
        out_specs=[rows(Q_LORA), rows(KV_LORA), rows(IDX_HEADS * IDX_DIM), rows(LANES),
                   rows(d), rows(d)],
        out_shape=[jax.ShapeDtypeStruct((n, Q_LORA), BF16),
                   jax.ShapeDtypeStruct((n, KV_LORA), BF16),
                   jax.ShapeDtypeStruct((n, IDX_HEADS * IDX_DIM), BF16),
                   jax.ShapeDtypeStruct((n, LANES), F32),
                   jax.ShapeDtypeStruct((n, d), BF16),
                   jax.ShapeDtypeStruct((n, d), BF16)],
        compiler_params=pltpu.CompilerParams(dimension_semantics=("parallel",),
                                             vmem_limit_bytes=VMEM_LIMIT),
        name="inproj",
    )(x2, g_mix, w_in_p, g_cq, g_ckv, g_ki, b_ki, g_sgu, b_sgu, ws_pair, bsp, w_br_b)


def _attn_body(s_eff, topk, q_block0, cq_ref, ckv_ref, ki_ref, wi_ref, wuq_ref, wqi_ref,
               wuv_ref, ya_ref, q_scr, qi_scr, wcol_scr, sc_scr, bias_scr, acc_scr, s_buf,
               p_buf, kva_scr, sc16_scr):
    i = pl.program_id(1) + q_block0
    cq = cq_ref[0]
    q = _dot(cq, wuq_ref[...])
    for h in range(N_HEADS):
        q_scr[h] = (q[:, h * KV_LORA:(h + 1) * KV_LORA] * ATTN_SCALE).astype(BF16)

    rowq = lax.broadcasted_iota(jnp.int32, (Q_BLOCK, s_eff), 0)
    colk = lax.broadcasted_iota(jnp.int32, (Q_BLOCK, s_eff), 1)
    allowed = (colk // CHUNK) <= ((i * Q_BLOCK + rowq) // CHUNK)

    if s_eff <= topk:
        bias_scr[...] = jnp.where(allowed, 0.0, -jnp.inf)
    else:
        _select_topk(s_eff, topk, cq, allowed, colk, ki_ref, wi_ref, wqi_ref, qi_scr, wcol_scr,
                     sc_scr, bias_scr, s_buf, sc16_scr)

    acc_scr[...] = jnp.zeros_like(acc_scr)

    def logits(h):
        s_buf[h % 2] = _dot_nt(q_scr[h], ckv_ref[0, 0:s_eff, :]) + bias_scr[...]

    def softmax(h):
        s = s_buf[h % 2]
        p_buf[h % 2] = jnp.exp(s - jnp.max(s, axis=1, keepdims=True)).astype(BF16)

    def values(h):
        ol = _dot(p_buf[h % 2], kva_scr[0:s_eff, :])
        o = ol[:, :KV_LORA] / ol[:, KV_LORA:]
        acc_scr[...] += _dot(o.astype(BF16), wuv_ref[h])

    logits(0)
    for h in range(N_HEADS):
        if h + 1 < N_HEADS:
            logits(h + 1)
        softmax(h)
        values(h)
    ya_ref[0] = acc_scr[...].astype(BF16)


def _select_topk(s_eff, topk, cq, allowed, colk, ki_ref, wi_ref, wqi_ref, qi_scr, wcol_scr,
                 sc_scr, bias_scr, s_buf, sc16_scr):
    qi = _dot(cq, wqi_ref[...]).astype(BF16)
    wi = wi_ref[0]
    head_of_lane = lax.broadcasted_iota(jnp.int32, qi.shape, 1) // IDX_DIM
    for h in range(IDX_HEADS):
        qi_scr[h] = jnp.where(head_of_lane == h, qi, jnp.zeros_like(qi))
        wcol_scr[h] = wi[:, h:h + 1]

    def index_dots(h):
        s_buf[h % 2] = _dot_nt(qi_scr[h], ki_ref[0, 0:s_eff, :])

    index_dots(0)
    for h in range(IDX_HEADS):
        if h + 1 < IDX_HEADS:
            index_dots(h + 1)
        term = wcol_scr[h] * jnp.maximum(s_buf[h % 2], 0.0)
        bias_scr[...] = term if h == 0 else bias_scr[...] + term

    sc_scr[...] = jnp.where(allowed, bias_scr[...], -jnp.inf)

    def code_to_bits(code):
        return jnp.where(code < 0, code ^ jnp.int32(0x7FFFFFFF), code)

    def code_to_float(code):
        return lax.bitcast_convert_type(code_to_bits(code), F32)

    upper_half = jnp.int32(-65536)
    sc_bits = lax.bitcast_convert_type(sc_scr[...], jnp.int32)
    sc16_scr[...] = lax.bitcast_convert_type(sc_bits & upper_half, F32).astype(BF16)

    def count_ge_upper(cand):
        thr16 = lax.bitcast_convert_type(code_to_bits(cand) & upper_half, F32).astype(BF16)
        part = None
        for t in range(s_eff // LANES):
            tile = sc16_scr[:, t * LANES:(t + 1) * LANES]
            ind = jnp.where(tile >= thr16, jnp.ones((), BF16), jnp.zeros((), BF16))
            part = ind if part is None else part + ind
        return jnp.sum(part.astype(F32), axis=1, keepdims=True)

    def count_ge(cand):
        return jnp.sum(jnp.where(sc_scr[...] >= code_to_float(cand), 1.0, 0.0), axis=1,
                       keepdims=True)

    kf = float(topk)
    code = jnp.where(count_ge_upper(jnp.zeros((Q_BLOCK, 1), jnp.int32)) >= kf,
                     jnp.int32(0), jnp.int32(INT_MIN))
    for bit in range(30, -1, -1):
        cand = code + jnp.int32(1 << bit)
        cnt = count_ge_upper(cand) if bit >= 16 else count_ge(cand)
        feasible = (cnt >= kf) | (cand <= jnp.int32(CODE_NEG_INF))
        code = jnp.where(feasible, cand, code)

    thr = code_to_float(code)
    thr_up = code_to_float(code + 1)
    sel = sc_scr[...] >= thr
    n_ge = jnp.sum(jnp.where(sel, 1.0, 0.0), axis=1, keepdims=True)
    bias_scr[...] = jnp.where(sel & allowed, 0.0, -jnp.inf)
    tie_rows = (n_ge > kf) & (code > jnp.int32(CODE_NEG_INF))
    any_tie = jnp.max(jnp.where(tie_rows, 1.0, 0.0))

    @pl.when(any_tie > 0.0)
    def _():
        sc = sc_scr[...]
        gt = sc >= thr_up
        eq = (sc >= thr) & jnp.logical_not(gt)
        need = kf - jnp.sum(jnp.where(gt, 1.0, 0.0), axis=1, keepdims=True)
        nbits = (s_eff - 1).bit_length()

        def tie_step(it, j):
            cand = j + jnp.left_shift(jnp.int32(1), jnp.int32(nbits - 1) - it)
            v = sc_scr[...]
            tie = (v >= thr) & jnp.logical_not(v >= thr_up) & (colk < cand)
            cnt = jnp.sum(jnp.where(tie, 1.0, 0.0), axis=1, keepdims=True)
            return jnp.where(cnt < need, cand, j)

        jmax = lax.fori_loop(0, nbits, tie_step, jnp.zeros((Q_BLOCK, 1), jnp.int32))
        keep = gt | (eq & (colk <= jmax))
        bias_scr[...] = jnp.where(keep & allowed, 0.0, -jnp.inf)


def _attn_kernel(s_eff, topk, q_block0, cq_ref, ckv_ref, ki_ref, wi_ref, wuq_ref, wqi_ref,
                 wuv_ref, ya_ref, q_scr, qi_scr, wcol_scr, sc_scr, bias_scr, acc_scr, s_buf,
                 p_buf, kva_scr, sc16_scr):
    @pl.when(pl.program_id(1) == 0)
    def _():
        kva_scr[:, 0:KV_LORA] = ckv_ref[0]
        kva_scr[:, KV_LORA:] = jnp.ones((s_eff, KV_LORA), BF16)

    _attn_body(s_eff, topk, q_block0, cq_ref, ckv_ref, ki_ref, wi_ref, wuq_ref, wqi_ref,
               wuv_ref, ya_ref, q_scr, qi_scr, wcol_scr, sc_scr, bias_scr, acc_scr, s_buf,
               p_buf, kva_scr, sc16_scr)


def _attn(cq, ckv, ki, wi, w_uq, w_qi, w_uv_pad, n_buckets):
    b, s, _ = cq.shape
    nq = s // Q_BLOCK
    per = nq // n_buckets
    topk = min(TOPK_MAX, s // 4)
    aw = N_HEADS * HEAD_DIM

    def full(shape):
        return pl.BlockSpec(shape, lambda bi, i: (0,) * len(shape))

    parts = []
    for k in range(n_buckets):
        s_eff = (k + 1) * per * Q_BLOCK
        q0 = k * per

        def q_rows(width, q0=q0):
            return pl.BlockSpec((1, Q_BLOCK, width), lambda bi, i: (bi, i + q0, 0))

        def keys(width, s_eff=s_eff):
            return pl.BlockSpec((1, s_eff, width), lambda bi, i: (bi, 0, 0))

        parts.append(pl.pallas_call(
            functools.partial(_attn_kernel, s_eff, topk, q0),
            grid=(b, per),
            in_specs=[q_rows(Q_LORA), keys(KV_LORA), keys(IDX_HEADS * IDX_DIM), q_rows(LANES),
                      full(w_uq.shape), full(w_qi.shape), full(w_uv_pad.shape)],
            out_specs=pl.BlockSpec((1, Q_BLOCK, aw), lambda bi, i: (bi, i, 0)),
            out_shape=jax.ShapeDtypeStruct((b, per * Q_BLOCK, aw), BF16),
            scratch_shapes=[pltpu.VMEM((N_HEADS, Q_BLOCK, KV_LORA), BF16),
                            pltpu.VMEM((IDX_HEADS, Q_BLOCK, IDX_HEADS * IDX_DIM), BF16),
                            pltpu.VMEM((IDX_HEADS, Q_BLOCK, 1), F32),
                            pltpu.VMEM((Q_BLOCK, s_eff), F32),
                            pltpu.VMEM((Q_BLOCK, s_eff), F32),
                            pltpu.VMEM((Q_BLOCK, aw), F32),
                            pltpu.VMEM((2, Q_BLOCK, s_eff), F32),
                            pltpu.VMEM((2, Q_BLOCK, s_eff), BF16),
                            pltpu.VMEM((s_eff, 2 * KV_LORA), BF16),
                            pltpu.VMEM((Q_BLOCK, s_eff), BF16)],
            compiler_params=pltpu.CompilerParams(
                dimension_semantics=("parallel", "arbitrary"), vmem_limit_bytes=VMEM_LIMIT),
            name=f"attn_keys{s_eff}",
        )(cq, ckv, ki, wi, w_uq, w_qi, w_uv_pad))
    return jnp.concatenate(parts, axis=1)


def _merge_kernel(x_ref, ya_ref, gas_ref, pb_ref, wbra_ref, wo_ref, gffn_ref,
                  wr_hi_ref, wr_lo_ref, br_ref,
                  x1_ref, h2p_ref, idx_ref, gate_ref, rank_ref, cnt_ref, base_scr):
    step = pl.program_id(0)
    tm, d = x_ref.shape

    @pl.when(step == 0)
    def _():
        base_scr[...] = jnp.zeros_like(base_scr)

    a = _dot(ya_ref[...], wbra_ref[...])
    merged = gas_ref[...].astype(F32) * a + pb_ref[...].astype(F32)
    x1 = x_ref[...] + _dot(merged.astype(BF16), wo_ref[...])
    x1_ref[...] = x1
    h2 = _rms(x1, gffn_ref[...])

    h_hi = h2.astype(BF16)
    h2p_ref[...] = _pack_bf16_pairs(h2)
    h_lo = (h2 - h_hi.astype(F32)).astype(BF16)
    logits = (_dot(h_hi, wr_hi_ref[...]) + _dot(h_hi, wr_lo_ref[...])
              + _dot(h_lo, wr_hi_ref[...]) + br_ref[...])

    lane_e = lax.broadcasted_iota(jnp.int32, (tm, N_EXPERTS), 1).astype(F32)
    lane_o = lax.broadcasted_iota(jnp.int32, (tm, LANES), 1)
    work = logits
    vals, idxs = [], []
    onehot = jnp.zeros((tm, N_EXPERTS), F32)
    for _ in range(TOP_K):
        m = jnp.max(work, axis=1, keepdims=True)
        idx = jnp.min(jnp.where(work == m, lane_e, float(N_EXPERTS)), axis=1, keepdims=True)
        hit = lane_e == idx
        onehot = onehot + jnp.where(hit, 1.0, 0.0)
        work = jnp.where(hit, -jnp.inf, work)
        vals.append(m)
        idxs.append(idx)
    exps = [jnp.exp(v - vals[0]) for v in vals]
    denom = exps[0] + exps[1] + exps[2] + exps[3]

    r = lax.broadcasted_iota(jnp.int32, (tm, tm), 0)
    c = lax.broadcasted_iota(jnp.int32, (tm, tm), 1)
    tri = jnp.where(c < r, 1.0, 0.0).astype(BF16)
    rank_full = _dot(tri, onehot.astype(BF16)) + base_scr[...]

    idx_out = jnp.zeros((tm, LANES), F32)
    gate_out = jnp.zeros((tm, LANES), F32)
    rank_out = jnp.zeros((tm, LANES), F32)
    for k in range(TOP_K):
        rk = jnp.sum(jnp.where(lane_e == idxs[k], rank_full, 0.0), axis=1, keepdims=True)
        idx_out = jnp.where(lane_o == k, idxs[k], idx_out)
        gate_out = jnp.where(lane_o == k, exps[k] / denom, gate_out)
        rank_out = jnp.where(lane_o == k, rk, rank_out)
    gate_ref[...] = gate_out
    idx_ref[...] = idx_out.T[0:SUBLANES, :].astype(jnp.int32)
    rank_ref[...] = rank_out.T[0:SUBLANES, :].astype(jnp.int32)

    base_scr[...] = base_scr[...] + jnp.sum(onehot, axis=0, keepdims=True)
    cnt_ref[...] = base_scr[...].astype(jnp.int32)


def _merge(x2, ya, gas, pb, w_br_a, w_o, g_ffn, wr_hi, wr_lo, b_router, tm):
    n, d = x2.shape
    aw = ya.shape[1]

    def full(shape):
        return pl.BlockSpec(shape, lambda i: (0,) * len(shape))

    def rows(width):
        return pl.BlockSpec((tm, width), lambda i: (i, 0))

    slots = pl.BlockSpec((SUBLANES, tm), lambda i: (0, i))
    return pl.pallas_call(
        _merge_kernel,
        grid=(n // tm,),
        in_specs=[rows(d), rows(aw), rows(d), rows(d), full(w_br_a.shape), full(w_o.shape),
                  full((1, d)), full(wr_hi.shape), full(wr_lo.shape), full((1, N_EXPERTS))],
        out_specs=[rows(d), rows(d // 2), slots, rows(LANES), slots, full((1, N_EXPERTS))],
        out_shape=[jax.ShapeDtypeStruct((n, d), F32),
                   jax.ShapeDtypeStruct((n, d // 2), jnp.int32),
                   jax.ShapeDtypeStruct((SUBLANES, n), jnp.int32),
                   jax.ShapeDtypeStruct((n, LANES), F32),
                   jax.ShapeDtypeStruct((SUBLANES, n), jnp.int32),
                   jax.ShapeDtypeStruct((1, N_EXPERTS), jnp.int32)],
        scratch_shapes=[pltpu.VMEM((1, N_EXPERTS), F32)],
        compiler_params=pltpu.CompilerParams(dimension_semantics=("arbitrary",),
                                             vmem_limit_bytes=VMEM_LIMIT),
        name="merge_route",
    )(x2, ya, gas, pb, w_br_a, w_o, g_ffn, wr_hi, wr_lo, b_router)


def _sc_mesh():
    info = plsc.get_sparse_core_info()
    mesh = plsc.VectorSubcoreMesh(core_axis_name="c", subcore_axis_name="s")
    return mesh, info.num_cores, info.num_cores * info.num_subcores


def _sc_dispatch(src, dest_t, pad_idx, n_out_rows):
    n, w = src.shape
    mesh, n_cores, n_workers = _sc_mesh()
    per_w = n // n_workers
    n_chunks = per_w // SC_ROWS
    n_pad_chunks = pad_idx.size // (n_workers * SC_ROWS)

    @functools.partial(
        pl.kernel, mesh=mesh,
        out_type=jax.ShapeDtypeStruct((n_out_rows, w), jnp.int32),
        scratch_types=[pltpu.VMEM((n_chunks, TOP_K, SC_ROWS), jnp.int32),
                       pltpu.VMEM((n_pad_chunks, SC_ROWS), jnp.int32),
                       pltpu.VMEM((SC_ROWS, w), jnp.int32),
                       pltpu.VMEM((SC_ROWS, w), jnp.int32)],
        name="sc_dispatch",
    )
    def run(src_hbm, dest_hbm, pad_hbm, zeros_hbm, out_hbm, idx_v, pad_v, rows_v, zero_v):
        wid = lax.axis_index("s") * n_cores + lax.axis_index("c")
        pltpu.sync_copy(dest_hbm.at[wid], idx_v)
        pltpu.sync_copy(pad_hbm.at[wid], pad_v)
        pltpu.sync_copy(zeros_hbm, zero_v)

        @pl.loop(0, n_pad_chunks)
        def _(c):
            pltpu.sync_copy(zero_v, out_hbm.at[pad_v.at[c]])

        @pl.loop(0, n_chunks)
        def _(c):
            pltpu.sync_copy(src_hbm.at[pl.ds(wid * per_w + c * SC_ROWS, SC_ROWS)], rows_v)
            for k in range(TOP_K):
                pltpu.sync_copy(rows_v, out_hbm.at[idx_v.at[c, k]])

    dest_w = dest_t.reshape(TOP_K, n_workers, n_chunks, SC_ROWS).transpose(1, 2, 0, 3)
    return run(src, dest_w, pad_idx.reshape(n_workers, n_pad_chunks, SC_ROWS),
               jnp.zeros((SC_ROWS, w), jnp.int32))


def _sc_gather(table, idx):
    n_rows = idx.shape[0]
    w = table.shape[1]
    mesh, n_cores, n_workers = _sc_mesh()
    per_w = n_rows // n_workers
    n_chunks = per_w // SC_ROWS
    assert n_chunks % 2 == 0 and n_chunks * SC_ROWS * n_workers == n_rows

    @functools.partial(
        pl.kernel, mesh=mesh,
        out_type=jax.ShapeDtypeStruct((n_rows, w), table.dtype),
        scratch_types=[pltpu.VMEM((n_chunks, SC_ROWS), jnp.int32),
                       pltpu.VMEM((SC_ROWS, w), table.dtype),
                       pltpu.VMEM((SC_ROWS, w), table.dtype),
                       pltpu.SemaphoreType.DMA, pltpu.SemaphoreType.DMA],
        name="sc_gather",
    )
    def run(table_hbm, idx_hbm, out_hbm, idx_v, buf0, buf1, sem0, sem1):
        wid = lax.axis_index("s") * n_cores + lax.axis_index("c")
        base = wid * per_w
        pltpu.sync_copy(idx_hbm.at[wid], idx_v)

        def gather(c, buf, sem):
            return pltpu.make_async_copy(table_hbm.at[idx_v.at[c]], buf, sem)

        gather(0, buf0, sem0).start()

        @pl.loop(0, n_chunks, step=2)
        def _(c):
            gather(c + 1, buf1, sem1).start()
            gather(c, buf0, sem0).wait()
            pltpu.sync_copy(buf0, out_hbm.at[pl.ds(base + c * SC_ROWS, SC_ROWS)])

            @pl.when(c + 2 < n_chunks)
            def _():
                gather(c + 2, buf0, sem0).start()

            gather(c + 1, buf1, sem1).wait()
            pltpu.sync_copy(buf1, out_hbm.at[pl.ds(base + (c + 1) * SC_ROWS, SC_ROWS)])

    return run(table, idx.reshape(n_workers, n_chunks, SC_ROWS))


def _expert_kernel(be_ref, nu_ref, xs_ref, wgu_ref, bgu_ref, wd_ref, bd_ref, y_ref,
                   wgu_bf, wd_bf):
    i = pl.program_id(0)
    f = wd_ref.shape[1]
    half = xs_ref.shape[1]
    live = i < nu_ref[0]

    @pl.when(live & ((i == 0) | (be_ref[i] != be_ref[jnp.maximum(i - 1, 0)])))
    def _():
        wgu_bf[...] = wgu_ref[0].astype(BF16)
        wd_bf[...] = wd_ref[0].astype(BF16)

    @pl.when(live)
    def _():
        x_lo, x_hi = _unpack_bf16_pairs(xs_ref[...])
        gu = (_dot(x_lo.astype(BF16), wgu_bf[0:half, :])
              + _dot(x_hi.astype(BF16), wgu_bf[half:2 * half, :])
              + bgu_ref[0])
        gate = jnp.minimum(gu[:, :f], SWIGLU_LIMIT)
        up = jnp.clip(gu[:, f:], -SWIGLU_LIMIT, SWIGLU_LIMIT)
        act = (up + 1.0) * (gate * jax.nn.sigmoid(SWIGLU_ALPHA * gate))
        y = _dot(act.astype(BF16), wd_bf[...]) + bd_ref[0]
        y_ref[...] = _pack_bf16_pairs(y)


def _experts(block_e, n_used, xs, n_blocks, w_gu, b_gu, w_down, b_down):
    half = xs.shape[1]
    d = 2 * half
    f2 = w_gu.shape[2]
    f = w_down.shape[1]

    def blk(i, be, nu):
        return (jnp.minimum(i, nu[0] - 1), 0)

    def expert(i, be, nu):
        return (be[jnp.minimum(i, nu[0] - 1)], 0, 0)

    grid_spec = pltpu.PrefetchScalarGridSpec(
        num_scalar_prefetch=2,
        grid=(n_blocks,),
        in_specs=[pl.BlockSpec((EXPERT_BLOCK, half), blk),
                  pl.BlockSpec((1, d, f2), expert),
                  pl.BlockSpec((1, 1, f2), expert),
                  pl.BlockSpec((1, f, d), expert),
                  pl.BlockSpec((1, 1, d), expert)],
        out_specs=pl.BlockSpec((EXPERT_BLOCK, half), blk),
        scratch_shapes=[pltpu.VMEM((d, f2), BF16), pltpu.VMEM((f, d), BF16)],
    )
    return pl.pallas_call(
        _expert_kernel,
        grid_spec=grid_spec,
        out_shape=jax.ShapeDtypeStruct((n_blocks * EXPERT_BLOCK, half), jnp.int32),
        compiler_params=pltpu.CompilerParams(dimension_semantics=("arbitrary",),
                                             vmem_limit_bytes=VMEM_LIMIT),
        name="experts",
    )(block_e, n_used, xs, w_gu, b_gu, w_down, b_down)


def _combine_kernel(yg_ref, x1_ref, gate_ref, gfin_ref, out_ref):
    gates = gate_ref[...]
    half = yg_ref.shape[2]
    acc_lo = x1_ref[:, :half]
    acc_hi = x1_ref[:, half:]
    for k in range(TOP_K):
        y_lo, y_hi = _unpack_bf16_pairs(yg_ref[k])
        acc_lo = acc_lo + gates[:, k:k + 1] * y_lo
        acc_hi = acc_hi + gates[:, k:k + 1] * y_hi
    out_ref[...] = _rms(jnp.concatenate([acc_lo, acc_hi], axis=1), gfin_ref[...])


def _combine(yg, x1, gates, g_final, tm):
    n, d = x1.shape
    return pl.pallas_call(
        _combine_kernel,
        grid=(n // tm,),
        in_specs=[pl.BlockSpec((TOP_K, tm, d // 2), lambda i: (0, i, 0)),
                  pl.BlockSpec((tm, d), lambda i: (i, 0)),
                  pl.BlockSpec((tm, LANES), lambda i: (i, 0)),
                  pl.BlockSpec((1, d), lambda i: (0, 0))],
        out_specs=pl.BlockSpec((tm, d), lambda i: (i, 0)),
        out_shape=jax.ShapeDtypeStruct((n, d), F32),
        compiler_params=pltpu.CompilerParams(dimension_semantics=("parallel",),
                                             vmem_limit_bytes=VMEM_LIMIT),
        name="combine",
    )(yg, x1, gates, g_final)


def _pack_in_proj(w_in, d):
    offs = [0]
    for width in (Q_LORA, KV_LORA, IDX_DIM, IDX_HEADS, SGU_WIDTH, SGU_WIDTH, d, d):
        offs.append(offs[-1] + width)
    wq, wkv, wki, wwi, wu, wv, wga, wgb = [w_in[:, offs[j]:offs[j + 1]] for j in range(8)]
    wki_rep = jnp.tile(wki, (1, IDX_HEADS))
    wwi_pad = jnp.pad(wwi, ((0, 0), (0, LANES - IDX_HEADS)))
    return jnp.concatenate([wq, wkv, wki_rep, wwi_pad, wu, wv, wga, wgb], axis=1).astype(BF16)


def _layer(x, g_mix, w_in, g_cq, g_ckv, w_uq, w_uv, w_q_idx, g_kidx, b_kidx, g_sgu, b_sgu,
           w_spatial, b_spatial, w_br_a, w_br_b, w_o, g_ffn, w_router, b_router, w_gu, b_gu,
           w_down, b_down, g_final):
    b, s, d = x.shape
    n = b * s
    x2 = x.reshape(n, d)
    row = lambda v: v.reshape(1, -1).astype(F32)

    w_in_p = _pack_in_proj(w_in, d)
    ws_pair = w_spatial.reshape(SGU_GROUPS // 2, 2, SGU_CHUNK, SGU_CHUNK).transpose(
        0, 2, 1, 3).reshape(SGU_GROUPS // 2, SGU_CHUNK, 2 * SGU_CHUNK)
    bsp = jnp.repeat(b_spatial.T, SGU_GROUP_DIM, axis=1)
    cq, ckv, ki, wi, gas, pb = _inproj(
        x2, row(g_mix), w_in_p, row(g_cq), row(g_ckv), row(jnp.tile(g_kidx, IDX_HEADS)),
        row(jnp.tile(b_kidx, IDX_HEADS)), row(g_sgu), row(b_sgu), ws_pair, bsp,
        w_br_b.astype(BF16), tm=512)

    head_eye = jnp.eye(N_HEADS, dtype=F32)
    w_uv_pad = (w_uv[:, :, None, :] * head_eye[:, None, :, None]).reshape(
        N_HEADS, KV_LORA, N_HEADS * HEAD_DIM)
    n_buckets = next(nbk for nbk in (8, 4, 2, 1) if (s // Q_BLOCK) % nbk == 0)
    ya = _attn(cq.reshape(b, s, -1), ckv.reshape(b, s, -1), ki.reshape(b, s, -1),
               wi.reshape(b, s, -1), w_uq.astype(BF16), w_q_idx.astype(BF16),
               w_uv_pad.astype(BF16), n_buckets)

    wr_hi = w_router.astype(BF16)
    wr_lo = (w_router - wr_hi.astype(F32)).astype(BF16)
    x1, h2p, idx_t, gate_p, rank_t, counts = _merge(
        x2, ya.reshape(n, -1), gas, pb, w_br_a.astype(BF16), w_o.astype(BF16), row(g_ffn),
        wr_hi, wr_lo, row(b_router), tm=512)

    counts = counts[0]
    padded = (counts + EXPERT_BLOCK - 1) // EXPERT_BLOCK * EXPERT_BLOCK
    padded_end = jnp.cumsum(padded)
    padded_start = padded_end - padded
    expert_ids = jnp.arange(N_EXPERTS, dtype=jnp.int32)[:, None, None]
    group_start = jnp.sum(jnp.where(idx_t[None, :TOP_K] == expert_ids,
                                    padded_start[:, None, None], 0), axis=0)
    dest_t = group_start + rank_t[:TOP_K]
    nk = n * TOP_K
    nb = -(-(nk + N_EXPERTS * EXPERT_BLOCK) // EXPERT_BLOCK)
    block_row0 = jnp.arange(nb, dtype=jnp.int32) * EXPERT_BLOCK
    block_e = jnp.minimum(
        jnp.sum((padded_end[None, :] <= block_row0[:, None]).astype(jnp.int32), axis=1),
        N_EXPERTS - 1).astype(jnp.int32)
    n_used = (padded_end[-1:] // EXPERT_BLOCK).astype(jnp.int32)
    j = jnp.arange(EXPERT_BLOCK, dtype=jnp.int32)[None, :]
    pad_idx = jnp.where(j < (padded - counts)[:, None], (padded_start + counts)[:, None] + j,
                        nb * EXPERT_BLOCK + j).astype(jnp.int32)

    xs = _sc_dispatch(h2p, dest_t, pad_idx, (nb + 1) * EXPERT_BLOCK)
    ybuf = _experts(block_e, n_used, xs, nb, w_gu, b_gu.reshape(N_EXPERTS, 1, -1), w_down,
                    b_down.reshape(N_EXPERTS, 1, -1))
    yg = _sc_gather(ybuf, dest_t.reshape(-1))
    out = _combine(yg.reshape(TOP_K, n, d // 2), x1, gate_p, row(g_final), tm=512)
    return out.reshape(b, s, d)


def kernel(x, g_mix, w_in, g_cq, g_ckv, w_uq, w_uv, w_q_idx, g_kidx, b_kidx, g_sgu, b_sgu,
           w_spatial, b_spatial, w_br_a, w_br_b, w_o, g_ffn, w_router, b_router, w_gu, b_gu,
           w_down, b_down, g_final):
    assert g_mix.shape[0] == 1, "single-layer block"
    return _layer(x, g_mix[0], w_in[0], g_cq[0], g_ckv[0], w_uq[0], w_uv[0], w_q_idx[0],
                  g_kidx[0], b_kidx[0], g_sgu[0], b_sgu[0], w_spatial[0], b_spatial[0],
                  w_br_a[0], w_br_b[0], w_o[0], g_ffn[0], w_router[0], b_router[0], w_gu[0],
                  b_gu[0], w_down[0], b_down[0], g_final)
```

```python
import functools

import jax
import jax.numpy as jnp
from jax import lax
from jax.experimental import pallas as pl
from jax.experimental.pallas import tpu as pltpu
from jax.experimental.pallas import tpu_sc as plsc

EPS = 1e-6
CHUNK = 64
N_HEADS = 8
HEAD_DIM = 64
Q_LORA = 256
KV_LORA = 128
IDX_HEADS = 8
IDX_DIM = 32
TOPK_MAX = 256
Q_BLOCK = 256
ATTN_SCALE = KV_LORA ** -0.5
IDX_SCALE = (IDX_HEADS * IDX_DIM) ** -0.5
SGU_CHUNK = 128
SGU_GROUPS = 8
SGU_WIDTH = 512
SGU_GROUP_DIM = SGU_WIDTH // SGU_GROUPS
N_EXPERTS = 32
TOP_K = 4
SWIGLU_LIMIT = 7.0
SWIGLU_ALPHA = 1.702
EXPERT_BLOCK = 512

COMBINE_PARTS = 4
SC_ROWS = 32
LANES = 128
SUBLANES = 8
VMEM_LIMIT = 56 * 1024 * 1024
INT_MIN = -(2 ** 31)
CODE_NEG_INF = INT_MIN + 0x7FFFFF

BF16 = jnp.bfloat16
F32 = jnp.float32


def _dot(a, b):
    return jnp.dot(a, b, preferred_element_type=F32)


def _dot_nt(a, b):
    return lax.dot_general(a, b, (((1,), (1,)), ((), ())), preferred_element_type=F32)


def _rms(x, g):
    return x * lax.rsqrt(jnp.mean(x * x, axis=-1, keepdims=True) + EPS) * g


def _pack_bf16_pairs(x):
    half = x.shape[1] // 2
    bits = lax.bitcast_convert_type(x.astype(BF16).astype(F32), jnp.int32)
    return lax.shift_right_logical(bits[:, :half], 16) | (bits[:, half:] & jnp.int32(-65536))


def _unpack_bf16_pairs(words):
    lo = lax.bitcast_convert_type(lax.shift_left(words, 16), F32)
    hi = lax.bitcast_convert_type(words & jnp.int32(-65536), F32)
    return lo, hi


def _layer_norm(x, g, b):
    mu = jnp.mean(x, axis=-1, keepdims=True)
    xc = x - mu
    var = jnp.mean(xc * xc, axis=-1, keepdims=True)
    return xc * lax.rsqrt(var + EPS) * g + b


C_Q = 0
C_KV = C_Q + Q_LORA
C_KI = C_KV + KV_LORA
C_WI = C_KI + IDX_HEADS * IDX_DIM
C_U = C_WI + LANES
C_V = C_U + SGU_WIDTH
D_IN_P_BASE = C_V + SGU_WIDTH


def _inproj_kernel(x_ref, gmix_ref, w_ref, gcq_ref, gckv_ref, gki_ref, bki_ref,
                   gsgu_ref, bsgu_ref, wsp_ref, bsp_ref, wbrb_ref,
                   cq_ref, ckv_ref, ki_ref, wi_ref, gas_ref, pb_ref):
    tm, d = x_ref.shape
    c_ga = D_IN_P_BASE
    c_gb = c_ga + d
    h = _rms(x_ref[...], gmix_ref[...]).astype(BF16)

    def proj(lo, width):
        return _dot(h, w_ref[:, lo:lo + width])

    cq_ref[...] = _rms(proj(C_Q, Q_LORA), gcq_ref[...]).astype(BF16)
    ckv_ref[...] = _rms(proj(C_KV, KV_LORA), gckv_ref[...]).astype(BF16)
    ki_ref[...] = _layer_norm(proj(C_KI, IDX_HEADS * IDX_DIM), gki_ref[...],
                              bki_ref[...]).astype(BF16)
    wi_ref[...] = proj(C_WI, LANES) * IDX_SCALE
    gas_ref[...] = jax.nn.sigmoid(proj(c_ga, d)).astype(BF16)

    u = jax.nn.gelu(proj(C_U, SGU_WIDTH))
    v = _layer_norm(jax.nn.gelu(proj(C_V, SGU_WIDTH)), gsgu_ref[...], bsgu_ref[...])

    row = lax.broadcasted_iota(jnp.int32, (SGU_CHUNK, 2 * SGU_CHUNK), 0)
    col = lax.broadcasted_iota(jnp.int32, (SGU_CHUNK, 2 * SGU_CHUNK), 1) % SGU_CHUNK
    causal = (row // CHUNK) >= (col // CHUNK)
    lane = lax.broadcasted_iota(jnp.int32, (SGU_CHUNK, LANES), 1)
    left = lane < SGU_GROUP_DIM
    n_tiles = SGU_WIDTH // LANES
    ws = [jnp.where(causal, wsp_ref[j], 0.0).astype(BF16) for j in range(n_tiles)]
    yb_chunks = []
    for c in range(tm // SGU_CHUNK):
        tiles = []
        for j in range(n_tiles):
            blk = v[c * SGU_CHUNK:(c + 1) * SGU_CHUNK, j * LANES:(j + 1) * LANES]
            stacked = jnp.concatenate(
                [jnp.where(left, blk, 0.0), jnp.where(left, 0.0, blk)], axis=0).astype(BF16)
            tiles.append(_dot(ws[j], stacked))
        s = jnp.concatenate(tiles, axis=1) + bsp_ref[...]
        yb_chunks.append(u[c * SGU_CHUNK:(c + 1) * SGU_CHUNK, :] * s)
    yb = jnp.concatenate(yb_chunks, axis=0).astype(BF16)
    pb_ref[...] = (jax.nn.sigmoid(proj(c_gb, d)) * _dot(yb, wbrb_ref[...])).astype(BF16)


def _inproj(x2, g_mix, w_in_p, g_cq, g_ckv, g_ki, b_ki, g_sgu, b_sgu, ws_pair, bsp, w_br_b,
            tm):
    n, d = x2.shape
    d_in_p = w_in_p.shape[1]

    def full(shape):
        return pl.BlockSpec(shape, lambda i: (0,) * len(shape))

    def rows(width):
        return pl.BlockSpec((tm, width), lambda i: (i, 0))

    return pl.pallas_call(
        _inproj_kernel,
        grid=(n // tm,),
        in_specs=[rows(d), full((1, d)), full((d, d_in_p)), full((1, Q_LORA)),
                  full((1, KV_LORA)), full((1, IDX_HEADS * IDX_DIM)),
                  full((1, IDX_HEADS * IDX_DIM)), full((1, SGU_WIDTH)), full((1, SGU_WIDTH)),
                  full(ws_pair.shape), full(bsp.shape), full(w_br_b.shape)],
        out_specs=[rows(Q_LORA), rows(KV_LORA), rows(IDX_HEADS * IDX_DIM), rows(LANES),
                   rows(d), rows(d)],
        out_shape=[jax.ShapeDtypeStruct((n, Q_LORA), BF16),
                   jax.ShapeDtypeStruct((n, KV_LORA), BF16),
                   jax.ShapeDtypeStruct((n, IDX_HEADS * IDX_DIM), BF16),
                   jax.ShapeDtypeStruct((n, LANES), F32),
                   jax.ShapeDtypeStruct((n, d), BF16),
                   jax.ShapeDtypeStruct((n, d), BF16)],
        compiler_params=pltpu.CompilerParams(dimension_semantics=("parallel",),
                                             vmem_limit_bytes=VMEM_LIMIT),
        name="inproj",
    )(x2, g_mix, w_in_p, g_cq, g_ckv, g_ki, b_ki, g_sgu, b_sgu, ws_pair, bsp, w_br_b)


def _attn_body(s_eff, topk, q_block0, cq_ref, ckv_ref, ki_ref, wi_ref, wuq_ref, wqi_ref,
               wuv_ref, ya_ref, q_scr, qi_scr, wcol_scr, sc_scr, bias_scr, acc_scr, s_buf,
               p_buf, kva_scr, sc16_scr):
    i = pl.program_id(1) + q_block0
    cq = cq_ref[0]
    q = _dot(cq, wuq_ref[...])
    for h in range(N_HEADS):
        q_scr[h] = (q[:, h * KV_LORA:(h + 1) * KV_LORA] * ATTN_SCALE).astype(BF16)

    rowq = lax.broadcasted_iota(jnp.int32, (Q_BLOCK, s_eff), 0)
    colk = lax.broadcasted_iota(jnp.int32, (Q_BLOCK, s_eff), 1)
    allowed = (colk // CHUNK) <= ((i * Q_BLOCK + rowq) // CHUNK)

    if s_eff <= topk:
        bias_scr[...] = jnp.where(allowed, 0.0, -jnp.inf)
    else:
        _select_topk(s_eff, topk, cq, allowed, colk, ki_ref, wi_ref, wqi_ref, qi_scr, wcol_scr,
                     sc_scr, bias_scr, s_buf, sc16_scr)

    acc_scr[...] = jnp.zeros_like(acc_scr)

    def logits(h):
        s_buf[h % 2] = _dot_nt(q_scr[h], ckv_ref[0, 0:s_eff, :]) + bias_scr[...]

    def softmax(h):
        s = s_buf[h % 2]
        p_buf[h % 2] = jnp.exp(s - jnp.max(s, axis=1, keepdims=True)).astype(BF16)

    def values(h):
        ol = _dot(p_buf[h % 2], kva_scr[0:s_eff, :])
        o = ol[:, :KV_LORA] / ol[:, KV_LORA:]
        acc_scr[...] += _dot(o.astype(BF16), wuv_ref[h])

    logits(0)
    for h in range(N_HEADS):
        if h + 1 < N_HEADS:
            logits(h + 1)
        softmax(h)
        values(h)
    ya_ref[0] = acc_scr[...].astype(BF16)


def _select_topk(s_eff, topk, cq, allowed, colk, ki_ref, wi_ref, wqi_ref, qi_scr, wcol_scr,
                 sc_scr, bias_scr, s_buf, sc16_scr):
    qi = _dot(cq, wqi_ref[...]).astype(BF16)
    wi = wi_ref[0]
    head_of_lane = lax.broadcasted_iota(jnp.int32, qi.shape, 1) // IDX_DIM
    for h in range(IDX_HEADS):
        qi_scr[h] = jnp.where(head_of_lane == h, qi, jnp.zeros_like(qi))
        wcol_scr[h] = wi[:, h:h + 1]

    def index_dots(h):
        s_buf[h % 2] = _dot_nt(qi_scr[h], ki_ref[0, 0:s_eff, :])

    index_dots(0)
    for h in range(IDX_HEADS):
        if h + 1 < IDX_HEADS:
            index_dots(h + 1)
        term = wcol_scr[h] * jnp.maximum(s_buf[h % 2], 0.0)
        bias_scr[...] = term if h == 0 else bias_scr[...] + term

    sc_scr[...] = jnp.where(allowed, bias_scr[...], -jnp.inf)

    def code_to_bits(code):
        return jnp.where(code < 0, code ^ jnp.int32(0x7FFFFFFF), code)

    def code_to_float(code):
        return lax.bitcast_convert_type(code_to_bits(code), F32)

    upper_half = jnp.int32(-65536)
    sc_bits = lax.bitcast_convert_type(sc_scr[...], jnp.int32)
    sc16_scr[...] = lax.bitcast_convert_type(sc_bits & upper_half, F32).astype(BF16)

    def count_ge_upper(cand):
        thr16 = lax.bitcast_convert_type(code_to_bits(cand) & upper_half, F32).astype(BF16)
        part = None
        for t in range(s_eff // LANES):
            tile = sc16_scr[:, t * LANES:(t + 1) * LANES]
            ind = jnp.where(tile >= thr16, jnp.ones((), BF16), jnp.zeros((), BF16))
            part = ind if part is None else part + ind
        return jnp.sum(part.astype(F32), axis=1, keepdims=True)

    def count_ge(cand):
        return jnp.sum(jnp.where(sc_scr[...] >= code_to_float(cand), 1.0, 0.0), axis=1,
                       keepdims=True)

    kf = float(topk)
    code = jnp.where(count_ge_upper(jnp.zeros((Q_BLOCK, 1), jnp.int32)) >= kf,
                     jnp.int32(0), jnp.int32(INT_MIN))
    for bit in range(30, -1, -1):
        cand = code + jnp.int32(1 << bit)
        cnt = count_ge_upper(cand) if bit >= 16 else count_ge(cand)
        feasible = (cnt >= kf) | (cand <= jnp.int32(CODE_NEG_INF))
        code = jnp.where(feasible, cand, code)

    thr = code_to_float(code)
    thr_up = code_to_float(code + 1)
    sel = sc_scr[...] >= thr
    n_ge = jnp.sum(jnp.where(sel, 1.0, 0.0), axis=1, keepdims=True)
    bias_scr[...] = jnp.where(sel & allowed, 0.0, -jnp.inf)
    tie_rows = (n_ge > kf) & (code > jnp.int32(CODE_NEG_INF))
    any_tie = jnp.max(jnp.where(tie_rows, 1.0, 0.0))

    @pl.when(any_tie > 0.0)
    def _():
        sc = sc_scr[...]
        gt = sc >= thr_up
        eq = (sc >= thr) & jnp.logical_not(gt)
        need = kf - jnp.sum(jnp.where(gt, 1.0, 0.0), axis=1, keepdims=True)
        nbits = (s_eff - 1).bit_length()

        def tie_step(it, j):
            cand = j + jnp.left_shift(jnp.int32(1), jnp.int32(nbits - 1) - it)
            v = sc_scr[...]
            tie = (v >= thr) & jnp.logical_not(v >= thr_up) & (colk < cand)
            cnt = jnp.sum(jnp.where(tie, 1.0, 0.0), axis=1, keepdims=True)
            return jnp.where(cnt < need, cand, j)

        jmax = lax.fori_loop(0, nbits, tie_step, jnp.zeros((Q_BLOCK, 1), jnp.int32))
        keep = gt | (eq & (colk <= jmax))
        bias_scr[...] = jnp.where(keep & allowed, 0.0, -jnp.inf)


def _attn_kernel(s_eff, topk, q_block0, cq_ref, ckv_ref, ki_ref, wi_ref, wuq_ref, wqi_ref,
                 wuv_ref, ya_ref, q_scr, qi_scr, wcol_scr, sc_scr, bias_scr, acc_scr, s_buf,
                 p_buf, kva_scr, sc16_scr):
    @pl.when(pl.program_id(1) == 0)
    def _():
        kva_scr[:, 0:KV_LORA] = ckv_ref[0]
        kva_scr[:, KV_LORA:] = jnp.ones((s_eff, KV_LORA), BF16)

    _attn_body(s_eff, topk, q_block0, cq_ref, ckv_ref, ki_ref, wi_ref, wuq_ref, wqi_ref,
               wuv_ref, ya_ref, q_scr, qi_scr, wcol_scr, sc_scr, bias_scr, acc_scr, s_buf,
               p_buf, kva_scr, sc16_scr)


def _attn(cq, ckv, ki, wi, w_uq, w_qi, w_uv_pad, n_buckets):
    b, s, _ = cq.shape
    nq = s // Q_BLOCK
    per = nq // n_buckets
    topk = min(TOPK_MAX, s // 4)
    aw = N_HEADS * HEAD_DIM

    def full(shape):
        return pl.BlockSpec(shape, lambda bi, i: (0,) * len(shape))

    parts = []
    for k in range(n_buckets):
        s_eff = (k + 1) * per * Q_BLOCK
        q0 = k * per

        def q_rows(width, q0=q0):
            return pl.BlockSpec((1, Q_BLOCK, width), lambda bi, i: (bi, i + q0, 0))

        def keys(width, s_eff=s_eff):
            return pl.BlockSpec((1, s_eff, width), lambda bi, i: (bi, 0, 0))

        parts.append(pl.pallas_call(
            functools.partial(_attn_kernel, s_eff, topk, q0),
            grid=(b, per),
            in_specs=[q_rows(Q_LORA), keys(KV_LORA), keys(IDX_HEADS * IDX_DIM), q_rows(LANES),
                      full(w_uq.shape), full(w_qi.shape), full(w_uv_pad.shape)],
            out_specs=pl.BlockSpec((1, Q_BLOCK, aw), lambda bi, i: (bi, i, 0)),
            out_shape=jax.ShapeDtypeStruct((b, per * Q_BLOCK, aw), BF16),
            scratch_shapes=[pltpu.VMEM((N_HEADS, Q_BLOCK, KV_LORA), BF16),
                            pltpu.VMEM((IDX_HEADS, Q_BLOCK, IDX_HEADS * IDX_DIM), BF16),
                            pltpu.VMEM((IDX_HEADS, Q_BLOCK, 1), F32),
                            pltpu.VMEM((Q_BLOCK, s_eff), F32),
                            pltpu.VMEM((Q_BLOCK, s_eff), F32),
                            pltpu.VMEM((Q_BLOCK, aw), F32),
                            pltpu.VMEM((2, Q_BLOCK, s_eff), F32),
                            pltpu.VMEM((2, Q_BLOCK, s_eff), BF16),
                            pltpu.VMEM((s_eff, 2 * KV_LORA), BF16),
                            pltpu.VMEM((Q_BLOCK, s_eff), BF16)],
            compiler_params=pltpu.CompilerParams(
                dimension_semantics=("parallel", "arbitrary"), vmem_limit_bytes=VMEM_LIMIT),
            name=f"attn_keys{s_eff}",
        )(cq, ckv, ki, wi, w_uq, w_qi, w_uv_pad))
    return jnp.concatenate(parts, axis=1)


def _merge_kernel(x_ref, ya_ref, gas_ref, pb_ref, wbra_ref, wo_ref, gffn_ref,
                  wr_hi_ref, wr_lo_ref, br_ref,
                  x1_ref, h2p_ref, idx_ref, gate_ref, rank_ref, cnt_ref, base_scr):
    step = pl.program_id(0)
    tm, d = x_ref.shape

    @pl.when(step == 0)
    def _():
        base_scr[...] = jnp.zeros_like(base_scr)

    a = _dot(ya_ref[...], wbra_ref[...])
    merged = gas_ref[...].astype(F32) * a + pb_ref[...].astype(F32)
    x1 = x_ref[...] + _dot(merged.astype(BF16), wo_ref[...])
    x1_ref[...] = x1
    h2 = _rms(x1, gffn_ref[...])

    h_hi = h2.astype(BF16)
    h2p_ref[...] = _pack_bf16_pairs(h2)
    h_lo = (h2 - h_hi.astype(F32)).astype(BF16)
    logits = (_dot(h_hi, wr_hi_ref[...]) + _dot(h_hi, wr_lo_ref[...])
              + _dot(h_lo, wr_hi_ref[...]) + br_ref[...])

    lane_e = lax.broadcasted_iota(jnp.int32, (tm, N_EXPERTS), 1).astype(F32)
    lane_o = lax.broadcasted_iota(jnp.int32, (tm, LANES), 1)
    work = logits
    vals, idxs = [], []
    onehot = jnp.zeros((tm, N_EXPERTS), F32)
    for _ in range(TOP_K):
        m = jnp.max(work, axis=1, keepdims=True)
        idx = jnp.min(jnp.where(work == m, lane_e, float(N_EXPERTS)), axis=1, keepdims=True)
        hit = lane_e == idx
        onehot = onehot + jnp.where(hit, 1.0, 0.0)
        work = jnp.where(hit, -jnp.inf, work)
        vals.append(m)
        idxs.append(idx)
    exps = [jnp.exp(v - vals[0]) for v in vals]
    denom = exps[0] + exps[1] + exps[2] + exps[3]

    r = lax.broadcasted_iota(jnp.int32, (tm, tm), 0)
    c = lax.broadcasted_iota(jnp.int32, (tm, tm), 1)
    tri = jnp.where(c < r, 1.0, 0.0).astype(BF16)
    rank_full = _dot(tri, onehot.astype(BF16)) + base_scr[...]

    idx_out = jnp.zeros((tm, LANES), F32)
    gate_out = jnp.zeros((tm, LANES), F32)
    rank_out = jnp.zeros((tm, LANES), F32)
    for k in range(TOP_K):
        rk = jnp.sum(jnp.where(lane_e == idxs[k], rank_full, 0.0), axis=1, keepdims=True)
        idx_out = jnp.where(lane_o == k, idxs[k], idx_out)
        gate_out = jnp.where(lane_o == k, exps[k] / denom, gate_out)
        rank_out = jnp.where(lane_o == k, rk, rank_out)
    gate_ref[...] = gate_out
    idx_ref[...] = idx_out.T[0:SUBLANES, :].astype(jnp.int32)
    rank_ref[...] = rank_out.T[0:SUBLANES, :].astype(jnp.int32)

    base_scr[...] = base_scr[...] + jnp.sum(onehot, axis=0, keepdims=True)
    cnt_ref[...] = base_scr[...].astype(jnp.int32)


def _merge(x2, ya, gas, pb, w_br_a, w_o, g_ffn, wr_hi, wr_lo, b_router, tm):
    n, d = x2.shape
    aw = ya.shape[1]

    def full(shape):
        return pl.BlockSpec(shape, lambda i: (0,) * len(shape))

    def rows(width):
        return pl.BlockSpec((tm, width), lambda i: (i, 0))

    slots = pl.BlockSpec((SUBLANES, tm), lambda i: (0, i))
    return pl.pallas_call(
        _merge_kernel,
        grid=(n // tm,),
        in_specs=[rows(d), rows(aw), rows(d), rows(d), full(w_br_a.shape), full(w_o.shape),
                  full((1, d)), full(wr_hi.shape), full(wr_lo.shape), full((1, N_EXPERTS))],
        out_specs=[rows(d), rows(d // 2), slots, rows(LANES), slots, full((1, N_EXPERTS))],
        out_shape=[jax.ShapeDtypeStruct((n, d), F32),
                   jax.ShapeDtypeStruct((n, d // 2), jnp.int32),
                   jax.ShapeDtypeStruct((SUBLANES, n), jnp.int32),
                   jax.ShapeDtypeStruct((n, LANES), F32),
                   jax.ShapeDtypeStruct((SUBLANES, n), jnp.int32),
                   jax.ShapeDtypeStruct((1, N_EXPERTS), jnp.int32)],
        scratch_shapes=[pltpu.VMEM((1, N_EXPERTS), F32)],
        compiler_params=pltpu.CompilerParams(dimension_semantics=("arbitrary",),
                                             vmem_limit_bytes=VMEM_LIMIT),
        name="merge_route",
    )(x2, ya, gas, pb, w_br_a, w_o, g_ffn, wr_hi, wr_lo, b_router)


def _sc_mesh():
    info = plsc.get_sparse_core_info()
    mesh = plsc.VectorSubcoreMesh(core_axis_name="c", subcore_axis_name="s")
    return mesh, info.num_cores, info.num_cores * info.num_subcores


def _sc_dispatch(src, dest_t, pad_idx, n_out_rows):
    n, w = src.shape
    mesh, n_cores, n_workers = _sc_mesh()
    per_w = n // n_workers
    n_chunks = per_w // SC_ROWS
    n_pad_chunks = pad_idx.size // (n_workers * SC_ROWS)

    @functools.partial(
        pl.kernel, mesh=mesh,
        out_type=jax.ShapeDtypeStruct((n_out_rows, w), jnp.int32),
        scratch_types=[pltpu.VMEM((n_chunks, TOP_K, SC_ROWS), jnp.int32),
                       pltpu.VMEM((n_pad_chunks, SC_ROWS), jnp.int32),
                       pltpu.VMEM((SC_ROWS, w), jnp.int32),
                       pltpu.VMEM((SC_ROWS, w), jnp.int32)],
        name="sc_dispatch",
    )
    def run(src_hbm, dest_hbm, pad_hbm, zeros_hbm, out_hbm, idx_v, pad_v, rows_v, zero_v):
        wid = lax.axis_index("s") * n_cores + lax.axis_index("c")
        pltpu.sync_copy(dest_hbm.at[wid], idx_v)
        pltpu.sync_copy(pad_hbm.at[wid], pad_v)
        pltpu.sync_copy(zeros_hbm, zero_v)

        @pl.loop(0, n_pad_chunks)
        def _(c):
            pltpu.sync_copy(zero_v, out_hbm.at[pad_v.at[c]])

        @pl.loop(0, n_chunks)
        def _(c):
            pltpu.sync_copy(src_hbm.at[pl.ds(wid * per_w + c * SC_ROWS, SC_ROWS)], rows_v)
            for k in range(TOP_K):
                pltpu.sync_copy(rows_v, out_hbm.at[idx_v.at[c, k]])

    dest_w = dest_t.reshape(TOP_K, n_workers, n_chunks, SC_ROWS).transpose(1, 2, 0, 3)
    return run(src, dest_w, pad_idx.reshape(n_workers, n_pad_chunks, SC_ROWS),
               jnp.zeros((SC_ROWS, w), jnp.int32))


def _sc_gather(table, idx):
    n_rows = idx.shape[0]
    w = table.shape[1]
    mesh, n_cores, n_workers = _sc_mesh()
    per_w = n_rows // n_workers
    n_chunks = per_w // SC_ROWS
    assert n_chunks % 2 == 0 and n_chunks * SC_ROWS * n_workers == n_rows

    @functools.partial(
        pl.kernel, mesh=mesh,
        out_type=jax.ShapeDtypeStruct((n_rows, w), table.dtype),
        scratch_types=[pltpu.VMEM((n_chunks, SC_ROWS), jnp.int32),
                       pltpu.VMEM((SC_ROWS, w), table.dtype),
                       pltpu.VMEM((SC_ROWS, w), table.dtype),
                       pltpu.SemaphoreType.DMA, pltpu.SemaphoreType.DMA],
        name="sc_gather",
    )
    def run(table_hbm, idx_hbm, out_hbm, idx_v, buf0, buf1, sem0, sem1):
        wid = lax.axis_index("s") * n_cores + lax.axis_index("c")
        base = wid * per_w
        pltpu.sync_copy(idx_hbm.at[wid], idx_v)

        def gather(c, buf, sem):
            return pltpu.make_async_copy(table_hbm.at[idx_v.at[c]], buf, sem)

        gather(0, buf0, sem0).start()

        @pl.loop(0, n_chunks, step=2)
        def _(c):
            gather(c + 1, buf1, sem1).start()
            gather(c, buf0, sem0).wait()
            pltpu.sync_copy(buf0, out_hbm.at[pl.ds(base + c * SC_ROWS, SC_ROWS)])

            @pl.when(c + 2 < n_chunks)
            def _():
                gather(c + 2, buf0, sem0).start()

            gather(c + 1, buf1, sem1).wait()
            pltpu.sync_copy(buf1, out_hbm.at[pl.ds(base + (c + 1) * SC_ROWS, SC_ROWS)])

    return run(table, idx.reshape(n_workers, n_chunks, SC_ROWS))


def _expert_kernel(be_ref, nu_ref, xs_ref, wgu_ref, bgu_ref, wd_ref, bd_ref, y_ref,
                   wgu_bf, wd_bf):
    i = pl.program_id(0)
    f = wd_ref.shape[1]
    half = xs_ref.shape[1]
    live = i < nu_ref[0]

    @pl.when(live & ((i == 0) | (be_ref[i] != be_ref[jnp.maximum(i - 1, 0)])))
    def _():
        wgu_bf[...] = wgu_ref[0].astype(BF16)
        wd_bf[...] = wd_ref[0].astype(BF16)

    @pl.when(live)
    def _():
        x_lo, x_hi = _unpack_bf16_pairs(xs_ref[...])
        gu = (_dot(x_lo.astype(BF16), wgu_bf[0:half, :])
              + _dot(x_hi.astype(BF16), wgu_bf[half:2 * half, :])
              + bgu_ref[0])
        gate = jnp.minimum(gu[:, :f], SWIGLU_LIMIT)
        up = jnp.clip(gu[:, f:], -SWIGLU_LIMIT, SWIGLU_LIMIT)
        act = (up + 1.0) * (gate * jax.nn.sigmoid(SWIGLU_ALPHA * gate))
        y = _dot(act.astype(BF16), wd_bf[...]) + bd_ref[0]
        y_ref[...] = _pack_bf16_pairs(y)


def _experts(block_e, n_used, xs, n_blocks, w_gu, b_gu, w_down, b_down):
    half = xs.shape[1]
    d = 2 * half
    f2 = w_gu.shape[2]
    f = w_down.shape[1]

    def blk(i, be, nu):
        return (jnp.minimum(i, nu[0] - 1), 0)

    def expert(i, be, nu):
        return (be[jnp.minimum(i, nu[0] - 1)], 0, 0)

    grid_spec = pltpu.PrefetchScalarGridSpec(
        num_scalar_prefetch=2,
        grid=(n_blocks,),
        in_specs=[pl.BlockSpec((EXPERT_BLOCK, half), blk),
                  pl.BlockSpec((1, d, f2), expert),
                  pl.BlockSpec((1, 1, f2), expert),
                  pl.BlockSpec((1, f, d), expert),
                  pl.BlockSpec((1, 1, d), expert)],
        out_specs=pl.BlockSpec((EXPERT_BLOCK, half), blk),
        scratch_shapes=[pltpu.VMEM((d, f2), BF16), pltpu.VMEM((f, d), BF16)],
    )
    return pl.pallas_call(
        _expert_kernel,
        grid_spec=grid_spec,
        out_shape=jax.ShapeDtypeStruct((n_blocks * EXPERT_BLOCK, half), jnp.int32),
        compiler_params=pltpu.CompilerParams(dimension_semantics=("arbitrary",),
                                             vmem_limit_bytes=VMEM_LIMIT),
        name="experts",
    )(block_e, n_used, xs, w_gu, b_gu, w_down, b_down)


def _combine_kernel(yg_ref, x1_ref, gate_ref, gfin_ref, *refs):
    out_ref = refs[-1]
    gates = gate_ref[...]
    half = yg_ref.shape[2]
    acc_lo = x1_ref[:, :half]
    acc_hi = x1_ref[:, half:]
    for k in range(TOP_K):
        y_lo, y_hi = _unpack_bf16_pairs(yg_ref[k])
        acc_lo = acc_lo + gates[:, k:k + 1] * y_lo
        acc_hi = acc_hi + gates[:, k:k + 1] * y_hi
    out_ref[...] = _rms(jnp.concatenate([acc_lo, acc_hi], axis=1), gfin_ref[...])


def _combine(yg_part, x1, gates, g_final, tm, part, n_parts, out_so_far=None):
    n, d = x1.shape
    steps = n // n_parts // tm
    row0 = part * steps
    operands = [yg_part, x1, gates, g_final]
    in_specs = [pl.BlockSpec((TOP_K, tm, d // 2), lambda i: (0, i, 0)),
                pl.BlockSpec((tm, d), lambda i: (i + row0, 0)),
                pl.BlockSpec((tm, LANES), lambda i: (i + row0, 0)),
                pl.BlockSpec((1, d), lambda i: (0, 0))]
    aliases = {}
    if out_so_far is not None:
        operands.append(out_so_far)
        in_specs.append(pl.BlockSpec(memory_space=pl.ANY))
        aliases = {len(operands) - 1: 0}
    return pl.pallas_call(
        _combine_kernel,
        grid=(steps,),
        in_specs=in_specs,
        out_specs=pl.BlockSpec((tm, d), lambda i: (i + row0, 0)),
        out_shape=jax.ShapeDtypeStruct((n, d), F32),
        input_output_aliases=aliases,
        compiler_params=pltpu.CompilerParams(dimension_semantics=("parallel",),
                                             vmem_limit_bytes=VMEM_LIMIT),
        name=f"combine_part{part}",
    )(*operands)


def _pack_in_proj(w_in, d):
    offs = [0]
    for width in (Q_LORA, KV_LORA, IDX_DIM, IDX_HEADS, SGU_WIDTH, SGU_WIDTH, d, d):
        offs.append(offs[-1] + width)
    wq, wkv, wki, wwi, wu, wv, wga, wgb = [w_in[:, offs[j]:offs[j + 1]] for j in range(8)]
    wki_rep = jnp.tile(wki, (1, IDX_HEADS))
    wwi_pad = jnp.pad(wwi, ((0, 0), (0, LANES - IDX_HEADS)))
    return jnp.concatenate([wq, wkv, wki_rep, wwi_pad, wu, wv, wga, wgb], axis=1).astype(BF16)


def _layer(x, g_mix, w_in, g_cq, g_ckv, w_uq, w_uv, w_q_idx, g_kidx, b_kidx, g_sgu, b_sgu,
           w_spatial, b_spatial, w_br_a, w_br_b, w_o, g_ffn, w_router, b_router, w_gu, b_gu,
           w_down, b_down, g_final):
    b, s, d = x.shape
    n = b * s
    x2 = x.reshape(n, d)
    row = lambda v: v.reshape(1, -1).astype(F32)

    w_in_p = _pack_in_proj(w_in, d)
    ws_pair = w_spatial.reshape(SGU_GROUPS // 2, 2, SGU_CHUNK, SGU_CHUNK).transpose(
        0, 2, 1, 3).reshape(SGU_GROUPS // 2, SGU_CHUNK, 2 * SGU_CHUNK)
    bsp = jnp.repeat(b_spatial.T, SGU_GROUP_DIM, axis=1)
    cq, ckv, ki, wi, gas, pb = _inproj(
        x2, row(g_mix), w_in_p, row(g_cq), row(g_ckv), row(jnp.tile(g_kidx, IDX_HEADS)),
        row(jnp.tile(b_kidx, IDX_HEADS)), row(g_sgu), row(b_sgu), ws_pair, bsp,
        w_br_b.astype(BF16), tm=512)

    head_eye = jnp.eye(N_HEADS, dtype=F32)
    w_uv_pad = (w_uv[:, :, None, :] * head_eye[:, None, :, None]).reshape(
        N_HEADS, KV_LORA, N_HEADS * HEAD_DIM)
    n_buckets = next(nbk for nbk in (8, 4, 2, 1) if (s // Q_BLOCK) % nbk == 0)
    ya = _attn(cq.reshape(b, s, -1), ckv.reshape(b, s, -1), ki.reshape(b, s, -1),
               wi.reshape(b, s, -1), w_uq.astype(BF16), w_q_idx.astype(BF16),
               w_uv_pad.astype(BF16), n_buckets)

    wr_hi = w_router.astype(BF16)
    wr_lo = (w_router - wr_hi.astype(F32)).astype(BF16)
    x1, h2p, idx_t, gate_p, rank_t, counts = _merge(
        x2, ya.reshape(n, -1), gas, pb, w_br_a.astype(BF16), w_o.astype(BF16), row(g_ffn),
        wr_hi, wr_lo, row(b_router), tm=512)

    counts = counts[0]
    padded = (counts + EXPERT_BLOCK - 1) // EXPERT_BLOCK * EXPERT_BLOCK
    padded_end = jnp.cumsum(padded)
    padded_start = padded_end - padded
    expert_ids = jnp.arange(N_EXPERTS, dtype=jnp.int32)[:, None, None]
    group_start = jnp.sum(jnp.where(idx_t[None, :TOP_K] == expert_ids,
                                    padded_start[:, None, None], 0), axis=0)
    dest_t = group_start + rank_t[:TOP_K]
    nk = n * TOP_K
    nb = -(-(nk + N_EXPERTS * EXPERT_BLOCK) // EXPERT_BLOCK)
    block_row0 = jnp.arange(nb, dtype=jnp.int32) * EXPERT_BLOCK
    block_e = jnp.minimum(
        jnp.sum((padded_end[None, :] <= block_row0[:, None]).astype(jnp.int32), axis=1),
        N_EXPERTS - 1).astype(jnp.int32)
    n_used = (padded_end[-1:] // EXPERT_BLOCK).astype(jnp.int32)
    j = jnp.arange(EXPERT_BLOCK, dtype=jnp.int32)[None, :]
    pad_idx = jnp.where(j < (padded - counts)[:, None], (padded_start + counts)[:, None] + j,
                        nb * EXPERT_BLOCK + j).astype(jnp.int32)

    xs = _sc_dispatch(h2p, dest_t, pad_idx, (nb + 1) * EXPERT_BLOCK)
    ybuf = _experts(block_e, n_used, xs, nb, w_gu, b_gu.reshape(N_EXPERTS, 1, -1), w_down,
                    b_down.reshape(N_EXPERTS, 1, -1))
    tm_c = 512
    n_parts = next(p for p in (COMBINE_PARTS, 2, 1) if n % (p * tm_c) == 0)
    n_part = n // n_parts
    out = None
    for part in range(n_parts):
        dest_part = dest_t[:, part * n_part:(part + 1) * n_part]
        yg = _sc_gather(ybuf, dest_part.reshape(-1))
        out = _combine(yg.reshape(TOP_K, n_part, d // 2), x1, gate_p, row(g_final), tm_c, part,
                       n_parts, out)
    return out.reshape(b, s, d)


def kernel(x, g_mix, w_in, g_cq, g_ckv, w_uq, w_uv, w_q_idx, g_kidx, b_kidx, g_sgu, b_sgu,
           w_spatial, b_spatial, w_br_a, w_br_b, w_o, g_ffn, w_router, b_router, w_gu, b_gu,
           w_down, b_down, g_final):
    assert g_mix.shape[0] == 1, "single-layer block"
    return _layer(x, g_mix[0], w_in[0], g_cq[0], g_ckv[0], w_uq[0], w_uv[0], w_q_idx[0],
                  g_kidx[0], b_kidx[0], g_sgu[0], b_sgu[0], w_spatial[0], b_spatial[0],
                  w_br_a[0], w_br_b[0], w_o[0], g_ffn[0], w_router[0], b_router[0], w_gu[0],
                  b_gu[0], w_down[0], b_down[0], g_final)
```

```python
import functools

import jax
import jax.numpy as jnp
from jax import lax
from jax.experimental import pallas as pl
from jax.experimental.pallas import tpu as pltpu
from jax.experimental.pallas import tpu_sc as plsc

EPS = 1e-6
CHUNK = 64
N_HEADS = 8
HEAD_DIM = 64
Q_LORA = 256
KV_LORA = 128
IDX_HEADS = 8
IDX_DIM = 32
TOPK_MAX = 256
Q_BLOCK = 256
ATTN_SCALE = KV_LORA ** -0.5
IDX_SCALE = (IDX_HEADS * IDX_DIM) ** -0.5
SGU_CHUNK = 128
SGU_GROUPS = 8
SGU_WIDTH = 512
SGU_GROUP_DIM = SGU_WIDTH // SGU_GROUPS
N_EXPERTS = 32
TOP_K = 4
SWIGLU_LIMIT = 7.0
SWIGLU_ALPHA = 1.702
EXPERT_BLOCK = 512

SC_ROWS = 32
LANES = 128
SUBLANES = 8
VMEM_LIMIT = 56 * 1024 * 1024
INT_MIN = -(2 ** 31)
CODE_NEG_INF = INT_MIN + 0x7FFFFF

BF16 = jnp.bfloat16
F32 = jnp.float32


def _dot(a, b):
    return jnp.dot(a, b, preferred_element_type=F32)


def _dot_nt(a, b):
    return lax.dot_general(a, b, (((1,), (1,)), ((), ())), preferred_element_type=F32)


def _rms(x, g):
    return x * lax.rsqrt(jnp.mean(x * x, axis=-1, keepdims=True) + EPS) * g


def _pack_bf16_pairs(x):
    half = x.shape[1] // 2
    bits = lax.bitcast_convert_type(x.astype(BF16).astype(F32), jnp.int32)
    return lax.shift_right_logical(bits[:, :half], 16) | (bits[:, half:] & jnp.int32(-65536))


def _unpack_bf16_pairs(words):
    lo = lax.bitcast_convert_type(lax.shift_left(words, 16), F32)
    hi = lax.bitcast_convert_type(words & jnp.int32(-65536), F32)
    return lo, hi


def _layer_norm(x, g, b):
    mu = jnp.mean(x, axis=-1, keepdims=True)
    xc = x - mu
    var = jnp.mean(xc * xc, axis=-1, keepdims=True)
    return xc * lax.rsqrt(var + EPS) * g + b


C_Q = 0
C_KV = C_Q + Q_LORA
C_KI = C_KV + KV_LORA
C_WI = C_KI + IDX_HEADS * IDX_DIM
C_U = C_WI + LANES
C_V = C_U + SGU_WIDTH
D_IN_P_BASE = C_V + SGU_WIDTH


def _inproj_kernel(x_ref, gmix_ref, w_ref, gcq_ref, gckv_ref, gki_ref, bki_ref,
                   gsgu_ref, bsgu_ref, wsp_ref, bsp_ref, wbrb_ref,
                   cq_ref, ckv_ref, ki_ref, wi_ref, gas_ref, pb_ref):
    tm, d = x_ref.shape
    c_ga = D_IN_P_BASE
    c_gb = c_ga + d
    h = _rms(x_ref[...], gmix_ref[...]).astype(BF16)

    def proj(lo, width):
        return _dot(h, w_ref[:, lo:lo + width])

    cq_ref[...] = _rms(proj(C_Q, Q_LORA), gcq_ref[...]).astype(BF16)
    ckv_ref[...] = _rms(proj(C_KV, KV_LORA), gckv_ref[...]).astype(BF16)
    ki_ref[...] = _layer_norm(proj(C_KI, IDX_HEADS * IDX_DIM), gki_ref[...],
                              bki_ref[...]).astype(BF16)
    wi_ref[...] = proj(C_WI, LANES) * IDX_SCALE
    gas_ref[...] = jax.nn.sigmoid(proj(c_ga, d)).astype(BF16)

    u = jax.nn.gelu(proj(C_U, SGU_WIDTH))
    v = _layer_norm(jax.nn.gelu(proj(C_V, SGU_WIDTH)), gsgu_ref[...], bsgu_ref[...])

    row = lax.broadcasted_iota(jnp.int32, (SGU_CHUNK, 2 * SGU_CHUNK), 0)
    col = lax.broadcasted_iota(jnp.int32, (SGU_CHUNK, 2 * SGU_CHUNK), 1) % SGU_CHUNK
    causal = (row // CHUNK) >= (col // CHUNK)
    lane = lax.broadcasted_iota(jnp.int32, (SGU_CHUNK, LANES), 1)
    left = lane < SGU_GROUP_DIM
    n_tiles = SGU_WIDTH // LANES
    ws = [jnp.where(causal, wsp_ref[j], 0.0).astype(BF16) for j in range(n_tiles)]
    yb_chunks = []
    for c in range(tm // SGU_CHUNK):
        tiles = []
        for j in range(n_tiles):
            blk = v[c * SGU_CHUNK:(c + 1) * SGU_CHUNK, j * LANES:(j + 1) * LANES]
            stacked = jnp.concatenate(
                [jnp.where(left, blk, 0.0), jnp.where(left, 0.0, blk)], axis=0).astype(BF16)
            tiles.append(_dot(ws[j], stacked))
        s = jnp.concatenate(tiles, axis=1) + bsp_ref[...]
        yb_chunks.append(u[c * SGU_CHUNK:(c + 1) * SGU_CHUNK, :] * s)
    yb = jnp.concatenate(yb_chunks, axis=0).astype(BF16)
    pb_ref[...] = (jax.nn.sigmoid(proj(c_gb, d)) * _dot(yb, wbrb_ref[...])).astype(BF16)


def _inproj(x2, g_mix, w_in_p, g_cq, g_ckv, g_ki, b_ki, g_sgu, b_sgu, ws_pair, bsp, w_br_b,
            tm):
    n, d = x2.shape
    d_in_p = w_in_p.shape[1]

    def full(shape):
        return pl.BlockSpec(shape, lambda i: (0,) * len(shape))

    def rows(width):
        return pl.BlockSpec((tm, width), lambda i: (i, 0))

    return pl.pallas_call(
        _inproj_kernel,
        grid=(n // tm,),
        in_specs=[rows(d), full((1, d)), full((d, d_in_p)), full((1, Q_LORA)),
                  full((1, KV_LORA)), full((1, IDX_HEADS * IDX_DIM)),
                  full((1, IDX_HEADS * IDX_DIM)), full((1, SGU_WIDTH)), full((1, SGU_WIDTH)),
                  full(ws_pair.shape), full(bsp.shape), full(w_br_b.shape)],
        out_specs=[rows(Q_LORA), rows(KV_LORA), rows(IDX_HEADS * IDX_DIM), rows(LANES),
                   rows(d), rows(d)],
        out_shape=[jax.ShapeDtypeStruct((n, Q_LORA), BF16),
                   jax.ShapeDtypeStruct((n, KV_LORA), BF16),
                   jax.ShapeDtypeStruct((n, IDX_HEADS * IDX_DIM), BF16),
                   jax.ShapeDtypeStruct((n, LANES), F32),
                   jax.ShapeDtypeStruct((n, d), BF16),
                   jax.ShapeDtypeStruct((n, d), BF16)],
        compiler_params=pltpu.CompilerParams(dimension_semantics=("parallel",),
                                             vmem_limit_bytes=VMEM_LIMIT),
        name="inproj",
    )(x2, g_mix, w_in_p, g_cq, g_ckv, g_ki, b_ki, g_sgu, b_sgu, ws_pair, bsp, w_br_b)


def _attn_body(s_eff, topk, q_block0, cq_ref, ckv_ref, ki_ref, wi_ref, wuq_ref, wqi_ref,
               wuv_ref, ya_ref, q_scr, qi_scr, wcol_scr, sc_scr, bias_scr, acc_scr, s_buf,
               p_buf, kva_scr, sc16_scr):
    i = pl.program_id(1) + q_block0
    cq = cq_ref[0]
    q = _dot(cq, wuq_ref[...])
    for h in range(N_HEADS):
        q_scr[h] = (q[:, h * KV_LORA:(h + 1) * KV_LORA] * ATTN_SCALE).astype(BF16)

    rowq = lax.broadcasted_iota(jnp.int32, (Q_BLOCK, s_eff), 0)
    colk = lax.broadcasted_iota(jnp.int32, (Q_BLOCK, s_eff), 1)
    allowed = (colk // CHUNK) <= ((i * Q_BLOCK + rowq) // CHUNK)

    if s_eff <= topk:
        bias_scr[...] = jnp.where(allowed, 0.0, -jnp.inf)
    else:
        _select_topk(s_eff, topk, cq, allowed, colk, ki_ref, wi_ref, wqi_ref, qi_scr, wcol_scr,
                     sc_scr, bias_scr, s_buf, sc16_scr)

    acc_scr[...] = jnp.zeros_like(acc_scr)

    def logits(h):
        s_buf[h % 2] = _dot_nt(q_scr[h], ckv_ref[0, 0:s_eff, :]) + bias_scr[...]

    def softmax(h):
        s = s_buf[h % 2]
        p_buf[h % 2] = jnp.exp(s - jnp.max(s, axis=1, keepdims=True)).astype(BF16)

    def values(h):
        ol = _dot(p_buf[h % 2], kva_scr[0:s_eff, :])
        o = ol[:, :KV_LORA] / ol[:, KV_LORA:]
        acc_scr[...] += _dot(o.astype(BF16), wuv_ref[h])

    logits(0)
    for h in range(N_HEADS):
        if h + 1 < N_HEADS:
            logits(h + 1)
        softmax(h)
        values(h)
    ya_ref[0] = acc_scr[...].astype(BF16)


def _select_topk(s_eff, topk, cq, allowed, colk, ki_ref, wi_ref, wqi_ref, qi_scr, wcol_scr,
                 sc_scr, bias_scr, s_buf, sc16_scr):
    qi = _dot(cq, wqi_ref[...]).astype(BF16)
    wi = wi_ref[0]
    head_of_lane = lax.broadcasted_iota(jnp.int32, qi.shape, 1) // IDX_DIM
    for h in range(IDX_HEADS):
        qi_scr[h] = jnp.where(head_of_lane == h, qi, jnp.zeros_like(qi))
        wcol_scr[h] = wi[:, h:h + 1]

    def index_dots(h):
        s_buf[h % 2] = _dot_nt(qi_scr[h], ki_ref[0, 0:s_eff, :])

    index_dots(0)
    for h in range(IDX_HEADS):
        if h + 1 < IDX_HEADS:
            index_dots(h + 1)
        term = wcol_scr[h] * jnp.maximum(s_buf[h % 2], 0.0)
        bias_scr[...] = term if h == 0 else bias_scr[...] + term

    sc_scr[...] = jnp.where(allowed, bias_scr[...], -jnp.inf)

    def code_to_bits(code):
        return jnp.where(code < 0, code ^ jnp.int32(0x7FFFFFFF), code)

    def code_to_float(code):
        return lax.bitcast_convert_type(code_to_bits(code), F32)

    upper_half = jnp.int32(-65536)
    sc_bits = lax.bitcast_convert_type(sc_scr[...], jnp.int32)
    sc16_scr[...] = lax.bitcast_convert_type(sc_bits & upper_half, F32).astype(BF16)

    def count_ge_upper(cand):
        thr16 = lax.bitcast_convert_type(code_to_bits(cand) & upper_half, F32).astype(BF16)
        part = None
        for t in range(s_eff // LANES):
            tile = sc16_scr[:, t * LANES:(t + 1) * LANES]
            ind = jnp.where(tile >= thr16, jnp.ones((), BF16), jnp.zeros((), BF16))
            part = ind if part is None else part + ind
        return jnp.sum(part.astype(F32), axis=1, keepdims=True)

    def count_ge(cand):
        return jnp.sum(jnp.where(sc_scr[...] >= code_to_float(cand), 1.0, 0.0), axis=1,
                       keepdims=True)

    kf = float(topk)
    code = jnp.where(count_ge_upper(jnp.zeros((Q_BLOCK, 1), jnp.int32)) >= kf,
                     jnp.int32(0), jnp.int32(INT_MIN))
    for bit in range(30, -1, -1):
        cand = code + jnp.int32(1 << bit)
        cnt = count_ge_upper(cand) if bit >= 16 else count_ge(cand)
        feasible = (cnt >= kf) | (cand <= jnp.int32(CODE_NEG_INF))
        code = jnp.where(feasible, cand, code)

    thr = code_to_float(code)
    thr_up = code_to_float(code + 1)
    sel = sc_scr[...] >= thr
    n_ge = jnp.sum(jnp.where(sel, 1.0, 0.0), axis=1, keepdims=True)
    bias_scr[...] = jnp.where(sel & allowed, 0.0, -jnp.inf)
    tie_rows = (n_ge > kf) & (code > jnp.int32(CODE_NEG_INF))
    any_tie = jnp.max(jnp.where(tie_rows, 1.0, 0.0))

    @pl.when(any_tie > 0.0)
    def _():
        sc = sc_scr[...]
        gt = sc >= thr_up
        eq = (sc >= thr) & jnp.logical_not(gt)
        need = kf - jnp.sum(jnp.where(gt, 1.0, 0.0), axis=1, keepdims=True)
        nbits = (s_eff - 1).bit_length()

        def tie_step(it, j):
            cand = j + jnp.left_shift(jnp.int32(1), jnp.int32(nbits - 1) - it)
            v = sc_scr[...]
            tie = (v >= thr) & jnp.logical_not(v >= thr_up) & (colk < cand)
            cnt = jnp.sum(jnp.where(tie, 1.0, 0.0), axis=1, keepdims=True)
            return jnp.where(cnt < need, cand, j)

        jmax = lax.fori_loop(0, nbits, tie_step, jnp.zeros((Q_BLOCK, 1), jnp.int32))
        keep = gt | (eq & (colk <= jmax))
        bias_scr[...] = jnp.where(keep & allowed, 0.0, -jnp.inf)


def _attn_kernel(s_eff, topk, q_block0, cq_ref, ckv_ref, ki_ref, wi_ref, wuq_ref, wqi_ref,
                 wuv_ref, ya_ref, q_scr, qi_scr, wcol_scr, sc_scr, bias_scr, acc_scr, s_buf,
                 p_buf, kva_scr, sc16_scr):
    @pl.when(pl.program_id(1) == 0)
    def _():
        kva_scr[:, 0:KV_LORA] = ckv_ref[0]
        kva_scr[:, KV_LORA:] = jnp.ones((s_eff, KV_LORA), BF16)

    _attn_body(s_eff, topk, q_block0, cq_ref, ckv_ref, ki_ref, wi_ref, wuq_ref, wqi_ref,
               wuv_ref, ya_ref, q_scr, qi_scr, wcol_scr, sc_scr, bias_scr, acc_scr, s_buf,
               p_buf, kva_scr, sc16_scr)


def _attn(cq, ckv, ki, wi, w_uq, w_qi, w_uv_pad, n_buckets):
    b, s, _ = cq.shape
    nq = s // Q_BLOCK
    per = nq // n_buckets
    topk = min(TOPK_MAX, s // 4)
    aw = N_HEADS * HEAD_DIM

    def full(shape):
        return pl.BlockSpec(shape, lambda bi, i: (0,) * len(shape))

    parts = []
    for k in range(n_buckets):
        s_eff = (k + 1) * per * Q_BLOCK
        q0 = k * per

        def q_rows(width, q0=q0):
            return pl.BlockSpec((1, Q_BLOCK, width), lambda bi, i: (bi, i + q0, 0))

        def keys(width, s_eff=s_eff):
            return pl.BlockSpec((1, s_eff, width), lambda bi, i: (bi, 0, 0))

        parts.append(pl.pallas_call(
            functools.partial(_attn_kernel, s_eff, topk, q0),
            grid=(b, per),
            in_specs=[q_rows(Q_LORA), keys(KV_LORA), keys(IDX_HEADS * IDX_DIM), q_rows(LANES),
                      full(w_uq.shape), full(w_qi.shape), full(w_uv_pad.shape)],
            out_specs=pl.BlockSpec((1, Q_BLOCK, aw), lambda bi, i: (bi, i, 0)),
            out_shape=jax.ShapeDtypeStruct((b, per * Q_BLOCK, aw), BF16),
            scratch_shapes=[pltpu.VMEM((N_HEADS, Q_BLOCK, KV_LORA), BF16),
                            pltpu.VMEM((IDX_HEADS, Q_BLOCK, IDX_HEADS * IDX_DIM), BF16),
                            pltpu.VMEM((IDX_HEADS, Q_BLOCK, 1), F32),
                            pltpu.VMEM((Q_BLOCK, s_eff), F32),
                            pltpu.VMEM((Q_BLOCK, s_eff), F32),
                            pltpu.VMEM((Q_BLOCK, aw), F32),
                            pltpu.VMEM((2, Q_BLOCK, s_eff), F32),
                            pltpu.VMEM((2, Q_BLOCK, s_eff), BF16),
                            pltpu.VMEM((s_eff, 2 * KV_LORA), BF16),
                            pltpu.VMEM((Q_BLOCK, s_eff), BF16)],
            compiler_params=pltpu.CompilerParams(
                dimension_semantics=("parallel", "arbitrary"), vmem_limit_bytes=VMEM_LIMIT),
            name=f"attn_keys{s_eff}",
        )(cq, ckv, ki, wi, w_uq, w_qi, w_uv_pad))
    return jnp.concatenate(parts, axis=1)


def _merge_kernel(x_ref, ya_ref, gas_ref, pb_ref, wbra_ref, wo_ref, gffn_ref,
                  wr_hi_ref, wr_lo_ref, br_ref,
                  x1_ref, h2p_ref, idx_ref, gate_ref, rank_ref, cnt_ref, base_scr):
    step = pl.program_id(0)
    tm, d = x_ref.shape

    @pl.when(step == 0)
    def _():
        base_scr[...] = jnp.zeros_like(base_scr)

    a = _dot(ya_ref[...], wbra_ref[...])
    merged = gas_ref[...].astype(F32) * a + pb_ref[...].astype(F32)
    x1 = x_ref[...] + _dot(merged.astype(BF16), wo_ref[...])
    x1_ref[...] = x1
    h2 = _rms(x1, gffn_ref[...])

    h_hi = h2.astype(BF16)
    h2p_ref[...] = _pack_bf16_pairs(h2)
    h_lo = (h2 - h_hi.astype(F32)).astype(BF16)
    logits = (_dot(h_hi, wr_hi_ref[...]) + _dot(h_hi, wr_lo_ref[...])
              + _dot(h_lo, wr_hi_ref[...]) + br_ref[...])

    lane_e = lax.broadcasted_iota(jnp.int32, (tm, N_EXPERTS), 1).astype(F32)
    lane_o = lax.broadcasted_iota(jnp.int32, (tm, LANES), 1)
    work = logits
    vals, idxs = [], []
    onehot = jnp.zeros((tm, N_EXPERTS), F32)
    for _ in range(TOP_K):
        m = jnp.max(work, axis=1, keepdims=True)
        idx = jnp.min(jnp.where(work == m, lane_e, float(N_EXPERTS)), axis=1, keepdims=True)
        hit = lane_e == idx
        onehot = onehot + jnp.where(hit, 1.0, 0.0)
        work = jnp.where(hit, -jnp.inf, work)
        vals.append(m)
        idxs.append(idx)
    exps = [jnp.exp(v - vals[0]) for v in vals]
    denom = exps[0] + exps[1] + exps[2] + exps[3]

    r = lax.broadcasted_iota(jnp.int32, (tm, tm), 0)
    c = lax.broadcasted_iota(jnp.int32, (tm, tm), 1)
    tri = jnp.where(c < r, 1.0, 0.0).astype(BF16)
    rank_full = _dot(tri, onehot.astype(BF16)) + base_scr[...]

    idx_out = jnp.zeros((tm, LANES), F32)
    gate_out = jnp.zeros((tm, LANES), F32)
    rank_out = jnp.zeros((tm, LANES), F32)
    for k in range(TOP_K):
        rk = jnp.sum(jnp.where(lane_e == idxs[k], rank_full, 0.0), axis=1, keepdims=True)
        idx_out = jnp.where(lane_o == k, idxs[k], idx_out)
        gate_out = jnp.where(lane_o == k, exps[k] / denom, gate_out)
        rank_out = jnp.where(lane_o == k, rk, rank_out)
    gate_ref[...] = gate_out
    idx_ref[...] = idx_out.T[0:SUBLANES, :].astype(jnp.int32)
    rank_ref[...] = rank_out.T[0:SUBLANES, :].astype(jnp.int32)

    base_scr[...] = base_scr[...] + jnp.sum(onehot, axis=0, keepdims=True)
    cnt_ref[...] = base_scr[...].astype(jnp.int32)


def _merge(x2, ya, gas, pb, w_br_a, w_o, g_ffn, wr_hi, wr_lo, b_router, tm):
    n, d = x2.shape
    aw = ya.shape[1]

    def full(shape):
        return pl.BlockSpec(shape, lambda i: (0,) * len(shape))

    def rows(width):
        return pl.BlockSpec((tm, width), lambda i: (i, 0))

    slots = pl.BlockSpec((SUBLANES, tm), lambda i: (0, i))
    return pl.pallas_call(
        _merge_kernel,
        grid=(n // tm,),
        in_specs=[rows(d), rows(aw), rows(d), rows(d), full(w_br_a.shape), full(w_o.shape),
                  full((1, d)), full(wr_hi.shape), full(wr_lo.shape), full((1, N_EXPERTS))],
        out_specs=[rows(d), rows(d // 2), slots, rows(LANES), slots, full((1, N_EXPERTS))],
        out_shape=[jax.ShapeDtypeStruct((n, d), F32),
                   jax.ShapeDtypeStruct((n, d // 2), jnp.int32),
                   jax.ShapeDtypeStruct((SUBLANES, n), jnp.int32),
                   jax.ShapeDtypeStruct((n, LANES), F32),
                   jax.ShapeDtypeStruct((SUBLANES, n), jnp.int32),
                   jax.ShapeDtypeStruct((1, N_EXPERTS), jnp.int32)],
        scratch_shapes=[pltpu.VMEM((1, N_EXPERTS), F32)],
        compiler_params=pltpu.CompilerParams(dimension_semantics=("arbitrary",),
                                             vmem_limit_bytes=VMEM_LIMIT),
        name="merge_route",
    )(x2, ya, gas, pb, w_br_a, w_o, g_ffn, wr_hi, wr_lo, b_router)


def _sc_mesh():
    info = plsc.get_sparse_core_info()
    mesh = plsc.VectorSubcoreMesh(core_axis_name="c", subcore_axis_name="s")
    return mesh, info.num_cores, info.num_cores * info.num_subcores


def _sc_dispatch(src, dest_t, pad_idx, n_out_rows):
    n, w = src.shape
    mesh, n_cores, n_workers = _sc_mesh()
    per_w = n // n_workers
    n_chunks = per_w // SC_ROWS
    n_pad_chunks = pad_idx.size // (n_workers * SC_ROWS)

    @functools.partial(
        pl.kernel, mesh=mesh,
        out_type=jax.ShapeDtypeStruct((n_out_rows, w), jnp.int32),
        scratch_types=[pltpu.VMEM((n_chunks, TOP_K, SC_ROWS), jnp.int32),
                       pltpu.VMEM((n_pad_chunks, SC_ROWS), jnp.int32),
                       pltpu.VMEM((SC_ROWS, w), jnp.int32),
                       pltpu.VMEM((SC_ROWS, w), jnp.int32)],
        name="sc_dispatch",
    )
    def run(src_hbm, dest_hbm, pad_hbm, zeros_hbm, out_hbm, idx_v, pad_v, rows_v, zero_v):
        wid = lax.axis_index("s") * n_cores + lax.axis_index("c")
        pltpu.sync_copy(dest_hbm.at[wid], idx_v)
        pltpu.sync_copy(pad_hbm.at[wid], pad_v)
        pltpu.sync_copy(zeros_hbm, zero_v)

        @pl.loop(0, n_pad_chunks)
        def _(c):
            pltpu.sync_copy(zero_v, out_hbm.at[pad_v.at[c]])

        @pl.loop(0, n_chunks)
        def _(c):
            pltpu.sync_copy(src_hbm.at[pl.ds(wid * per_w + c * SC_ROWS, SC_ROWS)], rows_v)
            for k in range(TOP_K):
                pltpu.sync_copy(rows_v, out_hbm.at[idx_v.at[c, k]])

    dest_w = dest_t.reshape(TOP_K, n_workers, n_chunks, SC_ROWS).transpose(1, 2, 0, 3)
    return run(src, dest_w, pad_idx.reshape(n_workers, n_pad_chunks, SC_ROWS),
               jnp.zeros((SC_ROWS, w), jnp.int32))


def _sc_gather(table, idx):
    n_rows = idx.shape[0]
    w = table.shape[1]
    mesh, n_cores, n_workers = _sc_mesh()
    per_w = n_rows // n_workers
    n_chunks = per_w // SC_ROWS
    assert n_chunks % 2 == 0 and n_chunks * SC_ROWS * n_workers == n_rows

    @functools.partial(
        pl.kernel, mesh=mesh,
        out_type=jax.ShapeDtypeStruct((n_rows, w), table.dtype),
        scratch_types=[pltpu.VMEM((n_chunks, SC_ROWS), jnp.int32),
                       pltpu.VMEM((SC_ROWS, w), table.dtype),
                       pltpu.VMEM((SC_ROWS, w), table.dtype),
                       pltpu.SemaphoreType.DMA, pltpu.SemaphoreType.DMA],
        name="sc_gather",
    )
    def run(table_hbm, idx_hbm, out_hbm, idx_v, buf0, buf1, sem0, sem1):
        wid = lax.axis_index("s") * n_cores + lax.axis_index("c")
        base = wid * per_w
        pltpu.sync_copy(idx_hbm.at[wid], idx_v)

        def gather(c, buf, sem):
            return pltpu.make_async_copy(table_hbm.at[idx_v.at[c]], buf, sem)

        gather(0, buf0, sem0).start()

        @pl.loop(0, n_chunks, step=2)
        def _(c):
            gather(c + 1, buf1, sem1).start()
            gather(c, buf0, sem0).wait()
            pltpu.sync_copy(buf0, out_hbm.at[pl.ds(base + c * SC_ROWS, SC_ROWS)])

            @pl.when(c + 2 < n_chunks)
            def _():
                gather(c + 2, buf0, sem0).start()

            gather(c + 1, buf1, sem1).wait()
            pltpu.sync_copy(buf1, out_hbm.at[pl.ds(base + (c + 1) * SC_ROWS, SC_ROWS)])

    return run(table, idx.reshape(n_workers, n_chunks, SC_ROWS))


def _expert_kernel(be_ref, nu_ref, xs_ref, wgu_ref, bgu_ref, wd_ref, bd_ref, y_ref,
                   wgu_bf, wd_bf):
    i = pl.program_id(0)
    f = wd_ref.shape[1]
    half = xs_ref.shape[1]
    live = i < nu_ref[0]
    first = (i == 0) | (be_ref[i] != be_ref[jnp.maximum(i - 1, 0)])

    def mlp(w_gu_lo, w_gu_hi, w_d):
        x_lo, x_hi = _unpack_bf16_pairs(xs_ref[...])
        gu = _dot(x_lo.astype(BF16), w_gu_lo) + _dot(x_hi.astype(BF16), w_gu_hi) + bgu_ref[0]
        gate = jnp.minimum(gu[:, :f], SWIGLU_LIMIT)
        up = jnp.clip(gu[:, f:], -SWIGLU_LIMIT, SWIGLU_LIMIT)
        act = (up + 1.0) * (gate * jax.nn.sigmoid(SWIGLU_ALPHA * gate))
        y = _dot(act.astype(BF16), w_d) + bd_ref[0]
        y_ref[...] = _pack_bf16_pairs(y)

    @pl.when(live & first)
    def _():
        w_gu_lo = wgu_ref[0, 0:half, :].astype(BF16)
        w_gu_hi = wgu_ref[0, half:2 * half, :].astype(BF16)
        w_d = wd_ref[0].astype(BF16)
        wgu_bf[0:half, :] = w_gu_lo
        wgu_bf[half:2 * half, :] = w_gu_hi
        wd_bf[...] = w_d
        mlp(w_gu_lo, w_gu_hi, w_d)

    @pl.when(live & jnp.logical_not(first))
    def _():
        mlp(wgu_bf[0:half, :], wgu_bf[half:2 * half, :], wd_bf[...])


def _experts(block_e, n_used, xs, n_blocks, w_gu, b_gu, w_down, b_down):
    half = xs.shape[1]
    d = 2 * half
    f2 = w_gu.shape[2]
    f = w_down.shape[1]

    def blk(i, be, nu):
        return (jnp.minimum(i, nu[0] - 1), 0)

    def expert(i, be, nu):
        return (be[jnp.minimum(i, nu[0] - 1)], 0, 0)

    grid_spec = pltpu.PrefetchScalarGridSpec(
        num_scalar_prefetch=2,
        grid=(n_blocks,),
        in_specs=[pl.BlockSpec((EXPERT_BLOCK, half), blk),
                  pl.BlockSpec((1, d, f2), expert),
                  pl.BlockSpec((1, 1, f2), expert),
                  pl.BlockSpec((1, f, d), expert),
                  pl.BlockSpec((1, 1, d), expert)],
        out_specs=pl.BlockSpec((EXPERT_BLOCK, half), blk),
        scratch_shapes=[pltpu.VMEM((d, f2), BF16), pltpu.VMEM((f, d), BF16)],
    )
    return pl.pallas_call(
        _expert_kernel,
        grid_spec=grid_spec,
        out_shape=jax.ShapeDtypeStruct((n_blocks * EXPERT_BLOCK, half), jnp.int32),
        compiler_params=pltpu.CompilerParams(dimension_semantics=("arbitrary",),
                                             vmem_limit_bytes=VMEM_LIMIT),
        name="experts",
    )(block_e, n_used, xs, w_gu, b_gu, w_down, b_down)


def _combine_kernel(yg_ref, x1_ref, gate_ref, gfin_ref, out_ref):
    gates = gate_ref[...]
    half = yg_ref.shape[2]
    acc_lo = x1_ref[:, :half]
    acc_hi = x1_ref[:, half:]
    for k in range(TOP_K):
        y_lo, y_hi = _unpack_bf16_pairs(yg_ref[k])
        acc_lo = acc_lo + gates[:, k:k + 1] * y_lo
        acc_hi = acc_hi + gates[:, k:k + 1] * y_hi
    out_ref[...] = _rms(jnp.concatenate([acc_lo, acc_hi], axis=1), gfin_ref[...])


def _combine(yg, x1, gates, g_final, tm):
    n, d = x1.shape
    return pl.pallas_call(
        _combine_kernel,
        grid=(n // tm,),
        in_specs=[pl.BlockSpec((TOP_K, tm, d // 2), lambda i: (0, i, 0)),
                  pl.BlockSpec((tm, d), lambda i: (i, 0)),
                  pl.BlockSpec((tm, LANES), lambda i: (i, 0)),
                  pl.BlockSpec((1, d), lambda i: (0, 0))],
        out_specs=pl.BlockSpec((tm, d), lambda i: (i, 0)),
        out_shape=jax.ShapeDtypeStruct((n, d), F32),
        compiler_params=pltpu.CompilerParams(dimension_semantics=("parallel",),
                                             vmem_limit_bytes=VMEM_LIMIT),
        name="combine",
    )(yg, x1, gates, g_final)


def _pack_in_proj(w_in, d):
    offs = [0]
    for width in (Q_LORA, KV_LORA, IDX_DIM, IDX_HEADS, SGU_WIDTH, SGU_WIDTH, d, d):
        offs.append(offs[-1] + width)
    wq, wkv, wki, wwi, wu, wv, wga, wgb = [w_in[:, offs[j]:offs[j + 1]] for j in range(8)]
    wki_rep = jnp.tile(wki, (1, IDX_HEADS))
    wwi_pad = jnp.pad(wwi, ((0, 0), (0, LANES - IDX_HEADS)))
    return jnp.concatenate([wq, wkv, wki_rep, wwi_pad, wu, wv, wga, wgb], axis=1).astype(BF16)


def _layer(x, g_mix, w_in, g_cq, g_ckv, w_uq, w_uv, w_q_idx, g_kidx, b_kidx, g_sgu, b_sgu,
           w_spatial, b_spatial, w_br_a, w_br_b, w_o, g_ffn, w_router, b_router, w_gu, b_gu,
           w_down, b_down, g_final):
    b, s, d = x.shape
    n = b * s
    x2 = x.reshape(n, d)
    row = lambda v: v.reshape(1, -1).astype(F32)

    w_in_p = _pack_in_proj(w_in, d)
    ws_pair = w_spatial.reshape(SGU_GROUPS // 2, 2, SGU_CHUNK, SGU_CHUNK).transpose(
        0, 2, 1, 3).reshape(SGU_GROUPS // 2, SGU_CHUNK, 2 * SGU_CHUNK)
    bsp = jnp.repeat(b_spatial.T, SGU_GROUP_DIM, axis=1)
    cq, ckv, ki, wi, gas, pb = _inproj(
        x2, row(g_mix), w_in_p, row(g_cq), row(g_ckv), row(jnp.tile(g_kidx, IDX_HEADS)),
        row(jnp.tile(b_kidx, IDX_HEADS)), row(g_sgu), row(b_sgu), ws_pair, bsp,
        w_br_b.astype(BF16), tm=512)

    head_eye = jnp.eye(N_HEADS, dtype=F32)
    w_uv_pad = (w_uv[:, :, None, :] * head_eye[:, None, :, None]).reshape(
        N_HEADS, KV_LORA, N_HEADS * HEAD_DIM)
    n_buckets = next(nbk for nbk in (8, 4, 2, 1) if (s // Q_BLOCK) % nbk == 0)
    ya = _attn(cq.reshape(b, s, -1), ckv.reshape(b, s, -1), ki.reshape(b, s, -1),
               wi.reshape(b, s, -1), w_uq.astype(BF16), w_q_idx.astype(BF16),
               w_uv_pad.astype(BF16), n_buckets)

    wr_hi = w_router.astype(BF16)
    wr_lo = (w_router - wr_hi.astype(F32)).astype(BF16)
    x1, h2p, idx_t, gate_p, rank_t, counts = _merge(
        x2, ya.reshape(n, -1), gas, pb, w_br_a.astype(BF16), w_o.astype(BF16), row(g_ffn),
        wr_hi, wr_lo, row(b_router), tm=512)

    counts = counts[0]
    padded = (counts + EXPERT_BLOCK - 1) // EXPERT_BLOCK * EXPERT_BLOCK
    padded_end = jnp.cumsum(padded)
    padded_start = padded_end - padded
    expert_ids = jnp.arange(N_EXPERTS, dtype=jnp.int32)[:, None, None]
    group_start = jnp.sum(jnp.where(idx_t[None, :TOP_K] == expert_ids,
                                    padded_start[:, None, None], 0), axis=0)
    dest_t = group_start + rank_t[:TOP_K]
    nk = n * TOP_K
    nb = -(-(nk + N_EXPERTS * EXPERT_BLOCK) // EXPERT_BLOCK)
    block_row0 = jnp.arange(nb, dtype=jnp.int32) * EXPERT_BLOCK
    block_e = jnp.minimum(
        jnp.sum((padded_end[None, :] <= block_row0[:, None]).astype(jnp.int32), axis=1),
        N_EXPERTS - 1).astype(jnp.int32)
    n_used = (padded_end[-1:] // EXPERT_BLOCK).astype(jnp.int32)
    j = jnp.arange(EXPERT_BLOCK, dtype=jnp.int32)[None, :]
    pad_idx = jnp.where(j < (padded - counts)[:, None], (padded_start + counts)[:, None] + j,
                        nb * EXPERT_BLOCK + j).astype(jnp.int32)

    xs = _sc_dispatch(h2p, dest_t, pad_idx, (nb + 1) * EXPERT_BLOCK)
    ybuf = _experts(block_e, n_used, xs, nb, w_gu, b_gu.reshape(N_EXPERTS, 1, -1), w_down,
                    b_down.reshape(N_EXPERTS, 1, -1))
    yg = _sc_gather(ybuf, dest_t.reshape(-1))
    out = _combine(yg.reshape(TOP_K, n, d // 2), x1, gate_p, row(g_final), tm=512)
    return out.reshape(b, s, d)


def kernel(x, g_mix, w_in, g_cq, g_ckv, w_uq, w_uv, w_q_idx, g_kidx, b_kidx, g_sgu, b_sgu,
           w_spatial, b_spatial, w_br_a, w_br_b, w_o, g_ffn, w_router, b_router, w_gu, b_gu,
           w_down, b_down, g_final):
    assert g_mix.shape[0] == 1, "single-layer block"
    return _layer(x, g_mix[0], w_in[0], g_cq[0], g_ckv[0], w_uq[0], w_uv[0], w_q_idx[0],
                  g_kidx[0], b_kidx[0], g_sgu[0], b_sgu[0], w_spatial[0], b_spatial[0],
                  w_br_a[0], w_br_b[0], w_o[0], g_ffn[0], w_router[0], b_router[0], w_gu[0],
                  b_gu[0], w_down[0], b_down[0], g_final)
```

```python
import functools

import jax
import jax.numpy as jnp
from jax import lax
from jax.experimental import pallas as pl
from jax.experimental.pallas import tpu as pltpu
from jax.experimental.pallas import tpu_sc as plsc

EPS = 1e-6
CHUNK = 64
N_HEADS = 8
HEAD_DIM = 64
Q_LORA = 256
KV_LORA = 128
IDX_HEADS = 8
IDX_DIM = 32
TOPK_MAX = 256
Q_BLOCK = 256
ATTN_SCALE = KV_LORA ** -0.5
IDX_SCALE = (IDX_HEADS * IDX_DIM) ** -0.5
SGU_CHUNK = 128
SGU_GROUPS = 8
SGU_WIDTH = 512
SGU_GROUP_DIM = SGU_WIDTH // SGU_GROUPS
N_EXPERTS = 32
TOP_K = 4
SWIGLU_LIMIT = 7.0
SWIGLU_ALPHA = 1.702
EXPERT_BLOCK = 512

ROW_TILE = 1024
SC_ROWS = 32
LANES = 128
SUBLANES = 8
VMEM_LIMIT = 56 * 1024 * 1024
INT_MIN = -(2 ** 31)
CODE_NEG_INF = INT_MIN + 0x7FFFFF

BF16 = jnp.bfloat16
F32 = jnp.float32


def _dot(a, b):
    return jnp.dot(a, b, preferred_element_type=F32)


def _dot_nt(a, b):
    return lax.dot_general(a, b, (((1,), (1,)), ((), ())), preferred_element_type=F32)


def _rms(x, g):
    return x * lax.rsqrt(jnp.mean(x * x, axis=-1, keepdims=True) + EPS) * g


def _pack_bf16_pairs(x):
    half = x.shape[1] // 2
    bits = lax.bitcast_convert_type(x.astype(BF16).astype(F32), jnp.int32)
    return lax.shift_right_logical(bits[:, :half], 16) | (bits[:, half:] & jnp.int32(-65536))


def _unpack_bf16_pairs(words):
    lo = lax.bitcast_convert_type(lax.shift_left(words, 16), F32)
    hi = lax.bitcast_convert_type(words & jnp.int32(-65536), F32)
    return lo, hi


def _layer_norm(x, g, b):
    mu = jnp.mean(x, axis=-1, keepdims=True)
    xc = x - mu
    var = jnp.mean(xc * xc, axis=-1, keepdims=True)
    return xc * lax.rsqrt(var + EPS) * g + b


C_Q = 0
C_KV = C_Q + Q_LORA
C_KI = C_KV + KV_LORA
C_WI = C_KI + IDX_HEADS * IDX_DIM
C_U = C_WI + LANES
C_V = C_U + SGU_WIDTH
D_IN_P_BASE = C_V + SGU_WIDTH


def _inproj_kernel(x_ref, gmix_ref, w_ref, gcq_ref, gckv_ref, gki_ref, bki_ref,
                   gsgu_ref, bsgu_ref, wsp_ref, bsp_ref, wbrb_ref,
                   cq_ref, ckv_ref, ki_ref, wi_ref, gas_ref, pb_ref):
    tm, d = x_ref.shape
    c_ga = D_IN_P_BASE
    c_gb = c_ga + d
    h = _rms(x_ref[...], gmix_ref[...]).astype(BF16)

    def proj(lo, width):
        return _dot(h, w_ref[:, lo:lo + width])

    cq_ref[...] = _rms(proj(C_Q, Q_LORA), gcq_ref[...]).astype(BF16)
    ckv_ref[...] = _rms(proj(C_KV, KV_LORA), gckv_ref[...]).astype(BF16)
    ki_ref[...] = _layer_norm(proj(C_KI, IDX_HEADS * IDX_DIM), gki_ref[...],
                              bki_ref[...]).astype(BF16)
    wi_ref[...] = proj(C_WI, LANES) * IDX_SCALE
    gas_ref[...] = jax.nn.sigmoid(proj(c_ga, d)).astype(BF16)

    u = jax.nn.gelu(proj(C_U, SGU_WIDTH))
    v = _layer_norm(jax.nn.gelu(proj(C_V, SGU_WIDTH)), gsgu_ref[...], bsgu_ref[...])

    row = lax.broadcasted_iota(jnp.int32, (SGU_CHUNK, 2 * SGU_CHUNK), 0)
    col = lax.broadcasted_iota(jnp.int32, (SGU_CHUNK, 2 * SGU_CHUNK), 1) % SGU_CHUNK
    causal = (row // CHUNK) >= (col // CHUNK)
    lane = lax.broadcasted_iota(jnp.int32, (SGU_CHUNK, LANES), 1)
    left = lane < SGU_GROUP_DIM
    n_tiles = SGU_WIDTH // LANES
    ws = [jnp.where(causal, wsp_ref[j], 0.0).astype(BF16) for j in range(n_tiles)]
    yb_chunks = []
    for c in range(tm // SGU_CHUNK):
        tiles = []
        for j in range(n_tiles):
            blk = v[c * SGU_CHUNK:(c + 1) * SGU_CHUNK, j * LANES:(j + 1) * LANES]
            stacked = jnp.concatenate(
                [jnp.where(left, blk, 0.0), jnp.where(left, 0.0, blk)], axis=0).astype(BF16)
            tiles.append(_dot(ws[j], stacked))
        s = jnp.concatenate(tiles, axis=1) + bsp_ref[...]
        yb_chunks.append(u[c * SGU_CHUNK:(c + 1) * SGU_CHUNK, :] * s)
    yb = jnp.concatenate(yb_chunks, axis=0).astype(BF16)
    pb_ref[...] = (jax.nn.sigmoid(proj(c_gb, d)) * _dot(yb, wbrb_ref[...])).astype(BF16)


def _inproj(x2, g_mix, w_in_p, g_cq, g_ckv, g_ki, b_ki, g_sgu, b_sgu, ws_pair, bsp, w_br_b,
            tm):
    n, d = x2.shape
    d_in_p = w_in_p.shape[1]

    def full(shape):
        return pl.BlockSpec(shape, lambda i: (0,) * len(shape))

    def rows(width):
        return pl.BlockSpec((tm, width), lambda i: (i, 0))

    return pl.pallas_call(
        _inproj_kernel,
        grid=(n // tm,),
        in_specs=[rows(d), full((1, d)), full((d, d_in_p)), full((1, Q_LORA)),
                  full((1, KV_LORA)), full((1, IDX_HEADS * IDX_DIM)),
                  full((1, IDX_HEADS * IDX_DIM)), full((1, SGU_WIDTH)), full((1, SGU_WIDTH)),
                  full(ws_pair.shape), full(bsp.shape), full(w_br_b.shape)],
        out_specs=[rows(Q_LORA), rows(KV_LORA), rows(IDX_HEADS * IDX_DIM), rows(LANES),
                   rows(d), rows(d)],
        out_shape=[jax.ShapeDtypeStruct((n, Q_LORA), BF16),
                   jax.ShapeDtypeStruct((n, KV_LORA), BF16),
                   jax.ShapeDtypeStruct((n, IDX_HEADS * IDX_DIM), BF16),
                   jax.ShapeDtypeStruct((n, LANES), F32),
                   jax.ShapeDtypeStruct((n, d), BF16),
                   jax.ShapeDtypeStruct((n, d), BF16)],
        compiler_params=pltpu.CompilerParams(dimension_semantics=("parallel",),
                                             vmem_limit_bytes=VMEM_LIMIT),
        name="inproj",
    )(x2, g_mix, w_in_p, g_cq, g_ckv, g_ki, b_ki, g_sgu, b_sgu, ws_pair, bsp, w_br_b)


def _attn_body(s_eff, topk, q_block0, cq_ref, ckv_ref, ki_ref, wi_ref, wuq_ref, wqi_ref,
               wuv_ref, ya_ref, q_scr, qi_scr, wcol_scr, sc_scr, bias_scr, acc_scr, s_buf,
               p_buf, kva_scr, sc16_scr):
    i = pl.program_id(1) + q_block0
    cq = cq_ref[0]
    q = _dot(cq, wuq_ref[...])
    for h in range(N_HEADS):
        q_scr[h] = (q[:, h * KV_LORA:(h + 1) * KV_LORA] * ATTN_SCALE).astype(BF16)

    rowq = lax.broadcasted_iota(jnp.int32, (Q_BLOCK, s_eff), 0)
    colk = lax.broadcasted_iota(jnp.int32, (Q_BLOCK, s_eff), 1)
    allowed = (colk // CHUNK) <= ((i * Q_BLOCK + rowq) // CHUNK)

    if s_eff <= topk:
        bias_scr[...] = jnp.where(allowed, 0.0, -jnp.inf)
    else:
        _select_topk(s_eff, topk, cq, allowed, colk, ki_ref, wi_ref, wqi_ref, qi_scr, wcol_scr,
                     sc_scr, bias_scr, s_buf, sc16_scr)

    acc_scr[...] = jnp.zeros_like(acc_scr)

    def logits(h):
        s_buf[h % 2] = _dot_nt(q_scr[h], ckv_ref[0, 0:s_eff, :]) + bias_scr[...]

    def softmax(h):
        s = s_buf[h % 2]
        p_buf[h % 2] = jnp.exp(s - jnp.max(s, axis=1, keepdims=True)).astype(BF16)

    def values(h):
        ol = _dot(p_buf[h % 2], kva_scr[0:s_eff, :])
        o = ol[:, :KV_LORA] / ol[:, KV_LORA:]
        acc_scr[...] += _dot(o.astype(BF16), wuv_ref[h])

    logits(0)
    for h in range(N_HEADS):
        if h + 1 < N_HEADS:
            logits(h + 1)
        softmax(h)
        values(h)
    ya_ref[0] = acc_scr[...].astype(BF16)


def _select_topk(s_eff, topk, cq, allowed, colk, ki_ref, wi_ref, wqi_ref, qi_scr, wcol_scr,
                 sc_scr, bias_scr, s_buf, sc16_scr):
    qi = _dot(cq, wqi_ref[...]).astype(BF16)
    wi = wi_ref[0]
    head_of_lane = lax.broadcasted_iota(jnp.int32, qi.shape, 1) // IDX_DIM
    for h in range(IDX_HEADS):
        qi_scr[h] = jnp.where(head_of_lane == h, qi, jnp.zeros_like(qi))
        wcol_scr[h] = wi[:, h:h + 1]

    def index_dots(h):
        s_buf[h % 2] = _dot_nt(qi_scr[h], ki_ref[0, 0:s_eff, :])

    index_dots(0)
    for h in range(IDX_HEADS):
        if h + 1 < IDX_HEADS:
            index_dots(h + 1)
        term = wcol_scr[h] * jnp.maximum(s_buf[h % 2], 0.0)
        bias_scr[...] = term if h == 0 else bias_scr[...] + term

    sc_scr[...] = jnp.where(allowed, bias_scr[...], -jnp.inf)

    def code_to_bits(code):
        return jnp.where(code < 0, code ^ jnp.int32(0x7FFFFFFF), code)

    def code_to_float(code):
        return lax.bitcast_convert_type(code_to_bits(code), F32)

    upper_half = jnp.int32(-65536)
    sc_bits = lax.bitcast_convert_type(sc_scr[...], jnp.int32)
    sc16_scr[...] = lax.bitcast_convert_type(sc_bits & upper_half, F32).astype(BF16)

    def count_ge_upper(cand):
        thr16 = lax.bitcast_convert_type(code_to_bits(cand) & upper_half, F32).astype(BF16)
        part = None
        for t in range(s_eff // LANES):
            tile = sc16_scr[:, t * LANES:(t + 1) * LANES]
            ind = jnp.where(tile >= thr16, jnp.ones((), BF16), jnp.zeros((), BF16))
            part = ind if part is None else part + ind
        return jnp.sum(part.astype(F32), axis=1, keepdims=True)

    def count_ge(cand):
        return jnp.sum(jnp.where(sc_scr[...] >= code_to_float(cand), 1.0, 0.0), axis=1,
                       keepdims=True)

    kf = float(topk)
    code = jnp.where(count_ge_upper(jnp.zeros((Q_BLOCK, 1), jnp.int32)) >= kf,
                     jnp.int32(0), jnp.int32(INT_MIN))
    for bit in range(30, -1, -1):
        cand = code + jnp.int32(1 << bit)
        cnt = count_ge_upper(cand) if bit >= 16 else count_ge(cand)
        feasible = (cnt >= kf) | (cand <= jnp.int32(CODE_NEG_INF))
        code = jnp.where(feasible, cand, code)

    thr = code_to_float(code)
    thr_up = code_to_float(code + 1)
    sel = sc_scr[...] >= thr
    n_ge = jnp.sum(jnp.where(sel, 1.0, 0.0), axis=1, keepdims=True)
    bias_scr[...] = jnp.where(sel & allowed, 0.0, -jnp.inf)
    tie_rows = (n_ge > kf) & (code > jnp.int32(CODE_NEG_INF))
    any_tie = jnp.max(jnp.where(tie_rows, 1.0, 0.0))

    @pl.when(any_tie > 0.0)
    def _():
        sc = sc_scr[...]
        gt = sc >= thr_up
        eq = (sc >= thr) & jnp.logical_not(gt)
        need = kf - jnp.sum(jnp.where(gt, 1.0, 0.0), axis=1, keepdims=True)
        nbits = (s_eff - 1).bit_length()
        bias_scr[...] = jnp.where(eq, 1.0, 0.0)

        def tie_step(it, j):
            cand = j + jnp.left_shift(jnp.int32(1), jnp.int32(nbits - 1) - it)
            cnt = jnp.sum(jnp.where(colk < cand, bias_scr[...], 0.0), axis=1, keepdims=True)
            return jnp.where(cnt < need, cand, j)

        jmax = lax.fori_loop(0, nbits, tie_step, jnp.zeros((Q_BLOCK, 1), jnp.int32))
        keep = gt | (eq & (colk <= jmax))
        bias_scr[...] = jnp.where(keep & allowed, 0.0, -jnp.inf)


def _attn_kernel(s_eff, topk, q_block0, cq_ref, ckv_ref, ki_ref, wi_ref, wuq_ref, wqi_ref,
                 wuv_ref, ya_ref, q_scr, qi_scr, wcol_scr, sc_scr, bias_scr, acc_scr, s_buf,
                 p_buf, kva_scr, sc16_scr):
    @pl.when(pl.program_id(1) == 0)
    def _():
        kva_scr[:, 0:KV_LORA] = ckv_ref[0]
        kva_scr[:, KV_LORA:] = jnp.ones((s_eff, KV_LORA), BF16)

    _attn_body(s_eff, topk, q_block0, cq_ref, ckv_ref, ki_ref, wi_ref, wuq_ref, wqi_ref,
               wuv_ref, ya_ref, q_scr, qi_scr, wcol_scr, sc_scr, bias_scr, acc_scr, s_buf,
               p_buf, kva_scr, sc16_scr)


def _attn(cq, ckv, ki, wi, w_uq, w_qi, w_uv_pad, n_buckets):
    b, s, _ = cq.shape
    nq = s // Q_BLOCK
    per = nq // n_buckets
    topk = min(TOPK_MAX, s // 4)
    aw = N_HEADS * HEAD_DIM

    def full(shape):
        return pl.BlockSpec(shape, lambda bi, i: (0,) * len(shape))

    parts = []
    for k in range(n_buckets):
        s_eff = (k + 1) * per * Q_BLOCK
        q0 = k * per

        def q_rows(width, q0=q0):
            return pl.BlockSpec((1, Q_BLOCK, width), lambda bi, i: (bi, i + q0, 0))

        def keys(width, s_eff=s_eff):
            return pl.BlockSpec((1, s_eff, width), lambda bi, i: (bi, 0, 0))

        parts.append(pl.pallas_call(
            functools.partial(_attn_kernel, s_eff, topk, q0),
            grid=(b, per),
            in_specs=[q_rows(Q_LORA), keys(KV_LORA), keys(IDX_HEADS * IDX_DIM), q_rows(LANES),
                      full(w_uq.shape), full(w_qi.shape), full(w_uv_pad.shape)],
            out_specs=pl.BlockSpec((1, Q_BLOCK, aw), lambda bi, i: (bi, i, 0)),
            out_shape=jax.ShapeDtypeStruct((b, per * Q_BLOCK, aw), BF16),
            scratch_shapes=[pltpu.VMEM((N_HEADS, Q_BLOCK, KV_LORA), BF16),
                            pltpu.VMEM((IDX_HEADS, Q_BLOCK, IDX_HEADS * IDX_DIM), BF16),
                            pltpu.VMEM((IDX_HEADS, Q_BLOCK, 1), F32),
                            pltpu.VMEM((Q_BLOCK, s_eff), F32),
                            pltpu.VMEM((Q_BLOCK, s_eff), F32),
                            pltpu.VMEM((Q_BLOCK, aw), F32),
                            pltpu.VMEM((2, Q_BLOCK, s_eff), F32),
                            pltpu.VMEM((2, Q_BLOCK, s_eff), BF16),
                            pltpu.VMEM((s_eff, 2 * KV_LORA), BF16),
                            pltpu.VMEM((Q_BLOCK, s_eff), BF16)],
            compiler_params=pltpu.CompilerParams(
                dimension_semantics=("parallel", "arbitrary"), vmem_limit_bytes=VMEM_LIMIT),
            name=f"attn_keys{s_eff}",
        )(cq, ckv, ki, wi, w_uq, w_qi, w_uv_pad))
    return jnp.concatenate(parts, axis=1)


def _merge_kernel(x_ref, ya_ref, gas_ref, pb_ref, wbra_ref, wo_ref, gffn_ref,
                  wr_hi_ref, wr_lo_ref, br_ref,
                  x1_ref, h2p_ref, idx_ref, gate_ref, rank_ref, cnt_ref, base_scr):
    step = pl.program_id(0)
    tm, d = x_ref.shape

    @pl.when(step == 0)
    def _():
        base_scr[...] = jnp.zeros_like(base_scr)

    a = _dot(ya_ref[...], wbra_ref[...])
    merged = gas_ref[...].astype(F32) * a + pb_ref[...].astype(F32)
    x1 = x_ref[...] + _dot(merged.astype(BF16), wo_ref[...])
    x1_ref[...] = x1
    h2 = _rms(x1, gffn_ref[...])

    h_hi = h2.astype(BF16)
    h2p_ref[...] = _pack_bf16_pairs(h2)
    h_lo = (h2 - h_hi.astype(F32)).astype(BF16)
    logits = (_dot(h_hi, wr_hi_ref[...]) + _dot(h_hi, wr_lo_ref[...])
              + _dot(h_lo, wr_hi_ref[...]) + br_ref[...])

    lane_e = lax.broadcasted_iota(jnp.int32, (tm, N_EXPERTS), 1).astype(F32)
    lane_o = lax.broadcasted_iota(jnp.int32, (tm, LANES), 1)
    work = logits
    vals, idxs = [], []
    onehot = jnp.zeros((tm, N_EXPERTS), F32)
    for _ in range(TOP_K):
        m = jnp.max(work, axis=1, keepdims=True)
        idx = jnp.min(jnp.where(work == m, lane_e, float(N_EXPERTS)), axis=1, keepdims=True)
        hit = lane_e == idx
        onehot = onehot + jnp.where(hit, 1.0, 0.0)
        work = jnp.where(hit, -jnp.inf, work)
        vals.append(m)
        idxs.append(idx)
    exps = [jnp.exp(v - vals[0]) for v in vals]
    denom = exps[0] + exps[1] + exps[2] + exps[3]

    r = lax.broadcasted_iota(jnp.int32, (tm, tm), 0)
    c = lax.broadcasted_iota(jnp.int32, (tm, tm), 1)
    tri = jnp.where(c < r, 1.0, 0.0).astype(BF16)
    rank_full = _dot(tri, onehot.astype(BF16)) + base_scr[...]

    idx_out = jnp.zeros((tm, LANES), F32)
    gate_out = jnp.zeros((tm, LANES), F32)
    rank_out = jnp.zeros((tm, LANES), F32)
    for k in range(TOP_K):
        rk = jnp.sum(jnp.where(lane_e == idxs[k], rank_full, 0.0), axis=1, keepdims=True)
        idx_out = jnp.where(lane_o == k, idxs[k], idx_out)
        gate_out = jnp.where(lane_o == k, exps[k] / denom, gate_out)
        rank_out = jnp.where(lane_o == k, rk, rank_out)
    gate_ref[...] = gate_out
    idx_ref[...] = idx_out.T[0:SUBLANES, :].astype(jnp.int32)
    rank_ref[...] = rank_out.T[0:SUBLANES, :].astype(jnp.int32)

    base_scr[...] = base_scr[...] + jnp.sum(onehot, axis=0, keepdims=True)
    cnt_ref[...] = base_scr[...].astype(jnp.int32)


def _merge(x2, ya, gas, pb, w_br_a, w_o, g_ffn, wr_hi, wr_lo, b_router, tm):
    n, d = x2.shape
    aw = ya.shape[1]

    def full(shape):
        return pl.BlockSpec(shape, lambda i: (0,) * len(shape))

    def rows(width):
        return pl.BlockSpec((tm, width), lambda i: (i, 0))

    slots = pl.BlockSpec((SUBLANES, tm), lambda i: (0, i))
    return pl.pallas_call(
        _merge_kernel,
        grid=(n // tm,),
        in_specs=[rows(d), rows(aw), rows(d), rows(d), full(w_br_a.shape), full(w_o.shape),
                  full((1, d)), full(wr_hi.shape), full(wr_lo.shape), full((1, N_EXPERTS))],
        out_specs=[rows(d), rows(d // 2), slots, rows(LANES), slots, full((1, N_EXPERTS))],
        out_shape=[jax.ShapeDtypeStruct((n, d), F32),
                   jax.ShapeDtypeStruct((n, d // 2), jnp.int32),
                   jax.ShapeDtypeStruct((SUBLANES, n), jnp.int32),
                   jax.ShapeDtypeStruct((n, LANES), F32),
                   jax.ShapeDtypeStruct((SUBLANES, n), jnp.int32),
                   jax.ShapeDtypeStruct((1, N_EXPERTS), jnp.int32)],
        scratch_shapes=[pltpu.VMEM((1, N_EXPERTS), F32)],
        compiler_params=pltpu.CompilerParams(dimension_semantics=("arbitrary",),
                                             vmem_limit_bytes=VMEM_LIMIT),
        name="merge_route",
    )(x2, ya, gas, pb, w_br_a, w_o, g_ffn, wr_hi, wr_lo, b_router)


def _sc_mesh():
    info = plsc.get_sparse_core_info()
    mesh = plsc.VectorSubcoreMesh(core_axis_name="c", subcore_axis_name="s")
    return mesh, info.num_cores, info.num_cores * info.num_subcores


def _sc_dispatch(src, dest_t, pad_idx, n_out_rows):
    n, w = src.shape
    mesh, n_cores, n_workers = _sc_mesh()
    per_w = n // n_workers
    n_chunks = per_w // SC_ROWS
    n_pad_chunks = pad_idx.size // (n_workers * SC_ROWS)

    @functools.partial(
        pl.kernel, mesh=mesh,
        out_type=jax.ShapeDtypeStruct((n_out_rows, w), jnp.int32),
        scratch_types=[pltpu.VMEM((n_chunks, TOP_K, SC_ROWS), jnp.int32),
                       pltpu.VMEM((n_pad_chunks, SC_ROWS), jnp.int32),
                       pltpu.VMEM((SC_ROWS, w), jnp.int32),
                       pltpu.VMEM((SC_ROWS, w), jnp.int32)],
        name="sc_dispatch",
    )
    def run(src_hbm, dest_hbm, pad_hbm, zeros_hbm, out_hbm, idx_v, pad_v, rows_v, zero_v):
        wid = lax.axis_index("s") * n_cores + lax.axis_index("c")
        pltpu.sync_copy(dest_hbm.at[wid], idx_v)
        pltpu.sync_copy(pad_hbm.at[wid], pad_v)
        pltpu.sync_copy(zeros_hbm, zero_v)

        @pl.loop(0, n_pad_chunks)
        def _(c):
            pltpu.sync_copy(zero_v, out_hbm.at[pad_v.at[c]])

        @pl.loop(0, n_chunks)
        def _(c):
            pltpu.sync_copy(src_hbm.at[pl.ds(wid * per_w + c * SC_ROWS, SC_ROWS)], rows_v)
            for k in range(TOP_K):
                pltpu.sync_copy(rows_v, out_hbm.at[idx_v.at[c, k]])

    dest_w = dest_t.reshape(TOP_K, n_workers, n_chunks, SC_ROWS).transpose(1, 2, 0, 3)
    return run(src, dest_w, pad_idx.reshape(n_workers, n_pad_chunks, SC_ROWS),
               jnp.zeros((SC_ROWS, w), jnp.int32))


def _sc_gather(table, idx):
    n_rows = idx.shape[0]
    w = table.shape[1]
    mesh, n_cores, n_workers = _sc_mesh()
    per_w = n_rows // n_workers
    n_chunks = per_w // SC_ROWS
    assert n_chunks % 2 == 0 and n_chunks * SC_ROWS * n_workers == n_rows

    @functools.partial(
        pl.kernel, mesh=mesh,
        out_type=jax.ShapeDtypeStruct((n_rows, w), table.dtype),
        scratch_types=[pltpu.VMEM((n_chunks, SC_ROWS), jnp.int32),
                       pltpu.VMEM((SC_ROWS, w), table.dtype),
                       pltpu.VMEM((SC_ROWS, w), table.dtype),
                       pltpu.SemaphoreType.DMA, pltpu.SemaphoreType.DMA],
        name="sc_gather",
    )
    def run(table_hbm, idx_hbm, out_hbm, idx_v, buf0, buf1, sem0, sem1):
        wid = lax.axis_index("s") * n_cores + lax.axis_index("c")
        base = wid * per_w
        pltpu.sync_copy(idx_hbm.at[wid], idx_v)

        def gather(c, buf, sem):
            return pltpu.make_async_copy(table_hbm.at[idx_v.at[c]], buf, sem)

        gather(0, buf0, sem0).start()

        @pl.loop(0, n_chunks, step=2)
        def _(c):
            gather(c + 1, buf1, sem1).start()
            gather(c, buf0, sem0).wait()
            pltpu.sync_copy(buf0, out_hbm.at[pl.ds(base + c * SC_ROWS, SC_ROWS)])

            @pl.when(c + 2 < n_chunks)
            def _():
                gather(c + 2, buf0, sem0).start()

            gather(c + 1, buf1, sem1).wait()
            pltpu.sync_copy(buf1, out_hbm.at[pl.ds(base + (c + 1) * SC_ROWS, SC_ROWS)])

    return run(table, idx.reshape(n_workers, n_chunks, SC_ROWS))


def _expert_kernel(be_ref, nu_ref, xs_ref, wgu_ref, bgu_ref, wd_ref, bd_ref, y_ref,
                   wgu_bf, wd_bf):
    i = pl.program_id(0)
    f = wd_ref.shape[1]
    half = xs_ref.shape[1]
    live = i < nu_ref[0]
    first = (i == 0) | (be_ref[i] != be_ref[jnp.maximum(i - 1, 0)])

    def mlp(w_gu_lo, w_gu_hi, w_d):
        x_lo, x_hi = _unpack_bf16_pairs(xs_ref[...])
        gu = _dot(x_lo.astype(BF16), w_gu_lo) + _dot(x_hi.astype(BF16), w_gu_hi) + bgu_ref[0]
        gate = jnp.minimum(gu[:, :f], SWIGLU_LIMIT)
        up = jnp.clip(gu[:, f:], -SWIGLU_LIMIT, SWIGLU_LIMIT)
        act = (up + 1.0) * (gate * jax.nn.sigmoid(SWIGLU_ALPHA * gate))
        y = _dot(act.astype(BF16), w_d) + bd_ref[0]
        y_ref[...] = _pack_bf16_pairs(y)

    @pl.when(live & first)
    def _():
        w_gu_lo = wgu_ref[0, 0:half, :].astype(BF16)
        w_gu_hi = wgu_ref[0, half:2 * half, :].astype(BF16)
        w_d = wd_ref[0].astype(BF16)
        wgu_bf[0:half, :] = w_gu_lo
        wgu_bf[half:2 * half, :] = w_gu_hi
        wd_bf[...] = w_d
        mlp(w_gu_lo, w_gu_hi, w_d)

    @pl.when(live & jnp.logical_not(first))
    def _():
        mlp(wgu_bf[0:half, :], wgu_bf[half:2 * half, :], wd_bf[...])


def _experts(block_e, n_used, xs, n_blocks, w_gu, b_gu, w_down, b_down):
    half = xs.shape[1]
    d = 2 * half
    f2 = w_gu.shape[2]
    f = w_down.shape[1]

    def blk(i, be, nu):
        return (jnp.minimum(i, nu[0] - 1), 0)

    def expert(i, be, nu):
        return (be[jnp.minimum(i, nu[0] - 1)], 0, 0)

    grid_spec = pltpu.PrefetchScalarGridSpec(
        num_scalar_prefetch=2,
        grid=(n_blocks,),
        in_specs=[pl.BlockSpec((EXPERT_BLOCK, half), blk),
                  pl.BlockSpec((1, d, f2), expert),
                  pl.BlockSpec((1, 1, f2), expert),
                  pl.BlockSpec((1, f, d), expert),
                  pl.BlockSpec((1, 1, d), expert)],
        out_specs=pl.BlockSpec((EXPERT_BLOCK, half), blk),
        scratch_shapes=[pltpu.VMEM((d, f2), BF16), pltpu.VMEM((f, d), BF16)],
    )
    return pl.pallas_call(
        _expert_kernel,
        grid_spec=grid_spec,
        out_shape=jax.ShapeDtypeStruct((n_blocks * EXPERT_BLOCK, half), jnp.int32),
        compiler_params=pltpu.CompilerParams(dimension_semantics=("arbitrary",),
                                             vmem_limit_bytes=VMEM_LIMIT),
        name="experts",
    )(block_e, n_used, xs, w_gu, b_gu, w_down, b_down)


def _combine_kernel(yg_ref, x1_ref, gate_ref, gfin_ref, out_ref):
    gates = gate_ref[...]
    half = yg_ref.shape[2]
    acc_lo = x1_ref[:, :half]
    acc_hi = x1_ref[:, half:]
    for k in range(TOP_K):
        y_lo, y_hi = _unpack_bf16_pairs(yg_ref[k])
        acc_lo = acc_lo + gates[:, k:k + 1] * y_lo
        acc_hi = acc_hi + gates[:, k:k + 1] * y_hi
    out_ref[...] = _rms(jnp.concatenate([acc_lo, acc_hi], axis=1), gfin_ref[...])


def _combine(yg, x1, gates, g_final, tm):
    n, d = x1.shape
    return pl.pallas_call(
        _combine_kernel,
        grid=(n // tm,),
        in_specs=[pl.BlockSpec((TOP_K, tm, d // 2), lambda i: (0, i, 0)),
                  pl.BlockSpec((tm, d), lambda i: (i, 0)),
                  pl.BlockSpec((tm, LANES), lambda i: (i, 0)),
                  pl.BlockSpec((1, d), lambda i: (0, 0))],
        out_specs=pl.BlockSpec((tm, d), lambda i: (i, 0)),
        out_shape=jax.ShapeDtypeStruct((n, d), F32),
        compiler_params=pltpu.CompilerParams(dimension_semantics=("parallel",),
                                             vmem_limit_bytes=VMEM_LIMIT),
        name="combine",
    )(yg, x1, gates, g_final)


def _pack_in_proj(w_in, d):
    offs = [0]
    for width in (Q_LORA, KV_LORA, IDX_DIM, IDX_HEADS, SGU_WIDTH, SGU_WIDTH, d, d):
        offs.append(offs[-1] + width)
    wq, wkv, wki, wwi, wu, wv, wga, wgb = [w_in[:, offs[j]:offs[j + 1]] for j in range(8)]
    wki_rep = jnp.tile(wki, (1, IDX_HEADS))
    wwi_pad = jnp.pad(wwi, ((0, 0), (0, LANES - IDX_HEADS)))
    return jnp.concatenate([wq, wkv, wki_rep, wwi_pad, wu, wv, wga, wgb], axis=1).astype(BF16)


def _layer(x, g_mix, w_in, g_cq, g_ckv, w_uq, w_uv, w_q_idx, g_kidx, b_kidx, g_sgu, b_sgu,
           w_spatial, b_spatial, w_br_a, w_br_b, w_o, g_ffn, w_router, b_router, w_gu, b_gu,
           w_down, b_down, g_final):
    b, s, d = x.shape
    n = b * s
    x2 = x.reshape(n, d)
    row = lambda v: v.reshape(1, -1).astype(F32)

    w_in_p = _pack_in_proj(w_in, d)
    ws_pair = w_spatial.reshape(SGU_GROUPS // 2, 2, SGU_CHUNK, SGU_CHUNK).transpose(
        0, 2, 1, 3).reshape(SGU_GROUPS // 2, SGU_CHUNK, 2 * SGU_CHUNK)
    bsp = jnp.repeat(b_spatial.T, SGU_GROUP_DIM, axis=1)
    cq, ckv, ki, wi, gas, pb = _inproj(
        x2, row(g_mix), w_in_p, row(g_cq), row(g_ckv), row(jnp.tile(g_kidx, IDX_HEADS)),
        row(jnp.tile(b_kidx, IDX_HEADS)), row(g_sgu), row(b_sgu), ws_pair, bsp,
        w_br_b.astype(BF16), tm=min(ROW_TILE, n))

    head_eye = jnp.eye(N_HEADS, dtype=F32)
    w_uv_pad = (w_uv[:, :, None, :] * head_eye[:, None, :, None]).reshape(
        N_HEADS, KV_LORA, N_HEADS * HEAD_DIM)
    n_buckets = next(nbk for nbk in (8, 4, 2, 1) if (s // Q_BLOCK) % nbk == 0)
    ya = _attn(cq.reshape(b, s, -1), ckv.reshape(b, s, -1), ki.reshape(b, s, -1),
               wi.reshape(b, s, -1), w_uq.astype(BF16), w_q_idx.astype(BF16),
               w_uv_pad.astype(BF16), n_buckets)

    wr_hi = w_router.astype(BF16)
    wr_lo = (w_router - wr_hi.astype(F32)).astype(BF16)
    x1, h2p, idx_t, gate_p, rank_t, counts = _merge(
        x2, ya.reshape(n, -1), gas, pb, w_br_a.astype(BF16), w_o.astype(BF16), row(g_ffn),
        wr_hi, wr_lo, row(b_router), tm=min(ROW_TILE, n))

    counts = counts[0]
    padded = (counts + EXPERT_BLOCK - 1) // EXPERT_BLOCK * EXPERT_BLOCK
    padded_end = jnp.cumsum(padded)
    padded_start = padded_end - padded
    expert_ids = jnp.arange(N_EXPERTS, dtype=jnp.int32)[:, None, None]
    group_start = jnp.sum(jnp.where(idx_t[None, :TOP_K] == expert_ids,
                                    padded_start[:, None, None], 0), axis=0)
    dest_t = group_start + rank_t[:TOP_K]
    nk = n * TOP_K
    nb = -(-(nk + N_EXPERTS * EXPERT_BLOCK) // EXPERT_BLOCK)
    block_row0 = jnp.arange(nb, dtype=jnp.int32) * EXPERT_BLOCK
    block_e = jnp.minimum(
        jnp.sum((padded_end[None, :] <= block_row0[:, None]).astype(jnp.int32), axis=1),
        N_EXPERTS - 1).astype(jnp.int32)
    n_used = (padded_end[-1:] // EXPERT_BLOCK).astype(jnp.int32)
    j = jnp.arange(EXPERT_BLOCK, dtype=jnp.int32)[None, :]
    pad_idx = jnp.where(j < (padded - counts)[:, None], (padded_start + counts)[:, None] + j,
                        nb * EXPERT_BLOCK + j).astype(jnp.int32)

    xs = _sc_dispatch(h2p, dest_t, pad_idx, (nb + 1) * EXPERT_BLOCK)
    ybuf = _experts(block_e, n_used, xs, nb, w_gu, b_gu.reshape(N_EXPERTS, 1, -1), w_down,
                    b_down.reshape(N_EXPERTS, 1, -1))
    yg = _sc_gather(ybuf, dest_t.reshape(-1))
    out = _combine(yg.reshape(TOP_K, n, d // 2), x1, gate_p, row(g_final), tm=512)
    return out.reshape(b, s, d)


def kernel(x, g_mix, w_in, g_cq, g_ckv, w_uq, w_uv, w_q_idx, g_kidx, b_kidx, g_sgu, b_sgu,
           w_spatial, b_spatial, w_br_a, w_br_b, w_o, g_ffn, w_router, b_router, w_gu, b_gu,
           w_down, b_down, g_final):
    assert g_mix.shape[0] == 1, "single-layer block"
    return _layer(x, g_mix[0], w_in[0], g_cq[0], g_ckv[0], w_uq[0], w_uv[0], w_q_idx[0],
                  g_kidx[0], b_kidx[0], g_sgu[0], b_sgu[0], w_spatial[0], b_spatial[0],
                  w_br_a[0], w_br_b[0], w_o[0], g_ffn[0], w_router[0], b_router[0], w_gu[0],
                  b_gu[0], w_down[0], b_down[0], g_final)
```

```python
import functools

import jax
import jax.numpy as jnp
from jax import lax
from jax.experimental import pallas as pl
from jax.experimental.pallas import tpu as pltpu
from jax.experimental.pallas import tpu_sc as plsc

EPS = 1e-6
CHUNK = 64
N_HEADS = 8
HEAD_DIM = 64
Q_LORA = 256
KV_LORA = 128
IDX_HEADS = 8
IDX_DIM = 32
TOPK_MAX = 256
Q_BLOCK = 256
ATTN_SCALE = KV_LORA ** -0.5
IDX_SCALE = (IDX_HEADS * IDX_DIM) ** -0.5
SGU_CHUNK = 128
SGU_GROUPS = 8
SGU_WIDTH = 512
SGU_GROUP_DIM = SGU_WIDTH // SGU_GROUPS
N_EXPERTS = 32
TOP_K = 4
SWIGLU_LIMIT = 7.0
SWIGLU_ALPHA = 1.702
EXPERT_BLOCK = 512

ROW_TILE = 1024
SC_ROWS = 32
LANES = 128
SUBLANES = 8
VMEM_LIMIT = 56 * 1024 * 1024
INT_MIN = -(2 ** 31)
CODE_NEG_INF = INT_MIN + 0x7FFFFF

BF16 = jnp.bfloat16
F32 = jnp.float32


def _dot(a, b):
    return jnp.dot(a, b, preferred_element_type=F32)


def _dot_nt(a, b):
    return lax.dot_general(a, b, (((1,), (1,)), ((), ())), preferred_element_type=F32)


def _rms(x, g):
    return x * lax.rsqrt(jnp.mean(x * x, axis=-1, keepdims=True) + EPS) * g


def _pack_bf16_pairs(x):
    half = x.shape[1] // 2
    bits = lax.bitcast_convert_type(x.astype(BF16).astype(F32), jnp.int32)
    return lax.shift_right_logical(bits[:, :half], 16) | (bits[:, half:] & jnp.int32(-65536))


def _unpack_bf16_pairs(words):
    lo = lax.bitcast_convert_type(lax.shift_left(words, 16), F32)
    hi = lax.bitcast_convert_type(words & jnp.int32(-65536), F32)
    return lo, hi


def _layer_norm(x, g, b):
    mu = jnp.mean(x, axis=-1, keepdims=True)
    xc = x - mu
    var = jnp.mean(xc * xc, axis=-1, keepdims=True)
    return xc * lax.rsqrt(var + EPS) * g + b


C_Q = 0
C_KV = C_Q + Q_LORA
C_KI = C_KV + KV_LORA
C_WI = C_KI + IDX_HEADS * IDX_DIM
C_U = C_WI + LANES
C_V = C_U + SGU_WIDTH
D_IN_P_BASE = C_V + SGU_WIDTH


def _inproj_kernel(x_ref, gmix_ref, w_ref, gcq_ref, gckv_ref, gki_ref, bki_ref,
                   gsgu_ref, bsgu_ref, wsp_ref, bsp_ref, wbrb_ref,
                   cq_ref, ckv_ref, ki_ref, wi_ref, gas_ref, pb_ref):
    tm, d = x_ref.shape
    c_ga = D_IN_P_BASE
    c_gb = c_ga + d
    h = _rms(x_ref[...], gmix_ref[...]).astype(BF16)

    def proj(lo, width):
        return _dot(h, w_ref[:, lo:lo + width])

    cq_ref[...] = _rms(proj(C_Q, Q_LORA), gcq_ref[...]).astype(BF16)
    ckv_ref[...] = _rms(proj(C_KV, KV_LORA), gckv_ref[...]).astype(BF16)
    ki_ref[...] = _layer_norm(proj(C_KI, IDX_HEADS * IDX_DIM), gki_ref[...],
                              bki_ref[...]).astype(BF16)
    wi_ref[...] = proj(C_WI, LANES) * IDX_SCALE
    gas_ref[...] = jax.nn.sigmoid(proj(c_ga, d)).astype(BF16)

    u = jax.nn.gelu(proj(C_U, SGU_WIDTH))
    v = _layer_norm(jax.nn.gelu(proj(C_V, SGU_WIDTH)), gsgu_ref[...], bsgu_ref[...])

    row = lax.broadcasted_iota(jnp.int32, (SGU_CHUNK, 2 * SGU_CHUNK), 0)
    col = lax.broadcasted_iota(jnp.int32, (SGU_CHUNK, 2 * SGU_CHUNK), 1) % SGU_CHUNK
    causal = (row // CHUNK) >= (col // CHUNK)
    lane = lax.broadcasted_iota(jnp.int32, (SGU_CHUNK, LANES), 1)
    left = lane < SGU_GROUP_DIM
    n_tiles = SGU_WIDTH // LANES
    ws = [jnp.where(causal, wsp_ref[j], 0.0).astype(BF16) for j in range(n_tiles)]
    yb_chunks = []
    for c in range(tm // SGU_CHUNK):
        tiles = []
        for j in range(n_tiles):
            blk = v[c * SGU_CHUNK:(c + 1) * SGU_CHUNK, j * LANES:(j + 1) * LANES]
            stacked = jnp.concatenate(
                [jnp.where(left, blk, 0.0), jnp.where(left, 0.0, blk)], axis=0).astype(BF16)
            tiles.append(_dot(ws[j], stacked))
        s = jnp.concatenate(tiles, axis=1) + bsp_ref[...]
        yb_chunks.append(u[c * SGU_CHUNK:(c + 1) * SGU_CHUNK, :] * s)
    yb = jnp.concatenate(yb_chunks, axis=0).astype(BF16)
    pb_ref[...] = (jax.nn.sigmoid(proj(c_gb, d)) * _dot(yb, wbrb_ref[...])).astype(BF16)


def _inproj(x2, g_mix, w_in_p, g_cq, g_ckv, g_ki, b_ki, g_sgu, b_sgu, ws_pair, bsp, w_br_b,
            tm):
    n, d = x2.shape
    d_in_p = w_in_p.shape[1]

    def full(shape):
        return pl.BlockSpec(shape, lambda i: (0,) * len(shape))

    def rows(width):
        return pl.BlockSpec((tm, width), lambda i: (i, 0))

    return pl.pallas_call(
        _inproj_kernel,
        grid=(n // tm,),
        in_specs=[rows(d), full((1, d)), full((d, d_in_p)), full((1, Q_LORA)),
                  full((1, KV_LORA)), full((1, IDX_HEADS * IDX_DIM)),
                  full((1, IDX_HEADS * IDX_DIM)), full((1, SGU_WIDTH)), full((1, SGU_WIDTH)),
                  full(ws_pair.shape), full(bsp.shape), full(w_br_b.shape)],
        out_specs=[rows(Q_LORA), rows(KV_LORA), rows(IDX_HEADS * IDX_DIM), rows(LANES),
                   rows(d), rows(d)],
        out_shape=[jax.ShapeDtypeStruct((n, Q_LORA), BF16),
                   jax.ShapeDtypeStruct((n, KV_LORA), BF16),
                   jax.ShapeDtypeStruct((n, IDX_HEADS * IDX_DIM), BF16),
                   jax.ShapeDtypeStruct((n, LANES), F32),
                   jax.ShapeDtypeStruct((n, d), BF16),
                   jax.ShapeDtypeStruct((n, d), BF16)],
        compiler_params=pltpu.CompilerParams(dimension_semantics=("parallel",),
                                             vmem_limit_bytes=VMEM_LIMIT),
        name="inproj",
    )(x2, g_mix, w_in_p, g_cq, g_ckv, g_ki, b_ki, g_sgu, b_sgu, ws_pair, bsp, w_br_b)


def _attn_body(s_eff, topk, q_block0, cq_ref, ckv_ref, ki_ref, wi_ref, wuq_ref, wqi_ref,
               wuv_ref, ya_ref, q_scr, qi_scr, wcol_scr, sc_scr, bias_scr, acc_scr, s_buf,
               p_buf, kva_scr, sc16_scr):
    i = pl.program_id(1) + q_block0
    cq = cq_ref[0]
    q = _dot(cq, wuq_ref[...])
    for h in range(N_HEADS):
        q_scr[h] = (q[:, h * KV_LORA:(h + 1) * KV_LORA] * ATTN_SCALE).astype(BF16)

    rowq = lax.broadcasted_iota(jnp.int32, (Q_BLOCK, s_eff), 0)
    colk = lax.broadcasted_iota(jnp.int32, (Q_BLOCK, s_eff), 1)
    allowed = (colk // CHUNK) <= ((i * Q_BLOCK + rowq) // CHUNK)

    if s_eff <= topk:
        bias_scr[...] = jnp.where(allowed, 0.0, -jnp.inf)
    else:
        _select_topk(s_eff, topk, cq, allowed, colk, ki_ref, wi_ref, wqi_ref, qi_scr, wcol_scr,
                     sc_scr, bias_scr, s_buf, sc16_scr)

    acc_scr[...] = jnp.zeros_like(acc_scr)

    def logits(h):
        s_buf[h % 2] = _dot_nt(q_scr[h], ckv_ref[0, 0:s_eff, :]) + bias_scr[...]

    def softmax(h):
        s = s_buf[h % 2]
        p_buf[h % 2] = jnp.exp(s - jnp.max(s, axis=1, keepdims=True)).astype(BF16)

    def values(h):
        ol = _dot(p_buf[h % 2], kva_scr[0:s_eff, :])
        o = ol[:, :KV_LORA] / ol[:, KV_LORA:]
        acc_scr[...] += _dot(o.astype(BF16), wuv_ref[h])

    logits(0)
    for h in range(N_HEADS):
        if h + 1 < N_HEADS:
            logits(h + 1)
        softmax(h)
        values(h)
    ya_ref[0] = acc_scr[...].astype(BF16)


def _select_topk(s_eff, topk, cq, allowed, colk, ki_ref, wi_ref, wqi_ref, qi_scr, wcol_scr,
                 sc_scr, bias_scr, s_buf, sc16_scr):
    qi = _dot(cq, wqi_ref[...]).astype(BF16)
    wi = wi_ref[0]
    head_of_lane = lax.broadcasted_iota(jnp.int32, qi.shape, 1) // IDX_DIM
    for h in range(IDX_HEADS):
        qi_scr[h] = jnp.where(head_of_lane == h, qi, jnp.zeros_like(qi))
        wcol_scr[h] = wi[:, h:h + 1]

    def index_dots(h):
        s_buf[h % 2] = _dot_nt(qi_scr[h], ki_ref[0, 0:s_eff, :])

    index_dots(0)
    for h in range(IDX_HEADS):
        if h + 1 < IDX_HEADS:
            index_dots(h + 1)
        term = wcol_scr[h] * jnp.maximum(s_buf[h % 2], 0.0)
        bias_scr[...] = term if h == 0 else bias_scr[...] + term

    sc_scr[...] = jnp.where(allowed, bias_scr[...], -jnp.inf)

    def code_to_bits(code):
        return jnp.where(code < 0, code ^ jnp.int32(0x7FFFFFFF), code)

    def code_to_float(code):
        return lax.bitcast_convert_type(code_to_bits(code), F32)

    upper_half = jnp.int32(-65536)
    sc_bits = lax.bitcast_convert_type(sc_scr[...], jnp.int32)
    sc16_scr[...] = lax.bitcast_convert_type(sc_bits & upper_half, F32).astype(BF16)

    def count_ge_upper(cand):
        thr16 = lax.bitcast_convert_type(code_to_bits(cand) & upper_half, F32).astype(BF16)
        part = None
        for t in range(s_eff // LANES):
            tile = sc16_scr[:, t * LANES:(t + 1) * LANES]
            ind = jnp.where(tile >= thr16, jnp.ones((), BF16), jnp.zeros((), BF16))
            part = ind if part is None else part + ind
        return jnp.sum(part.astype(F32), axis=1, keepdims=True)

    def count_ge(cand):
        return jnp.sum(jnp.where(sc_scr[...] >= code_to_float(cand), 1.0, 0.0), axis=1,
                       keepdims=True)

    kf = float(topk)
    code = jnp.where(count_ge_upper(jnp.zeros((Q_BLOCK, 1), jnp.int32)) >= kf,
                     jnp.int32(0), jnp.int32(INT_MIN))
    for bit in range(30, -1, -1):
        cand = code + jnp.int32(1 << bit)
        cnt = count_ge_upper(cand) if bit >= 16 else count_ge(cand)
        feasible = (cnt >= kf) | (cand <= jnp.int32(CODE_NEG_INF))
        code = jnp.where(feasible, cand, code)

    thr = code_to_float(code)
    thr_up = code_to_float(code + 1)
    sel = sc_scr[...] >= thr
    n_ge = jnp.sum(jnp.where(sel, 1.0, 0.0), axis=1, keepdims=True)
    bias_scr[...] = jnp.where(sel & allowed, 0.0, -jnp.inf)
    tie_rows = (n_ge > kf) & (code > jnp.int32(CODE_NEG_INF))
    any_tie = jnp.max(jnp.where(tie_rows, 1.0, 0.0))

    @pl.when(any_tie > 0.0)
    def _():
        sc = sc_scr[...]
        gt = sc >= thr_up
        eq = (sc >= thr) & jnp.logical_not(gt)
        need = kf - jnp.sum(jnp.where(gt, 1.0, 0.0), axis=1, keepdims=True)
        tie = jnp.where(eq, 1.0, 0.0).astype(BF16)
        r = lax.broadcasted_iota(jnp.int32, (LANES, LANES), 0)
        c = lax.broadcasted_iota(jnp.int32, (LANES, LANES), 1)
        tri = jnp.where(r <= c, 1.0, 0.0).astype(BF16)
        tiles = [slice(t * LANES, (t + 1) * LANES) for t in range(s_eff // LANES)]
        within = [_dot(tie[:, cols], tri) for cols in tiles]
        before = jnp.zeros((Q_BLOCK, 1), F32)
        for cols, pre in zip(tiles, within):
            keep = gt[:, cols] | (eq[:, cols] & (pre + before <= need))
            bias_scr[:, cols] = jnp.where(keep & allowed[:, cols], 0.0, -jnp.inf)
            before = before + pre[:, LANES - 1:LANES]


def _attn_kernel(s_eff, topk, q_block0, cq_ref, ckv_ref, ki_ref, wi_ref, wuq_ref, wqi_ref,
                 wuv_ref, ya_ref, q_scr, qi_scr, wcol_scr, sc_scr, bias_scr, acc_scr, s_buf,
                 p_buf, kva_scr, sc16_scr):
    @pl.when(pl.program_id(1) == 0)
    def _():
        kva_scr[:, 0:KV_LORA] = ckv_ref[0]
        kva_scr[:, KV_LORA:] = jnp.ones((s_eff, KV_LORA), BF16)

    _attn_body(s_eff, topk, q_block0, cq_ref, ckv_ref, ki_ref, wi_ref, wuq_ref, wqi_ref,
               wuv_ref, ya_ref, q_scr, qi_scr, wcol_scr, sc_scr, bias_scr, acc_scr, s_buf,
               p_buf, kva_scr, sc16_scr)


def _attn(cq, ckv, ki, wi, w_uq, w_qi, w_uv_pad, n_buckets):
    b, s, _ = cq.shape
    nq = s // Q_BLOCK
    per = nq // n_buckets
    topk = min(TOPK_MAX, s // 4)
    aw = N_HEADS * HEAD_DIM

    def full(shape):
        return pl.BlockSpec(shape, lambda bi, i: (0,) * len(shape))

    parts = []
    for k in range(n_buckets):
        s_eff = (k + 1) * per * Q_BLOCK
        q0 = k * per

        def q_rows(width, q0=q0):
            return pl.BlockSpec((1, Q_BLOCK, width), lambda bi, i: (bi, i + q0, 0))

        def keys(width, s_eff=s_eff):
            return pl.BlockSpec((1, s_eff, width), lambda bi, i: (bi, 0, 0))

        parts.append(pl.pallas_call(
            functools.partial(_attn_kernel, s_eff, topk, q0),
            grid=(b, per),
            in_specs=[q_rows(Q_LORA), keys(KV_LORA), keys(IDX_HEADS * IDX_DIM), q_rows(LANES),
                      full(w_uq.shape), full(w_qi.shape), full(w_uv_pad.shape)],
            out_specs=pl.BlockSpec((1, Q_BLOCK, aw), lambda bi, i: (bi, i, 0)),
            out_shape=jax.ShapeDtypeStruct((b, per * Q_BLOCK, aw), BF16),
            scratch_shapes=[pltpu.VMEM((N_HEADS, Q_BLOCK, KV_LORA), BF16),
                            pltpu.VMEM((IDX_HEADS, Q_BLOCK, IDX_HEADS * IDX_DIM), BF16),
                            pltpu.VMEM((IDX_HEADS, Q_BLOCK, 1), F32),
                            pltpu.VMEM((Q_BLOCK, s_eff), F32),
                            pltpu.VMEM((Q_BLOCK, s_eff), F32),
                            pltpu.VMEM((Q_BLOCK, aw), F32),
                            pltpu.VMEM((2, Q_BLOCK, s_eff), F32),
                            pltpu.VMEM((2, Q_BLOCK, s_eff), BF16),
                            pltpu.VMEM((s_eff, 2 * KV_LORA), BF16),
                            pltpu.VMEM((Q_BLOCK, s_eff), BF16)],
            compiler_params=pltpu.CompilerParams(
                dimension_semantics=("parallel", "arbitrary"), vmem_limit_bytes=VMEM_LIMIT),
            name=f"attn_keys{s_eff}",
        )(cq, ckv, ki, wi, w_uq, w_qi, w_uv_pad))
    return jnp.concatenate(parts, axis=1)


def _merge_kernel(x_ref, ya_ref, gas_ref, pb_ref, wbra_ref, wo_ref, gffn_ref,
                  wr_hi_ref, wr_lo_ref, br_ref,
                  x1_ref, h2p_ref, idx_ref, gate_ref, rank_ref, cnt_ref, base_scr):
    step = pl.program_id(0)
    tm, d = x_ref.shape

    @pl.when(step == 0)
    def _():
        base_scr[...] = jnp.zeros_like(base_scr)

    a = _dot(ya_ref[...], wbra_ref[...])
    merged = gas_ref[...].astype(F32) * a + pb_ref[...].astype(F32)
    x1 = x_ref[...] + _dot(merged.astype(BF16), wo_ref[...])
    x1_ref[...] = x1
    h2 = _rms(x1, gffn_ref[...])

    h_hi = h2.astype(BF16)
    h2p_ref[...] = _pack_bf16_pairs(h2)
    h_lo = (h2 - h_hi.astype(F32)).astype(BF16)
    logits = (_dot(h_hi, wr_hi_ref[...]) + _dot(h_hi, wr_lo_ref[...])
              + _dot(h_lo, wr_hi_ref[...]) + br_ref[...])

    lane_e = lax.broadcasted_iota(jnp.int32, (tm, N_EXPERTS), 1).astype(F32)
    lane_o = lax.broadcasted_iota(jnp.int32, (tm, LANES), 1)
    work = logits
    vals, idxs = [], []
    onehot = jnp.zeros((tm, N_EXPERTS), F32)
    for _ in range(TOP_K):
        m = jnp.max(work, axis=1, keepdims=True)
        idx = jnp.min(jnp.where(work == m, lane_e, float(N_EXPERTS)), axis=1, keepdims=True)
        hit = lane_e == idx
        onehot = onehot + jnp.where(hit, 1.0, 0.0)
        work = jnp.where(hit, -jnp.inf, work)
        vals.append(m)
        idxs.append(idx)
    exps = [jnp.exp(v - vals[0]) for v in vals]
    denom = exps[0] + exps[1] + exps[2] + exps[3]

    r = lax.broadcasted_iota(jnp.int32, (tm, tm), 0)
    c = lax.broadcasted_iota(jnp.int32, (tm, tm), 1)
    tri = jnp.where(c < r, 1.0, 0.0).astype(BF16)
    rank_full = _dot(tri, onehot.astype(BF16)) + base_scr[...]

    idx_out = jnp.zeros((tm, LANES), F32)
    gate_out = jnp.zeros((tm, LANES), F32)
    rank_out = jnp.zeros((tm, LANES), F32)
    for k in range(TOP_K):
        rk = jnp.sum(jnp.where(lane_e == idxs[k], rank_full, 0.0), axis=1, keepdims=True)
        idx_out = jnp.where(lane_o == k, idxs[k], idx_out)
        gate_out = jnp.where(lane_o == k, exps[k] / denom, gate_out)
        rank_out = jnp.where(lane_o == k, rk, rank_out)
    gate_ref[...] = gate_out
    idx_ref[...] = idx_out.T[0:SUBLANES, :].astype(jnp.int32)
    rank_ref[...] = rank_out.T[0:SUBLANES, :].astype(jnp.int32)

    base_scr[...] = base_scr[...] + jnp.sum(onehot, axis=0, keepdims=True)
    cnt_ref[...] = base_scr[...].astype(jnp.int32)


def _merge(x2, ya, gas, pb, w_br_a, w_o, g_ffn, wr_hi, wr_lo, b_router, tm):
    n, d = x2.shape
    aw = ya.shape[1]

    def full(shape):
        return pl.BlockSpec(shape, lambda i: (0,) * len(shape))

    def rows(width):
        return pl.BlockSpec((tm, width), lambda i: (i, 0))

    slots = pl.BlockSpec((SUBLANES, tm), lambda i: (0, i))
    return pl.pallas_call(
        _merge_kernel,
        grid=(n // tm,),
        in_specs=[rows(d), rows(aw), rows(d), rows(d), full(w_br_a.shape), full(w_o.shape),
                  full((1, d)), full(wr_hi.shape), full(wr_lo.shape), full((1, N_EXPERTS))],
        out_specs=[rows(d), rows(d // 2), slots, rows(LANES), slots, full((1, N_EXPERTS))],
        out_shape=[jax.ShapeDtypeStruct((n, d), F32),
                   jax.ShapeDtypeStruct((n, d // 2), jnp.int32),
                   jax.ShapeDtypeStruct((SUBLANES, n), jnp.int32),
                   jax.ShapeDtypeStruct((n, LANES), F32),
                   jax.ShapeDtypeStruct((SUBLANES, n), jnp.int32),
                   jax.ShapeDtypeStruct((1, N_EXPERTS), jnp.int32)],
        scratch_shapes=[pltpu.VMEM((1, N_EXPERTS), F32)],
        compiler_params=pltpu.CompilerParams(dimension_semantics=("arbitrary",),
                                             vmem_limit_bytes=VMEM_LIMIT),
        name="merge_route",
    )(x2, ya, gas, pb, w_br_a, w_o, g_ffn, wr_hi, wr_lo, b_router)


def _sc_mesh():
    info = plsc.get_sparse_core_info()
    mesh = plsc.VectorSubcoreMesh(core_axis_name="c", subcore_axis_name="s")
    return mesh, info.num_cores, info.num_cores * info.num_subcores


def _sc_dispatch(src, dest_t, pad_idx, n_out_rows):
    n, w = src.shape
    mesh, n_cores, n_workers = _sc_mesh()
    per_w = n // n_workers
    n_chunks = per_w // SC_ROWS
    n_pad_chunks = pad_idx.size // (n_workers * SC_ROWS)

    @functools.partial(
        pl.kernel, mesh=mesh,
        out_type=jax.ShapeDtypeStruct((n_out_rows, w), jnp.int32),
        scratch_types=[pltpu.VMEM((n_chunks, TOP_K, SC_ROWS), jnp.int32),
                       pltpu.VMEM((n_pad_chunks, SC_ROWS), jnp.int32),
                       pltpu.VMEM((SC_ROWS, w), jnp.int32),
                       pltpu.VMEM((SC_ROWS, w), jnp.int32)],
        name="sc_dispatch",
    )
    def run(src_hbm, dest_hbm, pad_hbm, zeros_hbm, out_hbm, idx_v, pad_v, rows_v, zero_v):
        wid = lax.axis_index("s") * n_cores + lax.axis_index("c")
        pltpu.sync_copy(dest_hbm.at[wid], idx_v)
        pltpu.sync_copy(pad_hbm.at[wid], pad_v)
        pltpu.sync_copy(zeros_hbm, zero_v)

        @pl.loop(0, n_pad_chunks)
        def _(c):
            pltpu.sync_copy(zero_v, out_hbm.at[pad_v.at[c]])

        @pl.loop(0, n_chunks)
        def _(c):
            pltpu.sync_copy(src_hbm.at[pl.ds(wid * per_w + c * SC_ROWS, SC_ROWS)], rows_v)
            for k in range(TOP_K):
                pltpu.sync_copy(rows_v, out_hbm.at[idx_v.at[c, k]])

    dest_w = dest_t.reshape(TOP_K, n_workers, n_chunks, SC_ROWS).transpose(1, 2, 0, 3)
    return run(src, dest_w, pad_idx.reshape(n_workers, n_pad_chunks, SC_ROWS),
               jnp.zeros((SC_ROWS, w), jnp.int32))


def _sc_gather(table, idx):
    n_rows = idx.shape[0]
    w = table.shape[1]
    mesh, n_cores, n_workers = _sc_mesh()
    per_w = n_rows // n_workers
    n_chunks = per_w // SC_ROWS
    assert n_chunks % 2 == 0 and n_chunks * SC_ROWS * n_workers == n_rows

    @functools.partial(
        pl.kernel, mesh=mesh,
        out_type=jax.ShapeDtypeStruct((n_rows, w), table.dtype),
        scratch_types=[pltpu.VMEM((n_chunks, SC_ROWS), jnp.int32),
                       pltpu.VMEM((SC_ROWS, w), table.dtype),
                       pltpu.VMEM((SC_ROWS, w), table.dtype),
                       pltpu.SemaphoreType.DMA, pltpu.SemaphoreType.DMA],
        name="sc_gather",
    )
    def run(table_hbm, idx_hbm, out_hbm, idx_v, buf0, buf1, sem0, sem1):
        wid = lax.axis_index("s") * n_cores + lax.axis_index("c")
        base = wid * per_w
        pltpu.sync_copy(idx_hbm.at[wid], idx_v)

        def gather(c, buf, sem):
            return pltpu.make_async_copy(table_hbm.at[idx_v.at[c]], buf, sem)

        gather(0, buf0, sem0).start()

        @pl.loop(0, n_chunks, step=2)
        def _(c):
            gather(c + 1, buf1, sem1).start()
            gather(c, buf0, sem0).wait()
            pltpu.sync_copy(buf0, out_hbm.at[pl.ds(base + c * SC_ROWS, SC_ROWS)])

            @pl.when(c + 2 < n_chunks)
            def _():
                gather(c + 2, buf0, sem0).start()

            gather(c + 1, buf1, sem1).wait()
            pltpu.sync_copy(buf1, out_hbm.at[pl.ds(base + (c + 1) * SC_ROWS, SC_ROWS)])

    return run(table, idx.reshape(n_workers, n_chunks, SC_ROWS))


def _expert_kernel(be_ref, nu_ref, xs_ref, wgu_ref, bgu_ref, wd_ref, bd_ref, y_ref,
                   wgu_bf, wd_bf):
    i = pl.program_id(0)
    f = wd_ref.shape[1]
    half = xs_ref.shape[1]
    live = i < nu_ref[0]
    first = (i == 0) | (be_ref[i] != be_ref[jnp.maximum(i - 1, 0)])

    def mlp(w_gu_lo, w_gu_hi, w_d):
        x_lo, x_hi = _unpack_bf16_pairs(xs_ref[...])
        gu = _dot(x_lo.astype(BF16), w_gu_lo) + _dot(x_hi.astype(BF16), w_gu_hi) + bgu_ref[0]
        gate = jnp.minimum(gu[:, :f], SWIGLU_LIMIT)
        up = jnp.clip(gu[:, f:], -SWIGLU_LIMIT, SWIGLU_LIMIT)
        act = (up + 1.0) * (gate * jax.nn.sigmoid(SWIGLU_ALPHA * gate))
        y = _dot(act.astype(BF16), w_d) + bd_ref[0]
        y_ref[...] = _pack_bf16_pairs(y)

    @pl.when(live & first)
    def _():
        w_gu_lo = wgu_ref[0, 0:half, :].astype(BF16)
        w_gu_hi = wgu_ref[0, half:2 * half, :].astype(BF16)
        w_d = wd_ref[0].astype(BF16)
        wgu_bf[0:half, :] = w_gu_lo
        wgu_bf[half:2 * half, :] = w_gu_hi
        wd_bf[...] = w_d
        mlp(w_gu_lo, w_gu_hi, w_d)

    @pl.when(live & jnp.logical_not(first))
    def _():
        mlp(wgu_bf[0:half, :], wgu_bf[half:2 * half, :], wd_bf[...])


def _experts(block_e, n_used, xs, n_blocks, w_gu, b_gu, w_down, b_down):
    half = xs.shape[1]
    d = 2 * half
    f2 = w_gu.shape[2]
    f = w_down.shape[1]

    def blk(i, be, nu):
        return (jnp.minimum(i, nu[0] - 1), 0)

    def expert(i, be, nu):
        return (be[jnp.minimum(i, nu[0] - 1)], 0, 0)

    grid_spec = pltpu.PrefetchScalarGridSpec(
        num_scalar_prefetch=2,
        grid=(n_blocks,),
        in_specs=[pl.BlockSpec((EXPERT_BLOCK, half), blk),
                  pl.BlockSpec((1, d, f2), expert),
                  pl.BlockSpec((1, 1, f2), expert),
                  pl.BlockSpec((1, f, d), expert),
                  pl.BlockSpec((1, 1, d), expert)],
        out_specs=pl.BlockSpec((EXPERT_BLOCK, half), blk),
        scratch_shapes=[pltpu.VMEM((d, f2), BF16), pltpu.VMEM((f, d), BF16)],
    )
    return pl.pallas_call(
        _expert_kernel,
        grid_spec=grid_spec,
        out_shape=jax.ShapeDtypeStruct((n_blocks * EXPERT_BLOCK, half), jnp.int32),
        compiler_params=pltpu.CompilerParams(dimension_semantics=("arbitrary",),
                                             vmem_limit_bytes=VMEM_LIMIT),
        name="experts",
    )(block_e, n_used, xs, w_gu, b_gu, w_down, b_down)


def _combine_kernel(yg_ref, x1_ref, gate_ref, gfin_ref, out_ref):
    gates = gate_ref[...]
    half = yg_ref.shape[2]
    acc_lo = x1_ref[:, :half]
    acc_hi = x1_ref[:, half:]
    for k in range(TOP_K):
        y_lo, y_hi = _unpack_bf16_pairs(yg_ref[k])
        acc_lo = acc_lo + gates[:, k:k + 1] * y_lo
        acc_hi = acc_hi + gates[:, k:k + 1] * y_hi
    out_ref[...] = _rms(jnp.concatenate([acc_lo, acc_hi], axis=1), gfin_ref[...])


def _combine(yg, x1, gates, g_final, tm):
    n, d = x1.shape
    return pl.pallas_call(
        _combine_kernel,
        grid=(n // tm,),
        in_specs=[pl.BlockSpec((TOP_K, tm, d // 2), lambda i: (0, i, 0)),
                  pl.BlockSpec((tm, d), lambda i: (i, 0)),
                  pl.BlockSpec((tm, LANES), lambda i: (i, 0)),
                  pl.BlockSpec((1, d), lambda i: (0, 0))],
        out_specs=pl.BlockSpec((tm, d), lambda i: (i, 0)),
        out_shape=jax.ShapeDtypeStruct((n, d), F32),
        compiler_params=pltpu.CompilerParams(dimension_semantics=("parallel",),
                                             vmem_limit_bytes=VMEM_LIMIT),
        name="combine",
    )(yg, x1, gates, g_final)


def _pack_in_proj(w_in, d):
    offs = [0]
    for width in (Q_LORA, KV_LORA, IDX_DIM, IDX_HEADS, SGU_WIDTH, SGU_WIDTH, d, d):
        offs.append(offs[-1] + width)
    wq, wkv, wki, wwi, wu, wv, wga, wgb = [w_in[:, offs[j]:offs[j + 1]] for j in range(8)]
    wki_rep = jnp.tile(wki, (1, IDX_HEADS))
    wwi_pad = jnp.pad(wwi, ((0, 0), (0, LANES - IDX_HEADS)))
    return jnp.concatenate([wq, wkv, wki_rep, wwi_pad, wu, wv, wga, wgb], axis=1).astype(BF16)


def _layer(x, g_mix, w_in, g_cq, g_ckv, w_uq, w_uv, w_q_idx, g_kidx, b_kidx, g_sgu, b_sgu,
           w_spatial, b_spatial, w_br_a, w_br_b, w_o, g_ffn, w_router, b_router, w_gu, b_gu,
           w_down, b_down, g_final):
    b, s, d = x.shape
    n = b * s
    x2 = x.reshape(n, d)
    row = lambda v: v.reshape(1, -1).astype(F32)

    w_in_p = _pack_in_proj(w_in, d)
    ws_pair = w_spatial.reshape(SGU_GROUPS // 2, 2, SGU_CHUNK, SGU_CHUNK).transpose(
        0, 2, 1, 3).reshape(SGU_GROUPS // 2, SGU_CHUNK, 2 * SGU_CHUNK)
    bsp = jnp.repeat(b_spatial.T, SGU_GROUP_DIM, axis=1)
    cq, ckv, ki, wi, gas, pb = _inproj(
        x2, row(g_mix), w_in_p, row(g_cq), row(g_ckv), row(jnp.tile(g_kidx, IDX_HEADS)),
        row(jnp.tile(b_kidx, IDX_HEADS)), row(g_sgu), row(b_sgu), ws_pair, bsp,
        w_br_b.astype(BF16), tm=min(ROW_TILE, n))

    head_eye = jnp.eye(N_HEADS, dtype=F32)
    w_uv_pad = (w_uv[:, :, None, :] * head_eye[:, None, :, None]).reshape(
        N_HEADS, KV_LORA, N_HEADS * HEAD_DIM)
    n_buckets = next(nbk for nbk in (8, 4, 2, 1) if (s // Q_BLOCK) % nbk == 0)
    ya = _attn(cq.reshape(b, s, -1), ckv.reshape(b, s, -1), ki.reshape(b, s, -1),
               wi.reshape(b, s, -1), w_uq.astype(BF16), w_q_idx.astype(BF16),
               w_uv_pad.astype(BF16), n_buckets)

    wr_hi = w_router.astype(BF16)
    wr_lo = (w_router - wr_hi.astype(F32)).astype(BF16)
    x1, h2p, idx_t, gate_p, rank_t, counts = _merge(
        x2, ya.reshape(n, -1), gas, pb, w_br_a.astype(BF16), w_o.astype(BF16), row(g_ffn),
        wr_hi, wr_lo, row(b_router), tm=min(ROW_TILE, n))

    counts = counts[0]
    padded = (counts + EXPERT_BLOCK - 1) // EXPERT_BLOCK * EXPERT_BLOCK
    padded_end = jnp.cumsum(padded)
    padded_start = padded_end - padded
    expert_ids = jnp.arange(N_EXPERTS, dtype=jnp.int32)[:, None, None]
    group_start = jnp.sum(jnp.where(idx_t[None, :TOP_K] == expert_ids,
                                    padded_start[:, None, None], 0), axis=0)
    dest_t = group_start + rank_t[:TOP_K]
    nk = n * TOP_K
    nb = -(-(nk + N_EXPERTS * EXPERT_BLOCK) // EXPERT_BLOCK)
    block_row0 = jnp.arange(nb, dtype=jnp.int32) * EXPERT_BLOCK
    block_e = jnp.minimum(
        jnp.sum((padded_end[None, :] <= block_row0[:, None]).astype(jnp.int32), axis=1),
        N_EXPERTS - 1).astype(jnp.int32)
    n_used = (padded_end[-1:] // EXPERT_BLOCK).astype(jnp.int32)
    j = jnp.arange(EXPERT_BLOCK, dtype=jnp.int32)[None, :]
    pad_idx = jnp.where(j < (padded - counts)[:, None], (padded_start + counts)[:, None] + j,
                        nb * EXPERT_BLOCK + j).astype(jnp.int32)

    xs = _sc_dispatch(h2p, dest_t, pad_idx, (nb + 1) * EXPERT_BLOCK)
    ybuf = _experts(block_e, n_used, xs, nb, w_gu, b_gu.reshape(N_EXPERTS, 1, -1), w_down,
                    b_down.reshape(N_EXPERTS, 1, -1))
    yg = _sc_gather(ybuf, dest_t.reshape(-1))
    out = _combine(yg.reshape(TOP_K, n, d // 2), x1, gate_p, row(g_final), tm=512)
    return out.reshape(b, s, d)


def kernel(x, g_mix, w_in, g_cq, g_ckv, w_uq, w_uv, w_q_idx, g_kidx, b_kidx, g_sgu, b_sgu,
           w_spatial, b_spatial, w_br_a, w_br_b, w_o, g_ffn, w_router, b_router, w_gu, b_gu,
           w_down, b_down, g_final):
    assert g_mix.shape[0] == 1, "single-layer block"
    return _layer(x, g_mix[0], w_in[0], g_cq[0], g_ckv[0], w_uq[0], w_uv[0], w_q_idx[0],
                  g_kidx[0], b_kidx[0], g_sgu[0], b_sgu[0], w_spatial[0], b_spatial[0],
                  w_br_a[0], w_br_b[0], w_o[0], g_ffn[0], w_router[0], b_router[0], w_gu[0],
                  b_gu[0], w_down[0], b_down[0], g_final)
```

```python
import functools

import jax
import jax.numpy as jnp
from jax import lax
from jax.experimental import pallas as pl
from jax.experimental.pallas import tpu as pltpu
from jax.experimental.pallas import tpu_sc as plsc

EPS = 1e-6
CHUNK = 64
N_HEADS = 8
HEAD_DIM = 64
Q_LORA = 256
KV_LORA = 128
IDX_HEADS = 8
IDX_DIM = 32
TOPK_MAX = 256
Q_BLOCK = 256
ATTN_SCALE = KV_LORA ** -0.5
IDX_SCALE = (IDX_HEADS * IDX_DIM) ** -0.5
SGU_CHUNK = 128
SGU_GROUPS = 8
SGU_WIDTH = 512
SGU_GROUP_DIM = SGU_WIDTH // SGU_GROUPS
N_EXPERTS = 32
TOP_K = 4
SWIGLU_LIMIT = 7.0
SWIGLU_ALPHA = 1.702
EXPERT_BLOCK = 512

ROW_TILE = 1024
SC_ROWS = 32
LANES = 128
SUBLANES = 8
VMEM_LIMIT = 56 * 1024 * 1024
INT_MIN = -(2 ** 31)
CODE_NEG_INF = INT_MIN + 0x7FFFFF

BF16 = jnp.bfloat16
F32 = jnp.float32


def _dot(a, b):
    return jnp.dot(a, b, preferred_element_type=F32)


def _dot_nt(a, b):
    return lax.dot_general(a, b, (((1,), (1,)), ((), ())), preferred_element_type=F32)


def _rms(x, g):
    return x * lax.rsqrt(jnp.mean(x * x, axis=-1, keepdims=True) + EPS) * g


def _pack_bf16_pairs(x):
    half = x.shape[1] // 2
    bits = lax.bitcast_convert_type(x.astype(BF16).astype(F32), jnp.int32)
    return lax.shift_right_logical(bits[:, :half], 16) | (bits[:, half:] & jnp.int32(-65536))


def _unpack_bf16_pairs(words):
    lo = lax.bitcast_convert_type(lax.shift_left(words, 16), F32)
    hi = lax.bitcast_convert_type(words & jnp.int32(-65536), F32)
    return lo, hi


def _layer_norm(x, g, b):
    mu = jnp.mean(x, axis=-1, keepdims=True)
    xc = x - mu
    var = jnp.mean(xc * xc, axis=-1, keepdims=True)
    return xc * lax.rsqrt(var + EPS) * g + b


C_Q = 0
C_KV = C_Q + Q_LORA
C_KI = C_KV + KV_LORA
C_WI = C_KI + IDX_HEADS * IDX_DIM
C_U = C_WI + LANES
C_V = C_U + SGU_WIDTH
D_IN_P_BASE = C_V + SGU_WIDTH


def _inproj_kernel(x_ref, gmix_ref, w_ref, gcq_ref, gckv_ref, gki_ref, bki_ref,
                   gsgu_ref, bsgu_ref, wsp_ref, bsp_ref, wbrb_ref,
                   cq_ref, ckv_ref, ki_ref, wi_ref, gas_ref, pb_ref):
    tm, d = x_ref.shape
    c_ga = D_IN_P_BASE
    c_gb = c_ga + d
    h = _rms(x_ref[...], gmix_ref[...]).astype(BF16)

    def proj(lo, width):
        return _dot(h, w_ref[:, lo:lo + width])

    cq_ref[...] = _rms(proj(C_Q, Q_LORA), gcq_ref[...]).astype(BF16)
    ckv_ref[...] = _rms(proj(C_KV, KV_LORA), gckv_ref[...]).astype(BF16)
    ki_ref[...] = _layer_norm(proj(C_KI, IDX_HEADS * IDX_DIM), gki_ref[...],
                              bki_ref[...]).astype(BF16)
    wi_ref[...] = proj(C_WI, LANES) * IDX_SCALE
    gas_ref[...] = jax.nn.sigmoid(proj(c_ga, d)).astype(BF16)

    u = jax.nn.gelu(proj(C_U, SGU_WIDTH))
    v = _layer_norm(jax.nn.gelu(proj(C_V, SGU_WIDTH)), gsgu_ref[...], bsgu_ref[...])

    row = lax.broadcasted_iota(jnp.int32, (SGU_CHUNK, 2 * SGU_CHUNK), 0)
    col = lax.broadcasted_iota(jnp.int32, (SGU_CHUNK, 2 * SGU_CHUNK), 1) % SGU_CHUNK
    causal = (row // CHUNK) >= (col // CHUNK)
    lane = lax.broadcasted_iota(jnp.int32, (SGU_CHUNK, LANES), 1)
    left = lane < SGU_GROUP_DIM
    n_tiles = SGU_WIDTH // LANES
    ws = [jnp.where(causal, wsp_ref[j], 0.0).astype(BF16) for j in range(n_tiles)]
    yb_chunks = []
    for c in range(tm // SGU_CHUNK):
        tiles = []
        for j in range(n_tiles):
            blk = v[c * SGU_CHUNK:(c + 1) * SGU_CHUNK, j * LANES:(j + 1) * LANES]
            stacked = jnp.concatenate(
                [jnp.where(left, blk, 0.0), jnp.where(left, 0.0, blk)], axis=0).astype(BF16)
            tiles.append(_dot(ws[j], stacked))
        s = jnp.concatenate(tiles, axis=1) + bsp_ref[...]
        yb_chunks.append(u[c * SGU_CHUNK:(c + 1) * SGU_CHUNK, :] * s)
    yb = jnp.concatenate(yb_chunks, axis=0).astype(BF16)
    pb_ref[...] = (jax.nn.sigmoid(proj(c_gb, d)) * _dot(yb, wbrb_ref[...])).astype(BF16)


def _inproj(x2, g_mix, w_in_p, g_cq, g_ckv, g_ki, b_ki, g_sgu, b_sgu, ws_pair, bsp, w_br_b,
            tm):
    n, d = x2.shape
    d_in_p = w_in_p.shape[1]

    def full(shape):
        return pl.BlockSpec(shape, lambda i: (0,) * len(shape))

    def rows(width):
        return pl.BlockSpec((tm, width), lambda i: (i, 0))

    return pl.pallas_call(
        _inproj_kernel,
        grid=(n // tm,),
        in_specs=[rows(d), full((1, d)), full((d, d_in_p)), full((1, Q_LORA)),
                  full((1, KV_LORA)), full((1, IDX_HEADS * IDX_DIM)),
                  full((1, IDX_HEADS * IDX_DIM)), full((1, SGU_WIDTH)), full((1, SGU_WIDTH)),
                  full(ws_pair.shape), full(bsp.shape), full(w_br_b.shape)],
        out_specs=[rows(Q_LORA), rows(KV_LORA), rows(IDX_HEADS * IDX_DIM), rows(LANES),
                   rows(d), rows(d)],
        out_shape=[jax.ShapeDtypeStruct((n, Q_LORA), BF16),
                   jax.ShapeDtypeStruct((n, KV_LORA), BF16),
                   jax.ShapeDtypeStruct((n, IDX_HEADS * IDX_DIM), BF16),
                   jax.ShapeDtypeStruct((n, LANES), F32),
                   jax.ShapeDtypeStruct((n, d), BF16),
                   jax.ShapeDtypeStruct((n, d), BF16)],
        compiler_params=pltpu.CompilerParams(dimension_semantics=("parallel",),
                                             vmem_limit_bytes=VMEM_LIMIT),
        name="inproj",
    )(x2, g_mix, w_in_p, g_cq, g_ckv, g_ki, b_ki, g_sgu, b_sgu, ws_pair, bsp, w_br_b)


def _attn_body(s_eff, topk, q_block0, cq_ref, ckv_ref, ki_ref, wi_ref, wuq_ref, wqi_ref,
               wuv_ref, ya_ref, q_scr, qi_scr, wcol_scr, sc_scr, bias_scr, acc_scr, s_buf,
               p_buf, kva_scr, sc16_scr):
    i = pl.program_id(1) + q_block0
    cq = cq_ref[0]
    q = _dot(cq, wuq_ref[...])
    for h in range(N_HEADS):
        q_scr[h] = (q[:, h * KV_LORA:(h + 1) * KV_LORA] * ATTN_SCALE).astype(BF16)

    rowq = lax.broadcasted_iota(jnp.int32, (Q_BLOCK, s_eff), 0)
    colk = lax.broadcasted_iota(jnp.int32, (Q_BLOCK, s_eff), 1)
    allowed = (colk // CHUNK) <= ((i * Q_BLOCK + rowq) // CHUNK)

    if s_eff <= topk:
        bias_scr[...] = jnp.where(allowed, 0.0, -jnp.inf)
    else:
        _select_topk(s_eff, topk, cq, allowed, colk, ki_ref, wi_ref, wqi_ref, qi_scr, wcol_scr,
                     sc_scr, bias_scr, s_buf, sc16_scr)

    acc_scr[...] = jnp.zeros_like(acc_scr)

    def logits(h):
        s_buf[h % 2] = _dot_nt(q_scr[h], ckv_ref[0, 0:s_eff, :]) + bias_scr[...]

    def softmax(h):
        s = s_buf[h % 2]
        p_buf[h % 2] = jnp.exp(s - jnp.max(s, axis=1, keepdims=True)).astype(BF16)

    def values(h):
        ol = _dot(p_buf[h % 2], kva_scr[0:s_eff, :])
        o = ol[:, :KV_LORA] / ol[:, KV_LORA:]
        acc_scr[...] += _dot(o.astype(BF16), wuv_ref[h])

    logits(0)
    for h in range(N_HEADS):
        if h + 1 < N_HEADS:
            logits(h + 1)
        softmax(h)
        values(h)
    ya_ref[0] = acc_scr[...].astype(BF16)


def _select_topk(s_eff, topk, cq, allowed, colk, ki_ref, wi_ref, wqi_ref, qi_scr, wcol_scr,
                 sc_scr, bias_scr, s_buf, sc16_scr):
    qi = _dot(cq, wqi_ref[...]).astype(BF16)
    wi = wi_ref[0]
    head_of_lane = lax.broadcasted_iota(jnp.int32, qi.shape, 1) // IDX_DIM
    for h in range(IDX_HEADS):
        qi_scr[h] = jnp.where(head_of_lane == h, qi, jnp.zeros_like(qi))
        wcol_scr[h] = wi[:, h:h + 1]

    def index_dots(h):
        s_buf[h % 2] = _dot_nt(qi_scr[h], ki_ref[0, 0:s_eff, :])

    index_dots(0)
    for h in range(IDX_HEADS):
        if h + 1 < IDX_HEADS:
            index_dots(h + 1)
        term = wcol_scr[h] * jnp.maximum(s_buf[h % 2], 0.0)
        bias_scr[...] = term if h == 0 else bias_scr[...] + term

    sc_scr[...] = jnp.where(allowed, bias_scr[...], -jnp.inf)

    def code_to_bits(code):
        return jnp.where(code < 0, code ^ jnp.int32(0x7FFFFFFF), code)

    def code_to_float(code):
        return lax.bitcast_convert_type(code_to_bits(code), F32)

    upper_half = jnp.int32(-65536)
    sc_bits = lax.bitcast_convert_type(sc_scr[...], jnp.int32)
    sc16_scr[...] = lax.bitcast_convert_type(sc_bits & upper_half, F32).astype(BF16)

    def count_ge_upper(cand):
        thr16 = lax.bitcast_convert_type(code_to_bits(cand) & upper_half, F32).astype(BF16)
        part = None
        for t in range(s_eff // LANES):
            tile = sc16_scr[:, t * LANES:(t + 1) * LANES]
            ind = jnp.where(tile >= thr16, jnp.ones((), BF16), jnp.zeros((), BF16))
            part = ind if part is None else part + ind
        return jnp.sum(part.astype(F32), axis=1, keepdims=True)

    def count_ge(cand):
        return jnp.sum(jnp.where(sc_scr[...] >= code_to_float(cand), 1.0, 0.0), axis=1,
                       keepdims=True)

    kf = float(topk)
    code = jnp.where(count_ge_upper(jnp.zeros((Q_BLOCK, 1), jnp.int32)) >= kf,
                     jnp.int32(0), jnp.int32(INT_MIN))
    for bit in range(30, -1, -1):
        cand = code + jnp.int32(1 << bit)
        cnt = count_ge_upper(cand) if bit >= 16 else count_ge(cand)
        feasible = (cnt >= kf) | (cand <= jnp.int32(CODE_NEG_INF))
        code = jnp.where(feasible, cand, code)

    thr = code_to_float(code)
    thr_up = code_to_float(code + 1)
    sc = sc_scr[...]
    gt = sc >= thr_up
    eq = (sc >= thr) & jnp.logical_not(gt)
    need = kf - jnp.sum(jnp.where(gt, 1.0, 0.0), axis=1, keepdims=True)
    tie = jnp.where(eq, 1.0, 0.0).astype(BF16)
    r = lax.broadcasted_iota(jnp.int32, (LANES, LANES), 0)
    c = lax.broadcasted_iota(jnp.int32, (LANES, LANES), 1)
    tri = jnp.where(r <= c, 1.0, 0.0).astype(BF16)
    tiles = [slice(t * LANES, (t + 1) * LANES) for t in range(s_eff // LANES)]
    within = [_dot(tie[:, cols], tri) for cols in tiles]
    before = jnp.zeros((Q_BLOCK, 1), F32)
    for cols, pre in zip(tiles, within):
        keep = gt[:, cols] | (eq[:, cols] & (pre + before <= need))
        bias_scr[:, cols] = jnp.where(keep & allowed[:, cols], 0.0, -jnp.inf)
        before = before + pre[:, LANES - 1:LANES]


def _attn_kernel(s_eff, topk, q_block0, cq_ref, ckv_ref, ki_ref, wi_ref, wuq_ref, wqi_ref,
                 wuv_ref, ya_ref, q_scr, qi_scr, wcol_scr, sc_scr, bias_scr, acc_scr, s_buf,
                 p_buf, kva_scr, sc16_scr):
    @pl.when(pl.program_id(1) == 0)
    def _():
        kva_scr[:, 0:KV_LORA] = ckv_ref[0]
        kva_scr[:, KV_LORA:] = jnp.ones((s_eff, KV_LORA), BF16)

    _attn_body(s_eff, topk, q_block0, cq_ref, ckv_ref, ki_ref, wi_ref, wuq_ref, wqi_ref,
               wuv_ref, ya_ref, q_scr, qi_scr, wcol_scr, sc_scr, bias_scr, acc_scr, s_buf,
               p_buf, kva_scr, sc16_scr)


def _attn(cq, ckv, ki, wi, w_uq, w_qi, w_uv_pad, n_buckets):
    b, s, _ = cq.shape
    nq = s // Q_BLOCK
    per = nq // n_buckets
    topk = min(TOPK_MAX, s // 4)
    aw = N_HEADS * HEAD_DIM

    def full(shape):
        return pl.BlockSpec(shape, lambda bi, i: (0,) * len(shape))

    parts = []
    for k in range(n_buckets):
        s_eff = (k + 1) * per * Q_BLOCK
        q0 = k * per

        def q_rows(width, q0=q0):
            return pl.BlockSpec((1, Q_BLOCK, width), lambda bi, i: (bi, i + q0, 0))

        def keys(width, s_eff=s_eff):
            return pl.BlockSpec((1, s_eff, width), lambda bi, i: (bi, 0, 0))

        parts.append(pl.pallas_call(
            functools.partial(_attn_kernel, s_eff, topk, q0),
            grid=(b, per),
            in_specs=[q_rows(Q_LORA), keys(KV_LORA), keys(IDX_HEADS * IDX_DIM), q_rows(LANES),
                      full(w_uq.shape), full(w_qi.shape), full(w_uv_pad.shape)],
            out_specs=pl.BlockSpec((1, Q_BLOCK, aw), lambda bi, i: (bi, i, 0)),
            out_shape=jax.ShapeDtypeStruct((b, per * Q_BLOCK, aw), BF16),
            scratch_shapes=[pltpu.VMEM((N_HEADS, Q_BLOCK, KV_LORA), BF16),
                            pltpu.VMEM((IDX_HEADS, Q_BLOCK, IDX_HEADS * IDX_DIM), BF16),
                            pltpu.VMEM((IDX_HEADS, Q_BLOCK, 1), F32),
                            pltpu.VMEM((Q_BLOCK, s_eff), F32),
                            pltpu.VMEM((Q_BLOCK, s_eff), F32),
                            pltpu.VMEM((Q_BLOCK, aw), F32),
                            pltpu.VMEM((2, Q_BLOCK, s_eff), F32),
                            pltpu.VMEM((2, Q_BLOCK, s_eff), BF16),
                            pltpu.VMEM((s_eff, 2 * KV_LORA), BF16),
                            pltpu.VMEM((Q_BLOCK, s_eff), BF16)],
            compiler_params=pltpu.CompilerParams(
                dimension_semantics=("parallel", "arbitrary"), vmem_limit_bytes=VMEM_LIMIT),
            name=f"attn_keys{s_eff}",
        )(cq, ckv, ki, wi, w_uq, w_qi, w_uv_pad))
    return jnp.concatenate(parts, axis=1)


def _merge_kernel(x_ref, ya_ref, gas_ref, pb_ref, wbra_ref, wo_ref, gffn_ref,
                  wr_hi_ref, wr_lo_ref, br_ref,
                  x1_ref, h2p_ref, idx_ref, gate_ref, rank_ref, cnt_ref, base_scr):
    step = pl.program_id(0)
    tm, d = x_ref.shape

    @pl.when(step == 0)
    def _():
        base_scr[...] = jnp.zeros_like(base_scr)

    a = _dot(ya_ref[...], wbra_ref[...])
    merged = gas_ref[...].astype(F32) * a + pb_ref[...].astype(F32)
    x1 = x_ref[...] + _dot(merged.astype(BF16), wo_ref[...])
    x1_ref[...] = x1
    h2 = _rms(x1, gffn_ref[...])

    h_hi = h2.astype(BF16)
    h2p_ref[...] = _pack_bf16_pairs(h2)
    h_lo = (h2 - h_hi.astype(F32)).astype(BF16)
    logits = (_dot(h_hi, wr_hi_ref[...]) + _dot(h_hi, wr_lo_ref[...])
              + _dot(h_lo, wr_hi_ref[...]) + br_ref[...])

    lane_e = lax.broadcasted_iota(jnp.int32, (tm, N_EXPERTS), 1).astype(F32)
    lane_o = lax.broadcasted_iota(jnp.int32, (tm, LANES), 1)
    work = logits
    vals, idxs = [], []
    onehot = jnp.zeros((tm, N_EXPERTS), F32)
    for _ in range(TOP_K):
        m = jnp.max(work, axis=1, keepdims=True)
        idx = jnp.min(jnp.where(work == m, lane_e, float(N_EXPERTS)), axis=1, keepdims=True)
        hit = lane_e == idx
        onehot = onehot + jnp.where(hit, 1.0, 0.0)
        work = jnp.where(hit, -jnp.inf, work)
        vals.append(m)
        idxs.append(idx)
    exps = [jnp.exp(v - vals[0]) for v in vals]
    denom = exps[0] + exps[1] + exps[2] + exps[3]

    r = lax.broadcasted_iota(jnp.int32, (tm, tm), 0)
    c = lax.broadcasted_iota(jnp.int32, (tm, tm), 1)
    tri = jnp.where(c < r, 1.0, 0.0).astype(BF16)
    rank_full = _dot(tri, onehot.astype(BF16)) + base_scr[...]

    idx_out = jnp.zeros((tm, LANES), F32)
    gate_out = jnp.zeros((tm, LANES), F32)
    rank_out = jnp.zeros((tm, LANES), F32)
    for k in range(TOP_K):
        rk = jnp.sum(jnp.where(lane_e == idxs[k], rank_full, 0.0), axis=1, keepdims=True)
        idx_out = jnp.where(lane_o == k, idxs[k], idx_out)
        gate_out = jnp.where(lane_o == k, exps[k] / denom, gate_out)
        rank_out = jnp.where(lane_o == k, rk, rank_out)
    gate_ref[...] = gate_out
    idx_ref[...] = idx_out.T[0:SUBLANES, :].astype(jnp.int32)
    rank_ref[...] = rank_out.T[0:SUBLANES, :].astype(jnp.int32)

    base_scr[...] = base_scr[...] + jnp.sum(onehot, axis=0, keepdims=True)
    cnt_ref[...] = base_scr[...].astype(jnp.int32)


def _merge(x2, ya, gas, pb, w_br_a, w_o, g_ffn, wr_hi, wr_lo, b_router, tm):
    n, d = x2.shape
    aw = ya.shape[1]

    def full(shape):
        return pl.BlockSpec(shape, lambda i: (0,) * len(shape))

    def rows(width):
        return pl.BlockSpec((tm, width), lambda i: (i, 0))

    slots = pl.BlockSpec((SUBLANES, tm), lambda i: (0, i))
    return pl.pallas_call(
        _merge_kernel,
        grid=(n // tm,),
        in_specs=[rows(d), rows(aw), rows(d), rows(d), full(w_br_a.shape), full(w_o.shape),
                  full((1, d)), full(wr_hi.shape), full(wr_lo.shape), full((1, N_EXPERTS))],
        out_specs=[rows(d), rows(d // 2), slots, rows(LANES), slots, full((1, N_EXPERTS))],
        out_shape=[jax.ShapeDtypeStruct((n, d), F32),
                   jax.ShapeDtypeStruct((n, d // 2), jnp.int32),
                   jax.ShapeDtypeStruct((SUBLANES, n), jnp.int32),
                   jax.ShapeDtypeStruct((n, LANES), F32),
                   jax.ShapeDtypeStruct((SUBLANES, n), jnp.int32),
                   jax.ShapeDtypeStruct((1, N_EXPERTS), jnp.int32)],
        scratch_shapes=[pltpu.VMEM((1, N_EXPERTS), F32)],
        compiler_params=pltpu.CompilerParams(dimension_semantics=("arbitrary",),
                                             vmem_limit_bytes=VMEM_LIMIT),
        name="merge_route",
    )(x2, ya, gas, pb, w_br_a, w_o, g_ffn, wr_hi, wr_lo, b_router)


def _sc_mesh():
    info = plsc.get_sparse_core_info()
    mesh = plsc.VectorSubcoreMesh(core_axis_name="c", subcore_axis_name="s")
    return mesh, info.num_cores, info.num_cores * info.num_subcores


def _sc_dispatch(src, dest_t, pad_idx, n_out_rows):
    n, w = src.shape
    mesh, n_cores, n_workers = _sc_mesh()
    per_w = n // n_workers
    n_chunks = per_w // SC_ROWS
    n_pad_chunks = pad_idx.size // (n_workers * SC_ROWS)

    @functools.partial(
        pl.kernel, mesh=mesh,
        out_type=jax.ShapeDtypeStruct((n_out_rows, w), jnp.int32),
        scratch_types=[pltpu.VMEM((n_chunks, TOP_K, SC_ROWS), jnp.int32),
                       pltpu.VMEM((n_pad_chunks, SC_ROWS), jnp.int32),
                       pltpu.VMEM((SC_ROWS, w), jnp.int32),
                       pltpu.VMEM((SC_ROWS, w), jnp.int32)],
        name="sc_dispatch",
    )
    def run(src_hbm, dest_hbm, pad_hbm, zeros_hbm, out_hbm, idx_v, pad_v, rows_v, zero_v):
        wid = lax.axis_index("s") * n_cores + lax.axis_index("c")
        pltpu.sync_copy(dest_hbm.at[wid], idx_v)
        pltpu.sync_copy(pad_hbm.at[wid], pad_v)
        pltpu.sync_copy(zeros_hbm, zero_v)

        @pl.loop(0, n_pad_chunks)
        def _(c):
            pltpu.sync_copy(zero_v, out_hbm.at[pad_v.at[c]])

        @pl.loop(0, n_chunks)
        def _(c):
            pltpu.sync_copy(src_hbm.at[pl.ds(wid * per_w + c * SC_ROWS, SC_ROWS)], rows_v)
            for k in range(TOP_K):
                pltpu.sync_copy(rows_v, out_hbm.at[idx_v.at[c, k]])

    dest_w = dest_t.reshape(TOP_K, n_workers, n_chunks, SC_ROWS).transpose(1, 2, 0, 3)
    return run(src, dest_w, pad_idx.reshape(n_workers, n_pad_chunks, SC_ROWS),
               jnp.zeros((SC_ROWS, w), jnp.int32))


def _sc_gather(table, idx):
    n_rows = idx.shape[0]
    w = table.shape[1]
    mesh, n_cores, n_workers = _sc_mesh()
    per_w = n_rows // n_workers
    n_chunks = per_w // SC_ROWS
    assert n_chunks % 2 == 0 and n_chunks * SC_ROWS * n_workers == n_rows

    @functools.partial(
        pl.kernel, mesh=mesh,
        out_type=jax.ShapeDtypeStruct((n_rows, w), table.dtype),
        scratch_types=[pltpu.VMEM((n_chunks, SC_ROWS), jnp.int32),
                       pltpu.VMEM((SC_ROWS, w), table.dtype),
                       pltpu.VMEM((SC_ROWS, w), table.dtype),
                       pltpu.SemaphoreType.DMA, pltpu.SemaphoreType.DMA],
        name="sc_gather",
    )
    def run(table_hbm, idx_hbm, out_hbm, idx_v, buf0, buf1, sem0, sem1):
        wid = lax.axis_index("s") * n_cores + lax.axis_index("c")
        base = wid * per_w
        pltpu.sync_copy(idx_hbm.at[wid], idx_v)

        def gather(c, buf, sem):
            return pltpu.make_async_copy(table_hbm.at[idx_v.at[c]], buf, sem)

        gather(0, buf0, sem0).start()

        @pl.loop(0, n_chunks, step=2)
        def _(c):
            gather(c + 1, buf1, sem1).start()
            gather(c, buf0, sem0).wait()
            pltpu.sync_copy(buf0, out_hbm.at[pl.ds(base + c * SC_ROWS, SC_ROWS)])

            @pl.when(c + 2 < n_chunks)
            def _():
                gather(c + 2, buf0, sem0).start()

            gather(c + 1, buf1, sem1).wait()
            pltpu.sync_copy(buf1, out_hbm.at[pl.ds(base + (c + 1) * SC_ROWS, SC_ROWS)])

    return run(table, idx.reshape(n_workers, n_chunks, SC_ROWS))


def _expert_kernel(be_ref, nu_ref, xs_ref, wgu_ref, bgu_ref, wd_ref, bd_ref, y_ref,
                   wgu_bf, wd_bf):
    i = pl.program_id(0)
    f = wd_ref.shape[1]
    half = xs_ref.shape[1]
    live = i < nu_ref[0]
    first = (i == 0) | (be_ref[i] != be_ref[jnp.maximum(i - 1, 0)])

    def mlp(w_gu_lo, w_gu_hi, w_d):
        x_lo, x_hi = _unpack_bf16_pairs(xs_ref[...])
        gu = _dot(x_lo.astype(BF16), w_gu_lo) + _dot(x_hi.astype(BF16), w_gu_hi) + bgu_ref[0]
        gate = jnp.minimum(gu[:, :f], SWIGLU_LIMIT)
        up = jnp.clip(gu[:, f:], -SWIGLU_LIMIT, SWIGLU_LIMIT)
        act = (up + 1.0) * (gate * jax.nn.sigmoid(SWIGLU_ALPHA * gate))
        y = _dot(act.astype(BF16), w_d) + bd_ref[0]
        y_ref[...] = _pack_bf16_pairs(y)

    @pl.when(live & first)
    def _():
        w_gu_lo = wgu_ref[0, 0:half, :].astype(BF16)
        w_gu_hi = wgu_ref[0, half:2 * half, :].astype(BF16)
        w_d = wd_ref[0].astype(BF16)
        wgu_bf[0:half, :] = w_gu_lo
        wgu_bf[half:2 * half, :] = w_gu_hi
        wd_bf[...] = w_d
        mlp(w_gu_lo, w_gu_hi, w_d)

    @pl.when(live & jnp.logical_not(first))
    def _():
        mlp(wgu_bf[0:half, :], wgu_bf[half:2 * half, :], wd_bf[...])


def _experts(block_e, n_used, xs, n_blocks, w_gu, b_gu, w_down, b_down):
    half = xs.shape[1]
    d = 2 * half
    f2 = w_gu.shape[2]
    f = w_down.shape[1]

    def blk(i, be, nu):
        return (jnp.minimum(i, nu[0] - 1), 0)

    def expert(i, be, nu):
        return (be[jnp.minimum(i, nu[0] - 1)], 0, 0)

    grid_spec = pltpu.PrefetchScalarGridSpec(
        num_scalar_prefetch=2,
        grid=(n_blocks,),
        in_specs=[pl.BlockSpec((EXPERT_BLOCK, half), blk),
                  pl.BlockSpec((1, d, f2), expert),
                  pl.BlockSpec((1, 1, f2), expert),
                  pl.BlockSpec((1, f, d), expert),
                  pl.BlockSpec((1, 1, d), expert)],
        out_specs=pl.BlockSpec((EXPERT_BLOCK, half), blk),
        scratch_shapes=[pltpu.VMEM((d, f2), BF16), pltpu.VMEM((f, d), BF16)],
    )
    return pl.pallas_call(
        _expert_kernel,
        grid_spec=grid_spec,
        out_shape=jax.ShapeDtypeStruct((n_blocks * EXPERT_BLOCK, half), jnp.int32),
        compiler_params=pltpu.CompilerParams(dimension_semantics=("arbitrary",),
                                             vmem_limit_bytes=VMEM_LIMIT),
        name="experts",
    )(block_e, n_used, xs, w_gu, b_gu, w_down, b_down)


def _combine_kernel(yg_ref, x1_ref, gate_ref, gfin_ref, out_ref):
    gates = gate_ref[...]
    half = yg_ref.shape[2]
    acc_lo = x1_ref[:, :half]
    acc_hi = x1_ref[:, half:]
    for k in range(TOP_K):
        y_lo, y_hi = _unpack_bf16_pairs(yg_ref[k])
        acc_lo = acc_lo + gates[:, k:k + 1] * y_lo
        acc_hi = acc_hi + gates[:, k:k + 1] * y_hi
    out_ref[...] = _rms(jnp.concatenate([acc_lo, acc_hi], axis=1), gfin_ref[...])


def _combine(yg, x1, gates, g_final, tm):
    n, d = x1.shape
    return pl.pallas_call(
        _combine_kernel,
        grid=(n // tm,),
        in_specs=[pl.BlockSpec((TOP_K, tm, d // 2), lambda i: (0, i, 0)),
                  pl.BlockSpec((tm, d), lambda i: (i, 0)),
                  pl.BlockSpec((tm, LANES), lambda i: (i, 0)),
                  pl.BlockSpec((1, d), lambda i: (0, 0))],
        out_specs=pl.BlockSpec((tm, d), lambda i: (i, 0)),
        out_shape=jax.ShapeDtypeStruct((n, d), F32),
        compiler_params=pltpu.CompilerParams(dimension_semantics=("parallel",),
                                             vmem_limit_bytes=VMEM_LIMIT),
        name="combine",
    )(yg, x1, gates, g_final)


def _pack_in_proj(w_in, d):
    offs = [0]
    for width in (Q_LORA, KV_LORA, IDX_DIM, IDX_HEADS, SGU_WIDTH, SGU_WIDTH, d, d):
        offs.append(offs[-1] + width)
    wq, wkv, wki, wwi, wu, wv, wga, wgb = [w_in[:, offs[j]:offs[j + 1]] for j in range(8)]
    wki_rep = jnp.tile(wki, (1, IDX_HEADS))
    wwi_pad = jnp.pad(wwi, ((0, 0), (0, LANES - IDX_HEADS)))
    return jnp.concatenate([wq, wkv, wki_rep, wwi_pad, wu, wv, wga, wgb], axis=1).astype(BF16)


def _layer(x, g_mix, w_in, g_cq, g_ckv, w_uq, w_uv, w_q_idx, g_kidx, b_kidx, g_sgu, b_sgu,
           w_spatial, b_spatial, w_br_a, w_br_b, w_o, g_ffn, w_router, b_router, w_gu, b_gu,
           w_down, b_down, g_final):
    b, s, d = x.shape
    n = b * s
    x2 = x.reshape(n, d)
    row = lambda v: v.reshape(1, -1).astype(F32)

    w_in_p = _pack_in_proj(w_in, d)
    ws_pair = w_spatial.reshape(SGU_GROUPS // 2, 2, SGU_CHUNK, SGU_CHUNK).transpose(
        0, 2, 1, 3).reshape(SGU_GROUPS // 2, SGU_CHUNK, 2 * SGU_CHUNK)
    bsp = jnp.repeat(b_spatial.T, SGU_GROUP_DIM, axis=1)
    cq, ckv, ki, wi, gas, pb = _inproj(
        x2, row(g_mix), w_in_p, row(g_cq), row(g_ckv), row(jnp.tile(g_kidx, IDX_HEADS)),
        row(jnp.tile(b_kidx, IDX_HEADS)), row(g_sgu), row(b_sgu), ws_pair, bsp,
        w_br_b.astype(BF16), tm=min(ROW_TILE, n))

    head_eye = jnp.eye(N_HEADS, dtype=F32)
    w_uv_pad = (w_uv[:, :, None, :] * head_eye[:, None, :, None]).reshape(
        N_HEADS, KV_LORA, N_HEADS * HEAD_DIM)
    n_buckets = next(nbk for nbk in (8, 4, 2, 1) if (s // Q_BLOCK) % nbk == 0)
    ya = _attn(cq.reshape(b, s, -1), ckv.reshape(b, s, -1), ki.reshape(b, s, -1),
               wi.reshape(b, s, -1), w_uq.astype(BF16), w_q_idx.astype(BF16),
               w_uv_pad.astype(BF16), n_buckets)

    wr_hi = w_router.astype(BF16)
    wr_lo = (w_router - wr_hi.astype(F32)).astype(BF16)
    x1, h2p, idx_t, gate_p, rank_t, counts = _merge(
        x2, ya.reshape(n, -1), gas, pb, w_br_a.astype(BF16), w_o.astype(BF16), row(g_ffn),
        wr_hi, wr_lo, row(b_router), tm=min(ROW_TILE, n))

    counts = counts[0]
    padded = (counts + EXPERT_BLOCK - 1) // EXPERT_BLOCK * EXPERT_BLOCK
    padded_end = jnp.cumsum(padded)
    padded_start = padded_end - padded
    expert_ids = jnp.arange(N_EXPERTS, dtype=jnp.int32)[:, None, None]
    group_start = jnp.sum(jnp.where(idx_t[None, :TOP_K] == expert_ids,
                                    padded_start[:, None, None], 0), axis=0)
    dest_t = group_start + rank_t[:TOP_K]
    nk = n * TOP_K
    nb = -(-(nk + N_EXPERTS * EXPERT_BLOCK) // EXPERT_BLOCK)
    block_row0 = jnp.arange(nb, dtype=jnp.int32) * EXPERT_BLOCK
    block_e = jnp.minimum(
        jnp.sum((padded_end[None, :] <= block_row0[:, None]).astype(jnp.int32), axis=1),
        N_EXPERTS - 1).astype(jnp.int32)
    n_used = (padded_end[-1:] // EXPERT_BLOCK).astype(jnp.int32)
    j = jnp.arange(EXPERT_BLOCK, dtype=jnp.int32)[None, :]
    pad_idx = jnp.where(j < (padded - counts)[:, None], (padded_start + counts)[:, None] + j,
                        nb * EXPERT_BLOCK + j).astype(jnp.int32)

    xs = _sc_dispatch(h2p, dest_t, pad_idx, (nb + 1) * EXPERT_BLOCK)
    ybuf = _experts(block_e, n_used, xs, nb, w_gu, b_gu.reshape(N_EXPERTS, 1, -1), w_down,
                    b_down.reshape(N_EXPERTS, 1, -1))
    yg = _sc_gather(ybuf, dest_t.reshape(-1))
    out = _combine(yg.reshape(TOP_K, n, d // 2), x1, gate_p, row(g_final), tm=512)
    return out.reshape(b, s, d)


def kernel(x, g_mix, w_in, g_cq, g_ckv, w_uq, w_uv, w_q_idx, g_kidx, b_kidx, g_sgu, b_sgu,
           w_spatial, b_spatial, w_br_a, w_br_b, w_o, g_ffn, w_router, b_router, w_gu, b_gu,
           w_down, b_down, g_final):
    assert g_mix.shape[0] == 1, "single-layer block"
    return _layer(x, g_mix[0], w_in[0], g_cq[0], g_ckv[0], w_uq[0], w_uv[0], w_q_idx[0],
                  g_kidx[0], b_kidx[0], g_sgu[0], b_sgu[0], w_spatial[0], b_spatial[0],
                  w_br_a[0], w_br_b[0], w_o[0], g_ffn[0], w_router[0], b_router[0], w_gu[0],
                  b_gu[0], w_down[0], b_down[0], g_final)
```
